```python
import jax
import jax.numpy as jnp
from jax import lax
import numpy as np

D_MODEL = 4096
BATCH = 2
SEQ = 4096
DEPTH = 2

MEM_LEN = 256
CONV_W = 4
RMS_EPS = 1e-6
W_A = D_MODEL // 2
NB_A = 8
BS_A = W_A // NB_A
LRU_C = 8.0
W_B = D_MODEL // 2
N_B = 64
H_B = W_B // N_B
R_W = 96
R_A = 96
RWKV_GN_EPS = 64e-5
B_SHIFT_W = 3 * W_B + R_W + R_A
W_C = D_MODEL // 2
H_C = 4
DV_C = W_C // H_C
DQK_C = DV_C // 2
QK_W = H_C * DQK_C
CHUNK = 64
MLSTM_GN_EPS = 1e-6
H_X = 4
DH_X = 128
W_X = H_X * DH_X
N_BRANCH = 4
IN_SIZES = (W_A, W_A, B_SHIFT_W, W_B, 2 * QK_W, W_C, W_C, W_C, 2 * H_C, W_X, W_X, N_BRANCH * D_MODEL)
C_IN = sum(IN_SIZES)

kernel_name = 'hybrid_rglru_rwkv7_mlstm_memxattn'

F32 = jnp.float32


def rmsnorm(x, g, eps=RMS_EPS):
    xf = x.astype(F32)
    y = xf * lax.rsqrt(jnp.mean(xf * xf, axis=-1, keepdims=True) + eps)
    return (y * g.astype(F32)).astype(x.dtype)


def head_layernorm(y, w, b, eps):
    mu = jnp.mean(y, axis=-1, keepdims=True)
    yc = y - mu
    var = jnp.mean(yc * yc, axis=-1, keepdims=True)
    out = (yc * lax.rsqrt(var + eps)).reshape(y.shape[0], y.shape[1], -1) * w.astype(F32)
    if b is not None:
        out = out + b.astype(F32)
    return out


def causal_dwconv(u, w, b):
    k_w = w.shape[0]
    s = u.shape[1]
    up = jnp.pad(u, ((0, 0), (k_w - 1, 0), (0, 0)))
    return sum(up[:, j:j + s] * w[j] for j in range(k_w)) + b


def token_shift(p, mu):
    prev = jnp.pad(p, ((0, 0), (1, 0), (0, 0)))[:, :-1]
    return p + (prev - p) * mu


def _lin_combine(left, right):
    a1, b1 = left
    a2, b2 = right
    return a1 * a2, a2 * b1 + b2


def rg_lru_branch(u_in, gate, conv_w, conv_b, w_a, b_a, w_x, b_x, lam):
    bsz, s, _ = u_in.shape
    u = causal_dwconv(u_in, conv_w, conv_b)
    ub = u.reshape(bsz, s, NB_A, BS_A)
    r = jax.nn.sigmoid(jnp.einsum('bsnc,ncd->bsnd', ub, w_a).reshape(bsz, s, W_A) + b_a)
    i = jax.nn.sigmoid(jnp.einsum('bsnc,ncd->bsnd', ub, w_x).reshape(bsz, s, W_A) + b_x)
    log_a = (-LRU_C * r.astype(F32) * jax.nn.softplus(-lam.astype(F32)))
    a = jnp.exp(log_a)
    mult = jnp.sqrt(-jnp.expm1(2.0 * log_a))
    bterm = mult * (i * u).astype(F32)
    _, h = lax.associative_scan(_lin_combine, (a, bterm), axis=1)
    return (h * jax.nn.silu(gate.astype(F32))).astype(u_in.dtype)


def rwkv7_branch(p, gate, mu, w0, w_up, a0, a_up, k_k, k_a, r_k, gn_w, gn_b):
    bsz, s, _ = p.shape
    q = token_shift(p, mu)
    r, k, v, wd, ad = jnp.split(q, [W_B, 2 * W_B, 3 * W_B, 3 * W_B + R_W], axis=-1)
    w_log = -jax.nn.softplus(-(w0 + jnp.tanh(wd) @ w_up)) - 0.5
    decay = jnp.exp(-jnp.exp(w_log.astype(F32)))
    a = jax.nn.sigmoid(a0 + ad @ a_up)

    def heads(t):
        return t.astype(F32).reshape(bsz, s, H_B, N_B)

    r, k, v, decay, a = heads(r), heads(k), heads(v), heads(decay), heads(a)
    kk = k * k_k.astype(F32).reshape(H_B, N_B)
    kk = kk / jnp.maximum(jnp.sqrt(jnp.sum(kk * kk, axis=-1, keepdims=True)), 1e-12)
    k = k * (1.0 + (a - 1.0) * k_a.astype(F32).reshape(H_B, N_B))
    kka = kk * a

    def step(state, inp):
        r_t, w_t, k_t, v_t, kk_t, kka_t = inp
        sa = jnp.einsum('bhij,bhj->bhi', state, kk_t)
        state = (state * w_t[:, :, None, :] - sa[..., None] * kka_t[:, :, None, :]
                 + v_t[..., None] * k_t[:, :, None, :])
        return state, jnp.einsum('bhij,bhj->bhi', state, r_t)

    xs = tuple(jnp.moveaxis(t, 1, 0) for t in (r, decay, k, v, kk, kka))
    state0 = jnp.zeros((bsz, H_B, N_B, N_B), F32)
    _, y = lax.scan(step, state0, xs)
    y = head_layernorm(jnp.moveaxis(y, 0, 1), gn_w, gn_b, RWKV_GN_EPS)
    bonus = (jnp.sum(r * k * r_k.astype(F32), axis=-1, keepdims=True) * v).reshape(bsz, s, W_B)
    return ((y + bonus) * jax.nn.silu(gate.astype(F32))).astype(p.dtype)


def mlstm_branch(qk_in, v_in, o_in, gate, if_in, conv_w, conv_b, b_i, b_f, gn_w):
    bsz, s, _ = v_in.shape
    n_chunks = s // CHUNK
    qk = jax.nn.silu(causal_dwconv(qk_in, conv_w, conv_b))
    q, k = jnp.split(qk, 2, axis=-1)
    q = q.reshape(bsz, s, H_C, DQK_C)
    k = k.reshape(bsz, s, H_C, DQK_C) * (DQK_C ** -0.5)
    v = v_in.reshape(bsz, s, H_C, DV_C)
    i_pre, f_pre = jnp.split(if_in, 2, axis=-1)
    log_i = (i_pre + b_i).astype(F32)
    log_f = jax.nn.log_sigmoid((f_pre + b_f).astype(F32))

    def to_chunks(t):
        t = t.astype(F32).reshape(bsz, n_chunks, CHUNK, H_C, *t.shape[3:])
        return jnp.moveaxis(t, (1, 3), (0, 2))

    causal = jnp.tril(jnp.ones((CHUNK, CHUNK), dtype=bool))

    def chunk_step(carry, inp):
        c_st, n_st, m_st = carry
        qc, kc, vc, li, lf = inp
        bcum = jnp.cumsum(lf, axis=-1)
        g_tot = bcum[..., -1]
        dmat = bcum[..., :, None] - bcum[..., None, :] + li[..., None, :]
        dmat = jnp.where(causal, dmat, -jnp.inf)
        inter = bcum + m_st[..., None]
        m_t = jnp.maximum(inter, jnp.max(dmat, axis=-1))
        w_intra = jnp.exp(dmat - m_t[..., None])
        w_inter = jnp.exp(inter - m_t)
        sc = jnp.einsum('bhld,bhsd->bhls', qc, kc) * w_intra
        num = (w_inter[..., None] * jnp.einsum('bhld,bhde->bhle', qc, c_st)
               + jnp.einsum('bhls,bhse->bhle', sc, vc))
        den = w_inter * jnp.einsum('bhld,bhd->bhl', qc, n_st) + jnp.sum(sc, axis=-1)
        h = num / jnp.maximum(jnp.abs(den), jnp.exp(-m_t))[..., None]
        le = g_tot[..., None] - bcum + li
        m_new = jnp.maximum(g_tot + m_st, jnp.max(le, axis=-1))
        keep = jnp.exp(g_tot + m_st - m_new)
        wk = jnp.exp(le - m_new[..., None])
        c_st = keep[..., None, None] * c_st + jnp.einsum('bhs,bhsd,bhse->bhde', wk, kc, vc)
        n_st = keep[..., None] * n_st + jnp.einsum('bhs,bhsd->bhd', wk, kc)
        return (c_st, n_st, m_new), h

    c0 = jnp.zeros((bsz, H_C, DQK_C, DV_C), F32)
    n0 = jnp.zeros((bsz, H_C, DQK_C), F32)
    m0 = jnp.full((bsz, H_C), -jnp.inf, F32)
    _, h = lax.scan(chunk_step, (c0, n0, m0), tuple(to_chunks(t) for t in (q, k, v, log_i, log_f)))
    h = jnp.moveaxis(h, (0, 2), (1, 3)).reshape(bsz, s, H_C, DV_C)
    o = jax.nn.sigmoid(o_in.astype(F32)).reshape(bsz, s, H_C, DV_C)
    hn = head_layernorm(h * o, gn_w, None, MLSTM_GN_EPS)
    return (hn * jax.nn.silu(gate.astype(F32))).astype(v_in.dtype)


def memory_xattn_branch(q_in, gate, mem_n, w_kv):
    bsz, s, _ = q_in.shape
    m_len = mem_n.shape[1]
    k, v = jnp.split(mem_n @ w_kv, 2, axis=-1)
    k = k.reshape(bsz, m_len, H_X, DH_X)
    v = v.reshape(bsz, m_len, H_X, DH_X)
    q = q_in.reshape(bsz, s, H_X, DH_X)
    logits = jnp.einsum('bshd,bmhd->bhsm', q, k).astype(F32) * (DH_X ** -0.5)
    probs = jax.nn.softmax(logits, axis=-1).astype(v.dtype)
    o = jnp.einsum('bhsm,bmhd->bshd', probs, v).reshape(bsz, s, W_X)
    return (o * jax.nn.silu(gate)).astype(q_in.dtype)


def hybrid_layer(x, mem, norm_g, mem_norm_g, w_in,
                 lru_conv_w, lru_conv_b, lru_wa, lru_ba, lru_wx, lru_bx, lru_lambda,
                 rwkv_mu, rwkv_w0, rwkv_w_up, rwkv_a0, rwkv_a_up, rwkv_k_k, rwkv_k_a, rwkv_r_k,
                 rwkv_gn_w, rwkv_gn_b,
                 mlstm_conv_w, mlstm_conv_b, mlstm_b_i, mlstm_b_f, mlstm_gn_w,
                 xattn_w_kv, w_branch_a, w_branch_b, w_branch_c, w_branch_x, w_out):
    bsz, s, d = x.shape
    h = rmsnorm(x, norm_g)
    proj = h @ w_in
    cuts = [int(c) for c in np.cumsum(IN_SIZES)[:-1]]
    (a_x, a_g, b_s, b_g, c_qk, c_v, c_o, c_g, c_if, x_q, x_g,
     gate_logits) = jnp.split(proj, cuts, axis=-1)
    y_a = rg_lru_branch(a_x, a_g, lru_conv_w, lru_conv_b, lru_wa, lru_ba, lru_wx, lru_bx, lru_lambda)
    y_b = rwkv7_branch(b_s, b_g, rwkv_mu, rwkv_w0, rwkv_w_up, rwkv_a0, rwkv_a_up,
                       rwkv_k_k, rwkv_k_a, rwkv_r_k, rwkv_gn_w, rwkv_gn_b)
    y_c = mlstm_branch(c_qk, c_v, c_o, c_g, c_if, mlstm_conv_w, mlstm_conv_b,
                       mlstm_b_i, mlstm_b_f, mlstm_gn_w)
    y_x = memory_xattn_branch(x_q, x_g, rmsnorm(mem, mem_norm_g), xattn_w_kv)
    gates = jax.nn.sigmoid(gate_logits.reshape(bsz, s, N_BRANCH, d))
    merged = (gates[:, :, 0] * (y_a @ w_branch_a) + gates[:, :, 1] * (y_b @ w_branch_b)
              + gates[:, :, 2] * (y_c @ w_branch_c) + gates[:, :, 3] * (y_x @ w_branch_x))
    return x + merged @ w_out


def setup_inputs(seed: int = 0) -> dict:
    key = jax.random.key(seed)
    ks = iter(jax.random.split(key, 40))
    L, D = DEPTH, D_MODEL

    def nrm(shape, scale):
        return scale * jax.random.normal(next(ks), shape, F32)

    def uni(shape, lo, hi):
        return jax.random.uniform(next(ks), shape, F32, lo, hi)

    u_lru = uni((L, W_A), 0.9, 0.999) ** (1.0 / LRU_C)
    lru_lambda = jnp.log(u_lru) - jnp.log1p(-u_lru)
    return {
        'x': nrm((BATCH, SEQ, D), 1.0),
        'mem': nrm((BATCH, MEM_LEN, D), 1.0),
        'norm_g': 1.0 + nrm((L, D), 0.01),
        'mem_norm_g': 1.0 + nrm((L, D), 0.01),
        'w_in': nrm((L, D, C_IN), D ** -0.5),
        'lru_conv_w': nrm((L, CONV_W, W_A), CONV_W ** -0.5),
        'lru_conv_b': nrm((L, W_A), 0.01),
        'lru_wa': nrm((L, NB_A, BS_A, BS_A), BS_A ** -0.5),
        'lru_ba': nrm((L, W_A), 0.01),
        'lru_wx': nrm((L, NB_A, BS_A, BS_A), BS_A ** -0.5),
        'lru_bx': nrm((L, W_A), 0.01),
        'lru_lambda': lru_lambda,
        'rwkv_mu': uni((L, B_SHIFT_W), 0.0, 1.0),
        'rwkv_w0': uni((L, W_B), -6.0, -1.0),
        'rwkv_w_up': nrm((L, R_W, W_B), 0.5 * R_W ** -0.5),
        'rwkv_a0': nrm((L, W_B), 0.1),
        'rwkv_a_up': nrm((L, R_A, W_B), R_A ** -0.5),
        'rwkv_k_k': 0.85 + nrm((L, W_B), 0.05),
        'rwkv_k_a': 1.0 + nrm((L, W_B), 0.05),
        'rwkv_r_k': nrm((L, H_B, N_B), 0.1),
        'rwkv_gn_w': 1.0 + nrm((L, W_B), 0.01),
        'rwkv_gn_b': nrm((L, W_B), 0.01),
        'mlstm_conv_w': nrm((L, CONV_W, 2 * QK_W), CONV_W ** -0.5),
        'mlstm_conv_b': nrm((L, 2 * QK_W), 0.01),
        'mlstm_b_i': nrm((L, H_C), 0.1),
        'mlstm_b_f': uni((L, H_C), 3.0, 6.0),
        'mlstm_gn_w': 1.0 + nrm((L, W_C), 0.01),
        'xattn_w_kv': nrm((L, D, 2 * W_X), D ** -0.5),
        'w_branch_a': nrm((L, W_A, D), W_A ** -0.5),
        'w_branch_b': nrm((L, W_B, D), W_B ** -0.5),
        'w_branch_c': nrm((L, W_C, D), W_C ** -0.5),
        'w_branch_x': nrm((L, W_X, D), W_X ** -0.5),
        'w_out': nrm((L, D, D), D ** -0.5),
        'final_norm_g': 1.0 + nrm((D,), 0.01),
    }


def reference(x, mem, norm_g, mem_norm_g, w_in,
              lru_conv_w, lru_conv_b, lru_wa, lru_ba, lru_wx, lru_bx, lru_lambda,
              rwkv_mu, rwkv_w0, rwkv_w_up, rwkv_a0, rwkv_a_up, rwkv_k_k, rwkv_k_a, rwkv_r_k,
              rwkv_gn_w, rwkv_gn_b,
              mlstm_conv_w, mlstm_conv_b, mlstm_b_i, mlstm_b_f, mlstm_gn_w,
              xattn_w_kv, w_branch_a, w_branch_b, w_branch_c, w_branch_x, w_out, final_norm_g):
    for l in range(DEPTH):
        x = hybrid_layer(x, mem, norm_g[l], mem_norm_g[l], w_in[l],
                         lru_conv_w[l], lru_conv_b[l], lru_wa[l], lru_ba[l], lru_wx[l], lru_bx[l],
                         lru_lambda[l],
                         rwkv_mu[l], rwkv_w0[l], rwkv_w_up[l], rwkv_a0[l], rwkv_a_up[l],
                         rwkv_k_k[l], rwkv_k_a[l], rwkv_r_k[l], rwkv_gn_w[l], rwkv_gn_b[l],
                         mlstm_conv_w[l], mlstm_conv_b[l], mlstm_b_i[l], mlstm_b_f[l], mlstm_gn_w[l],
                         xattn_w_kv[l], w_branch_a[l], w_branch_b[l], w_branch_c[l], w_branch_x[l],
                         w_out[l])
    return rmsnorm(x, final_norm_g)
```

```python
import functools

import jax
import jax.numpy as jnp
import numpy as np
from jax import lax
from jax.experimental import pallas as pl
from jax.experimental.pallas import tpu as pltpu

F32 = jnp.float32
BF16 = jnp.bfloat16
HIGHEST = lax.Precision.HIGHEST

LANES = 128
SUBLANES = 8
VMEM_LIMIT = 56 * 1024 * 1024

RMS_EPS = 1e-6
LRU_C = 8.0
RWKV_GN_EPS = 64e-5
MLSTM_GN_EPS = 1e-6
DH_X = 128


def _cparams(*sem):
    return pltpu.CompilerParams(dimension_semantics=sem, vmem_limit_bytes=VMEM_LIMIT)


def _tile(n, want):
    t = min(n, want)
    assert n % t == 0, (n, want)
    return t


def _silu(x):
    return x * jax.nn.sigmoid(x)


def _rmsnorm_kernel(x_ref, g_ref, o_ref):
    x = x_ref[...]
    ms = jnp.mean(x * x, axis=-1, keepdims=True)
    o_ref[...] = (x * lax.rsqrt(ms + RMS_EPS) * g_ref[...]).astype(o_ref.dtype)


def rmsnorm(x2d, g, out_dtype):
    m, d = x2d.shape
    tm = _tile(m, 256)
    return pl.pallas_call(
        _rmsnorm_kernel,
        grid=(m // tm,),
        in_specs=[pl.BlockSpec((tm, d), lambda i: (i, 0)),
                  pl.BlockSpec((1, d), lambda i: (0, 0))],
        out_specs=pl.BlockSpec((tm, d), lambda i: (i, 0)),
        out_shape=jax.ShapeDtypeStruct((m, d), out_dtype),
        compiler_params=_cparams("parallel"),
        name="rmsnorm",
    )(x2d, g.reshape(1, d))


def _mm_kernel(a_ref, b_ref, o_ref):
    o_ref[...] = jnp.dot(a_ref[...], b_ref[...], preferred_element_type=F32).astype(o_ref.dtype)


def _mm_res_kernel(a_ref, b_ref, r_ref, o_ref):
    acc = jnp.dot(a_ref[...], b_ref[...], preferred_element_type=F32)
    o_ref[...] = (r_ref[...] + acc).astype(o_ref.dtype)


def matmul(a, b, out_dtype, tm, tn, residual=None, name="matmul"):
    m, k = a.shape
    _, n = b.shape
    tm, tn = _tile(m, tm), _tile(n, tn)
    in_specs = [pl.BlockSpec((tm, k), lambda j, i: (i, 0)),
                pl.BlockSpec((k, tn), lambda j, i: (0, j))]
    args = [a, b]
    kern = _mm_kernel
    if residual is not None:
        in_specs.append(pl.BlockSpec((tm, tn), lambda j, i: (i, j)))
        args.append(residual)
        kern = _mm_res_kernel
    return pl.pallas_call(
        kern,
        grid=(n // tn, m // tm),
        in_specs=in_specs,
        out_specs=pl.BlockSpec((tm, tn), lambda j, i: (i, j)),
        out_shape=jax.ShapeDtypeStruct((m, n), out_dtype),
        compiler_params=_cparams("parallel", "parallel"),
        name=name,
    )(*args)


def _merge_kernel(ya_ref, yb_ref, yc_ref, yx_ref, wa_ref, wb_ref, wc_ref, wx_ref,
                  g0_ref, g1_ref, g2_ref, g3_ref, o_ref):
    def term(y_ref, w_ref, g_ref):
        return jax.nn.sigmoid(g_ref[...]) * jnp.dot(y_ref[...], w_ref[...], preferred_element_type=F32)

    acc = term(ya_ref, wa_ref, g0_ref)
    acc = acc + term(yb_ref, wb_ref, g1_ref)
    acc = acc + term(yc_ref, wc_ref, g2_ref)
    acc = acc + term(yx_ref, wx_ref, g3_ref)
    o_ref[...] = acc.astype(o_ref.dtype)


def merge(ys, ws, proj, gate_off, d):
    m = proj.shape[0]
    tm, tn = _tile(m, 512), _tile(d, 512)
    assert gate_off % tn == 0
    y_specs = [pl.BlockSpec((tm, y.shape[1]), lambda j, i: (i, 0)) for y in ys]
    w_specs = [pl.BlockSpec((w.shape[0], tn), lambda j, i: (0, j)) for w in ws]
    g_specs = [pl.BlockSpec((tm, tn), functools.partial(
        lambda j, i, base: (i, base + j), base=(gate_off + k * d) // tn)) for k in range(4)]
    return pl.pallas_call(
        _merge_kernel,
        grid=(d // tn, m // tm),
        in_specs=y_specs + w_specs + g_specs,
        out_specs=pl.BlockSpec((tm, tn), lambda j, i: (i, j)),
        out_shape=jax.ShapeDtypeStruct((m, d), BF16),
        compiler_params=_cparams("parallel", "parallel"),
        name="merge",
    )(*ys, *ws, proj, proj, proj, proj)


def _shift_rows(x, shift, hist):
    rolled = pltpu.roll(x, shift, axis=0)
    head = pltpu.roll(hist, shift, axis=0)
    row = lax.broadcasted_iota(jnp.int32, (SUBLANES, x.shape[1]), 0)
    fixed = jnp.where(row < shift, head, rolled[:SUBLANES])
    return jnp.concatenate([fixed, rolled[SUBLANES:]], axis=0)


def _causal_conv(x, hist, w, b):
    kw = w.shape[0]
    out = x * w[kw - 1:kw] + b
    for j in range(kw - 1):
        out = out + _shift_rows(x, kw - 1 - j, hist) * w[j:j + 1]
    return out


def _lru_kernel(ax_ref, ag_ref, cw_ref, cb_ref, wa_ref, ba_ref, wx_ref, bx_ref, lam_ref, o_ref,
                hist_sc, hcar_sc, a_sc, b_sc):
    t = pl.program_id(2)

    @pl.when(t == 0)
    def _():
        hist_sc[...] = jnp.zeros_like(hist_sc)
        hcar_sc[...] = jnp.zeros_like(hcar_sc)

    x = ax_ref[...]
    rows = x.shape[0]
    u = _causal_conv(x, hist_sc[...], cw_ref[...], cb_ref[...])
    hist_sc[...] = x[rows - SUBLANES:]
    ub = u.astype(BF16)
    r = jax.nn.sigmoid(jnp.dot(ub, wa_ref[...], preferred_element_type=F32) + ba_ref[...])
    i = jax.nn.sigmoid(jnp.dot(ub, wx_ref[...], preferred_element_type=F32) + bx_ref[...])
    log_a = -LRU_C * r * jax.nn.softplus(-lam_ref[...])
    a = jnp.exp(log_a)
    a_sc[...] = a
    b_sc[...] = jnp.sqrt(-jnp.tanh(log_a) * (a * a + 1.0)) * (i * u)

    row = lax.broadcasted_iota(jnp.int32, (SUBLANES, x.shape[1]), 0)

    def group(g, h_prev):
        sl = pl.ds(pl.multiple_of(g * SUBLANES, SUBLANES), SUBLANES)
        a8, b8 = a_sc[sl, :], b_sc[sl, :]
        for s in (1, 2, 4):
            a_sh = pltpu.roll(a8, s, axis=0)
            b_sh = pltpu.roll(b8, s, axis=0)
            keep = row >= s
            b8 = jnp.where(keep, a8 * b_sh + b8, b8)
            a8 = jnp.where(keep, a8 * a_sh, a8)
        h8 = a8 * h_prev + b8
        b_sc[sl, :] = h8
        return h8[SUBLANES - 1:SUBLANES]

    hcar_sc[...] = lax.fori_loop(0, rows // SUBLANES, group, hcar_sc[...])
    o_ref[...] = (b_sc[...] * _silu(ag_ref[...])).astype(o_ref.dtype)


def lru_branch(proj, off_x, off_g, bsz, s, cw, cb, wa, ba, wx, bx, lam):
    nb, bs, _ = wa.shape
    w = nb * bs
    tt = _tile(s, 512)
    nt = s // tt
    assert off_x % bs == 0 and off_g % bs == 0
    vec = lambda v: v.reshape(1, w)
    vspec = pl.BlockSpec((1, bs), lambda b, n, t: (0, n))
    return pl.pallas_call(
        _lru_kernel,
        grid=(bsz, nb, nt),
        in_specs=[
            pl.BlockSpec((tt, bs), lambda b, n, t: (b * nt + t, off_x // bs + n)),
            pl.BlockSpec((tt, bs), lambda b, n, t: (b * nt + t, off_g // bs + n)),
            pl.BlockSpec((cw.shape[0], bs), lambda b, n, t: (0, n)),
            vspec,
            pl.BlockSpec((None, bs, bs), lambda b, n, t: (n, 0, 0)),
            vspec,
            pl.BlockSpec((None, bs, bs), lambda b, n, t: (n, 0, 0)),
            vspec,
            vspec,
        ],
        out_specs=pl.BlockSpec((tt, bs), lambda b, n, t: (b * nt + t, n)),
        out_shape=jax.ShapeDtypeStruct((bsz * s, w), BF16),
        scratch_shapes=[pltpu.VMEM((SUBLANES, bs), F32), pltpu.VMEM((1, bs), F32),
                        pltpu.VMEM((tt, bs), F32), pltpu.VMEM((tt, bs), F32)],
        compiler_params=_cparams("parallel", "parallel", "arbitrary"),
        name="rg_lru",
    )(proj, proj, cw, vec(cb), wa.astype(BF16), vec(ba), wx.astype(BF16), vec(bx), vec(lam))


def _seg_sum(x, g):
    parts = [jnp.dot(x[:, c * LANES:(c + 1) * LANES], g, precision=HIGHEST, preferred_element_type=F32)
             for c in range(x.shape[1] // LANES)]
    return parts[0] if len(parts) == 1 else jnp.concatenate(parts, axis=1)


def _rwkv_prep_kernel(pr_ref, pk_ref, pv_ref, plo_ref, mur_ref, muk_ref, muv_ref, mulo_ref,
                      w0_ref, wup_ref, a0_ref, aup_ref, kk_ref, ka_ref, rk_ref, g_ref,
                      r_o, w_o, k_o, v_o, kk_o, kka_o, bonus_o,
                      lr_sc, lk_sc, lv_sc, llo_sc):
    t = pl.program_id(1)

    @pl.when(t == 0)
    def _():
        for sc in (lr_sc, lk_sc, lv_sc, llo_sc):
            sc[...] = jnp.zeros_like(sc)

    def shifted(p_ref, mu_ref, last_sc):
        p = p_ref[...]
        rows = p.shape[0]
        prev = _shift_rows(p, 1, last_sc[...])
        last_sc[...] = p[rows - SUBLANES:]
        return p + (prev - p) * mu_ref[...]

    r = shifted(pr_ref, mur_ref, lr_sc)
    k = shifted(pk_ref, muk_ref, lk_sc)
    v = shifted(pv_ref, muv_ref, lv_sc)
    lo = shifted(plo_ref, mulo_ref, llo_sc)

    w_pre = w0_ref[...] + jnp.dot(jnp.tanh(lo).astype(BF16), wup_ref[...], preferred_element_type=F32)
    w_log = -jax.nn.softplus(-w_pre) - 0.5
    decay = jnp.exp(-jnp.exp(w_log))
    a = jax.nn.sigmoid(a0_ref[...] + jnp.dot(lo.astype(BF16), aup_ref[...], preferred_element_type=F32))

    g = g_ref[...]
    kk = k * kk_ref[...]
    kk = kk / jnp.maximum(jnp.sqrt(_seg_sum(kk * kk, g)), 1e-12)
    k2 = k * (1.0 + (a - 1.0) * ka_ref[...])
    r_o[...] = r
    w_o[...] = decay
    k_o[...] = k2
    v_o[...] = v
    kk_o[...] = kk
    kka_o[...] = kk * a
    bonus_o[...] = _seg_sum(r * k2 * rk_ref[...], g) * v


def rwkv_prep(proj, offs, bsz, s, mu, w0, w_up, a0, a_up, k_k, k_a, r_k, n_head):
    off_r, off_k, off_v, off_lo, lo_w = offs
    r_w, w = w_up.shape
    r_a = a_up.shape[0]
    tt = _tile(s, 256)
    nt = s // tt
    assert off_r % w == 0 and off_k % w == 0 and off_v % w == 0 and off_lo % lo_w == 0
    assert LANES % n_head == 0 and w % LANES == 0
    vec = lambda x: x.reshape(1, -1)
    mu_lo = jnp.pad(mu[3 * w:], (0, lo_w - r_w - r_a))
    wup_pad = jnp.zeros((lo_w, w), F32).at[:r_w].set(w_up).astype(BF16)
    aup_pad = jnp.zeros((lo_w, w), F32).at[r_w:r_w + r_a].set(a_up).astype(BF16)
    gmat = jnp.asarray(np.kron(np.eye(LANES // n_head), np.ones((n_head, n_head))), F32)
    row_w = pl.BlockSpec((1, w), lambda b, t: (0, 0))
    blk = lambda off, width: pl.BlockSpec((tt, width), lambda b, t: (b * nt + t, off // width))
    out_spec = pl.BlockSpec((tt, w), lambda b, t: (b * nt + t, 0))
    out_shape = jax.ShapeDtypeStruct((bsz * s, w), F32)
    return pl.pallas_call(
        _rwkv_prep_kernel,
        grid=(bsz, nt),
        in_specs=[blk(off_r, w), blk(off_k, w), blk(off_v, w), blk(off_lo, lo_w),
                  row_w, row_w, row_w, pl.BlockSpec((1, lo_w), lambda b, t: (0, 0)),
                  row_w, pl.BlockSpec((lo_w, w), lambda b, t: (0, 0)),
                  row_w, pl.BlockSpec((lo_w, w), lambda b, t: (0, 0)),
                  row_w, row_w, row_w,
                  pl.BlockSpec((LANES, LANES), lambda b, t: (0, 0))],
        out_specs=[out_spec] * 7,
        out_shape=[out_shape] * 7,
        scratch_shapes=[pltpu.VMEM((SUBLANES, w), F32)] * 3 + [pltpu.VMEM((SUBLANES, lo_w), F32)],
        compiler_params=_cparams("parallel", "arbitrary"),
        name="rwkv_prep",
    )(proj, proj, proj, proj, vec(mu[:w]), vec(mu[w:2 * w]), vec(mu[2 * w:3 * w]), vec(mu_lo),
      vec(w0), wup_pad, vec(a0), aup_pad, vec(k_k), vec(k_a), vec(r_k), gmat)


def _rwkv_scan_kernel(kk_ref, w_ref, kka_ref, k_ref, r_ref, v_ref, y_ref, s_sc, *, groups):
    c = pl.program_id(0)

    @pl.when(c == 0)
    def _():
        s_sc[...] = jnp.zeros_like(s_sc)

    steps = kk_ref.shape[0]
    n_rows = s_sc.shape[0]

    def group_body(gi, carry):
        base = gi * groups
        init = tuple(s_sc[base + g] for g in range(groups))

        def step(t, states):
            kk, w, kka, k, r = kk_ref[t], w_ref[t], kka_ref[t], k_ref[t], r_ref[t]
            out = []
            for g in range(groups):
                st = states[g]
                vrow = v_ref[t, pl.ds(base + g, 1), :]
                sa = jnp.sum(st * kk, axis=0, keepdims=True)
                st = st * w - sa * kka + vrow * k
                y_ref[t, pl.ds(base + g, 1), :] = jnp.sum(st * r, axis=0, keepdims=True)
                out.append(st)
            return tuple(out)

        final = lax.fori_loop(0, steps, step, init)
        for g in range(groups):
            s_sc[base + g] = final[g]
        return carry

    lax.fori_loop(0, n_rows // groups, group_body, 0)


def rwkv_scan(kk_e, w_e, kka_e, k_e, r_e, v_e):
    s, n_k, lanes = kk_e.shape
    n_rows = v_e.shape[1]
    tc = _tile(s, 64)
    groups = min(4, n_rows)
    vec_spec = pl.BlockSpec((tc, n_k, lanes), lambda c: (c, 0, 0))
    row_spec = pl.BlockSpec((tc, n_rows, lanes), lambda c: (c, 0, 0))
    return pl.pallas_call(
        functools.partial(_rwkv_scan_kernel, groups=groups),
        grid=(s // tc,),
        in_specs=[vec_spec] * 5 + [row_spec],
        out_specs=row_spec,
        out_shape=jax.ShapeDtypeStruct((s, n_rows, lanes), F32),
        scratch_shapes=[pltpu.VMEM((n_rows, n_k, lanes), F32)],
        compiler_params=_cparams("arbitrary"),
        name="rwkv_scan",
    )(kk_e, w_e, kka_e, k_e, r_e, v_e)


def _rwkv_out_kernel(y_ref, bonus_ref, gate_ref, gw_ref, gb_ref, g_ref, o_ref, *, n_head):
    y = y_ref[...]
    g = g_ref[...] * (1.0 / n_head)
    mu = _seg_sum(y, g)
    yc = y - mu
    var = _seg_sum(yc * yc, g)
    yn = yc * lax.rsqrt(var + RWKV_GN_EPS) * gw_ref[...] + gb_ref[...]
    o_ref[...] = ((yn + bonus_ref[...]) * _silu(gate_ref[...])).astype(o_ref.dtype)


def rwkv_out(y, bonus, proj, off_g, gn_w, gn_b, n_head):
    m, w = y.shape
    tt = _tile(m, 256)
    assert off_g % w == 0
    gmat = jnp.asarray(np.kron(np.eye(LANES // n_head), np.ones((n_head, n_head))), F32)
    blk = pl.BlockSpec((tt, w), lambda i: (i, 0))
    row = pl.BlockSpec((1, w), lambda i: (0, 0))
    return pl.pallas_call(
        functools.partial(_rwkv_out_kernel, n_head=n_head),
        grid=(m // tt,),
        in_specs=[blk, blk, pl.BlockSpec((tt, w), lambda i: (i, off_g // w)), row, row,
                  pl.BlockSpec((LANES, LANES), lambda i: (0, 0))],
        out_specs=blk,
        out_shape=jax.ShapeDtypeStruct((m, w), BF16),
        compiler_params=_cparams("parallel"),
        name="rwkv_out",
    )(y, bonus, proj, gn_w.reshape(1, w), gn_b.reshape(1, w), gmat)


def _to_chain_lanes(x, bsz, s, n_heads, n_head):
    dup = LANES // (bsz * n_heads)
    e = jnp.transpose(x.reshape(bsz, s, n_heads, n_head), (1, 3, 0, 2)).reshape(s, n_head, bsz * n_heads)
    return jnp.concatenate([e] * dup, axis=-1)


def _rows_to_chain_lanes(v, bsz, s, n_heads, n_head):
    dup = LANES // (bsz * n_heads)
    e = v.reshape(bsz, s, n_heads, dup, n_head // dup)
    return jnp.transpose(e, (1, 4, 3, 0, 2)).reshape(s, n_head // dup, LANES)


def _rows_from_chain_lanes(y_e, bsz, s, n_heads, n_head):
    dup = LANES // (bsz * n_heads)
    e = y_e.reshape(s, n_head // dup, dup, bsz, n_heads)
    return jnp.transpose(e, (3, 0, 4, 2, 1)).reshape(bsz * s, n_heads * n_head)


def _mlstm_kernel(qk_ref, v_ref, o_ref, g_ref, if_ref, cw_ref, cb_ref, bias_ref, gw_ref, tril_ref,
                  out_ref, hist_sc, c_sc, m_sc, *, n_heads, dqk, dv):
    t = pl.program_id(1)

    @pl.when(t == 0)
    def _():
        hist_sc[...] = jnp.zeros_like(hist_sc)
        c_sc[...] = jnp.zeros_like(c_sc)
        m_sc[...] = jnp.full_like(m_sc, -jnp.inf)

    qk_in = qk_ref[...]
    rows = qk_in.shape[0]
    qk = _silu(_causal_conv(qk_in, hist_sc[...], cw_ref[...], cb_ref[...]))
    hist_sc[...] = qk_in[rows - SUBLANES:]

    pre = if_ref[...] + bias_ref[...]
    li = pre
    lf = jax.nn.log_sigmoid(pltpu.roll(pre, LANES - n_heads, axis=1))
    tril = tril_ref[...]
    bcum = jnp.dot(tril, lf, precision=HIGHEST, preferred_element_type=F32)
    g_tot = bcum[rows - 1:rows]
    m_prev = m_sc[...]
    le = g_tot - bcum + li
    m_new = jnp.maximum(g_tot + m_prev, jnp.max(le, axis=0, keepdims=True))
    keep = jnp.exp(g_tot + m_prev - m_new)
    wk = jnp.exp(le - m_new)
    m_sc[...] = m_new
    bcum_t = bcum.T
    li_t = li.T
    causal = tril > 0.5
    ones_blk = (lax.broadcasted_iota(jnp.int32, (rows, LANES), 1) == 0).astype(F32)

    for h in range(n_heads):
        q = qk[:, h * dqk:(h + 1) * dqk]
        k = qk[:, (n_heads + h) * dqk:(n_heads + h + 1) * dqk] * (dqk ** -0.5)
        v = v_ref[:, h * dv:(h + 1) * dv]
        v_aug = jnp.concatenate([v, ones_blk], axis=1).astype(BF16)
        b_col = bcum[:, h:h + 1]
        dmat = b_col - bcum_t[h:h + 1, :] + li_t[h:h + 1, :]
        dmat = jnp.where(causal, dmat, -jnp.inf)
        inter = b_col + m_prev[:, h:h + 1]
        m_t = jnp.maximum(inter, jnp.max(dmat, axis=-1, keepdims=True))
        w_intra = jnp.exp(dmat - m_t)
        w_inter = jnp.exp(inter - m_t)
        qb = q.astype(BF16)
        sc = lax.dot_general(qb, k.astype(BF16), (((1,), (1,)), ((), ())),
                             preferred_element_type=F32) * w_intra
        c_aug = c_sc[h]
        tot = (w_inter * jnp.dot(qb, c_aug.astype(BF16), preferred_element_type=F32)
               + jnp.dot(sc.astype(BF16), v_aug, preferred_element_type=F32))
        num = tot[:, :dv]
        den = tot[:, dv:dv + 1]
        hh = num / jnp.maximum(jnp.abs(den), jnp.exp(-m_t))
        kw = (k * wk[:, h:h + 1]).astype(BF16)
        c_sc[h] = keep[:, h:h + 1] * c_aug + lax.dot_general(
            kw, v_aug, (((0,), (0,)), ((), ())), preferred_element_type=F32)

        ho = hh * jax.nn.sigmoid(o_ref[:, h * dv:(h + 1) * dv])
        mu = jnp.mean(ho, axis=-1, keepdims=True)
        hc = ho - mu
        var = jnp.mean(hc * hc, axis=-1, keepdims=True)
        hn = hc * lax.rsqrt(var + MLSTM_GN_EPS) * gw_ref[:, h * dv:(h + 1) * dv]
        out_ref[:, h * dv:(h + 1) * dv] = (hn * _silu(g_ref[:, h * dv:(h + 1) * dv])).astype(out_ref.dtype)


def mlstm_branch(proj, offs, bsz, s, conv_w, conv_b, b_i, b_f, gn_w, w_c):
    off_qk, off_v, off_o, off_g, off_if, if_w = offs
    n_heads = b_i.shape[0]
    qk2 = conv_w.shape[1]
    dqk = qk2 // 2 // n_heads
    dv = w_c // n_heads
    ll = _tile(s, 256)
    nt = s // ll
    assert off_qk % qk2 == 0 and off_v % w_c == 0 and off_o % w_c == 0 and off_g % w_c == 0
    assert off_if % if_w == 0 and if_w == LANES and 2 * n_heads <= LANES
    bias = jnp.pad(jnp.concatenate([b_i, b_f]), (0, LANES - 2 * n_heads)).reshape(1, LANES)
    tril = jnp.asarray(np.tril(np.ones((ll, ll))), F32)
    blk = lambda off, width: pl.BlockSpec((ll, width), lambda b, t: (b * nt + t, off // width))
    full = lambda shape: pl.BlockSpec(shape, lambda b, t: (0,) * len(shape))
    return pl.pallas_call(
        functools.partial(_mlstm_kernel, n_heads=n_heads, dqk=dqk, dv=dv),
        grid=(bsz, nt),
        in_specs=[blk(off_qk, qk2), blk(off_v, w_c), blk(off_o, w_c), blk(off_g, w_c), blk(off_if, if_w),
                  full(conv_w.shape), full((1, qk2)), full((1, LANES)), full((1, w_c)), full((ll, ll))],
        out_specs=pl.BlockSpec((ll, w_c), lambda b, t: (b * nt + t, 0)),
        out_shape=jax.ShapeDtypeStruct((bsz * s, w_c), BF16),
        scratch_shapes=[pltpu.VMEM((SUBLANES, qk2), F32),
                        pltpu.VMEM((n_heads, dqk, dv + LANES), F32),
                        pltpu.VMEM((1, LANES), F32)],
        compiler_params=_cparams("parallel", "arbitrary"),
        name="mlstm",
    )(proj, proj, proj, proj, proj, conv_w, conv_b.reshape(1, qk2), bias, gn_w.reshape(1, w_c), tril)


def _xattn_kernel(q_ref, gate_ref, kv_ref, o_ref, *, n_heads):
    w_x = q_ref.shape[1]
    for h in range(n_heads):
        sl = slice(h * DH_X, (h + 1) * DH_X)
        q = q_ref[:, sl].astype(BF16)
        k = kv_ref[:, sl].astype(BF16)
        v = kv_ref[:, w_x + h * DH_X:w_x + (h + 1) * DH_X].astype(BF16)
        logits = lax.dot_general(q, k, (((1,), (1,)), ((), ())), preferred_element_type=F32) * (DH_X ** -0.5)
        e = jnp.exp(logits - jnp.max(logits, axis=-1, keepdims=True))
        probs = e / jnp.sum(e, axis=-1, keepdims=True)
        o = jnp.dot(probs.astype(BF16), v, preferred_element_type=F32)
        o_ref[:, sl] = (o * _silu(gate_ref[:, sl])).astype(o_ref.dtype)


def xattn_branch(proj, off_q, off_g, bsz, s, kv, w_x):
    m_len = kv.shape[0] // bsz
    tt = _tile(s, 512)
    nt = s // tt
    assert off_q % w_x == 0 and off_g % w_x == 0 and w_x % DH_X == 0
    return pl.pallas_call(
        functools.partial(_xattn_kernel, n_heads=w_x // DH_X),
        grid=(bsz, nt),
        in_specs=[pl.BlockSpec((tt, w_x), lambda b, t: (b * nt + t, off_q // w_x)),
                  pl.BlockSpec((tt, w_x), lambda b, t: (b * nt + t, off_g // w_x)),
                  pl.BlockSpec((m_len, 2 * w_x), lambda b, t: (b, 0))],
        out_specs=pl.BlockSpec((tt, w_x), lambda b, t: (b * nt + t, 0)),
        out_shape=jax.ShapeDtypeStruct((bsz * s, w_x), BF16),
        compiler_params=_cparams("parallel", "parallel"),
        name="mem_xattn",
    )(proj, proj, kv)


def _round_up(n, m):
    return (n + m - 1) // m * m


def _proj_layout(sizes, w_b, lo_w, if_w, n_tile):
    src = np.concatenate([[0], np.cumsum(sizes)])
    seg = {name: (int(src[i]), int(sizes[i])) for i, name in enumerate(
        ("a_x", "a_g", "b_s", "b_g", "c_qk", "c_v", "c_o", "c_g", "c_if", "x_q", "x_g", "gates"))}
    b0 = seg["b_s"][0]
    seg["b_r"], seg["b_k"], seg["b_v"] = (b0, w_b), (b0 + w_b, w_b), (b0 + 2 * w_b, w_b)
    seg["b_lo"] = (b0 + 3 * w_b, seg["b_s"][1] - 3 * w_b)
    order = ("a_x", "a_g", "b_g", "c_qk", "c_v", "c_o", "c_g", "b_r", "b_k", "b_v", "gates",
             "x_q", "x_g", "b_lo", "c_if")
    padded = {"b_lo": lo_w, "c_if": if_w}
    offs, pieces, pos = {}, [], 0
    for name in order:
        start, width = seg[name]
        offs[name] = pos
        pieces.append((start, width, padded.get(name, width)))
        pos += padded.get(name, width)
    total = _round_up(pos, n_tile)
    return offs, pieces, total


def _build_w_in(w_in, pieces, total):
    cols, pos = [], 0
    for start, width, padded in pieces:
        cols.append(w_in[:, start:start + width])
        if padded > width:
            cols.append(jnp.zeros((w_in.shape[0], padded - width), w_in.dtype))
        pos += padded
    if total > pos:
        cols.append(jnp.zeros((w_in.shape[0], total - pos), w_in.dtype))
    return jnp.concatenate(cols, axis=1).astype(BF16)


def _layer(x2d, memn_src, bsz, s, p):
    d = x2d.shape[1]
    w_a = p["lru_lambda"].shape[0]
    n_heads_b, n_head_b = p["rwkv_r_k"].shape
    w_b = n_heads_b * n_head_b
    w_c = p["w_branch_c"].shape[0]
    w_x = p["w_branch_x"].shape[0]
    qk2 = p["mlstm_conv_w"].shape[1]
    n_heads_c = p["mlstm_b_i"].shape[0]
    b_shift_w = p["rwkv_mu"].shape[0]
    sizes = (w_a, w_a, b_shift_w, w_b, qk2, w_c, w_c, w_c, 2 * n_heads_c, w_x, w_x, 4 * d)
    lo_w = _round_up(b_shift_w - 3 * w_b, 256)
    n_tile = 768
    offs, pieces, total = _proj_layout(sizes, w_b, lo_w, LANES, n_tile)

    h = rmsnorm(x2d, p["norm_g"], BF16)
    proj = matmul(h, _build_w_in(p["w_in"], pieces, total), F32, 1024, n_tile, name="in_proj")

    y_a = lru_branch(proj, offs["a_x"], offs["a_g"], bsz, s, p["lru_conv_w"], p["lru_conv_b"],
                     p["lru_wa"], p["lru_ba"], p["lru_wx"], p["lru_bx"], p["lru_lambda"])

    r, w, k, v, kk, kka, bonus = rwkv_prep(
        proj, (offs["b_r"], offs["b_k"], offs["b_v"], offs["b_lo"], lo_w), bsz, s, p["rwkv_mu"],
        p["rwkv_w0"], p["rwkv_w_up"], p["rwkv_a0"], p["rwkv_a_up"], p["rwkv_k_k"], p["rwkv_k_a"],
        p["rwkv_r_k"], n_head_b)
    to_e = functools.partial(_to_chain_lanes, bsz=bsz, s=s, n_heads=n_heads_b, n_head=n_head_b)
    y_e = rwkv_scan(to_e(kk), to_e(w), to_e(kka), to_e(k), to_e(r),
                    _rows_to_chain_lanes(v, bsz, s, n_heads_b, n_head_b))
    y_b = rwkv_out(_rows_from_chain_lanes(y_e, bsz, s, n_heads_b, n_head_b), bonus, proj, offs["b_g"],
                   p["rwkv_gn_w"], p["rwkv_gn_b"], n_head_b)

    y_c = mlstm_branch(proj, (offs["c_qk"], offs["c_v"], offs["c_o"], offs["c_g"], offs["c_if"], LANES),
                       bsz, s, p["mlstm_conv_w"], p["mlstm_conv_b"], p["mlstm_b_i"], p["mlstm_b_f"],
                       p["mlstm_gn_w"], w_c)

    memn = rmsnorm(memn_src, p["mem_norm_g"], BF16)
    kv = matmul(memn, p["xattn_w_kv"].astype(BF16), F32, 512, 512, name="mem_kv")
    y_x = xattn_branch(proj, offs["x_q"], offs["x_g"], bsz, s, kv, w_x)

    merged = merge((y_a, y_b, y_c, y_x),
                   tuple(p[n].astype(BF16) for n in ("w_branch_a", "w_branch_b", "w_branch_c", "w_branch_x")),
                   proj, offs["gates"], d)
    return matmul(merged, p["w_out"].astype(BF16), F32, 1024, 512, residual=x2d, name="out_proj")


_LAYER_PARAMS = (
    "norm_g", "mem_norm_g", "w_in", "lru_conv_w", "lru_conv_b", "lru_wa", "lru_ba", "lru_wx", "lru_bx",
    "lru_lambda", "rwkv_mu", "rwkv_w0", "rwkv_w_up", "rwkv_a0", "rwkv_a_up", "rwkv_k_k", "rwkv_k_a",
    "rwkv_r_k", "rwkv_gn_w", "rwkv_gn_b", "mlstm_conv_w", "mlstm_conv_b", "mlstm_b_i", "mlstm_b_f",
    "mlstm_gn_w", "xattn_w_kv", "w_branch_a", "w_branch_b", "w_branch_c", "w_branch_x", "w_out")


def kernel(x, mem, norm_g, mem_norm_g, w_in, lru_conv_w, lru_conv_b, lru_wa, lru_ba, lru_wx, lru_bx, lru_lambda, rwkv_mu, rwkv_w0, rwkv_w_up, rwkv_a0, rwkv_a_up, rwkv_k_k, rwkv_k_a, rwkv_r_k, rwkv_gn_w, rwkv_gn_b, mlstm_conv_w, mlstm_conv_b, mlstm_b_i, mlstm_b_f, mlstm_gn_w, xattn_w_kv, w_branch_a, w_branch_b, w_branch_c, w_branch_x, w_out, final_norm_g):
    stacked = dict(zip(_LAYER_PARAMS, (
        norm_g, mem_norm_g, w_in, lru_conv_w, lru_conv_b, lru_wa, lru_ba, lru_wx, lru_bx, lru_lambda,
        rwkv_mu, rwkv_w0, rwkv_w_up, rwkv_a0, rwkv_a_up, rwkv_k_k, rwkv_k_a, rwkv_r_k, rwkv_gn_w,
        rwkv_gn_b, mlstm_conv_w, mlstm_conv_b, mlstm_b_i, mlstm_b_f, mlstm_gn_w, xattn_w_kv,
        w_branch_a, w_branch_b, w_branch_c, w_branch_x, w_out)))
    bsz, s, d = x.shape
    x2d = x.reshape(bsz * s, d)
    mem2d = mem.reshape(-1, d)
    for layer in range(norm_g.shape[0]):
        x2d = _layer(x2d, mem2d, bsz, s, {name: val[layer] for name, val in stacked.items()})
    return rmsnorm(x2d, final_norm_g, x.dtype).reshape(bsz, s, d)
```

```python
import functools

import jax
import jax.numpy as jnp
import numpy as np
from jax import lax
from jax.experimental import pallas as pl
from jax.experimental.pallas import tpu as pltpu

F32 = jnp.float32
BF16 = jnp.bfloat16
HIGHEST = lax.Precision.HIGHEST

LANES = 128
SUBLANES = 8
VMEM_LIMIT = 56 * 1024 * 1024

RMS_EPS = 1e-6
LRU_C = 8.0
RWKV_GN_EPS = 64e-5
MLSTM_GN_EPS = 1e-6
DH_X = 128


def _cparams(*sem):
    return pltpu.CompilerParams(dimension_semantics=sem, vmem_limit_bytes=VMEM_LIMIT)


def _tile(n, want):
    t = min(n, want)
    assert n % t == 0, (n, want)
    return t


def _silu(x):
    return x * jax.nn.sigmoid(x)


def _rmsnorm_kernel(x_ref, g_ref, o_ref):
    x = x_ref[...]
    ms = jnp.mean(x * x, axis=-1, keepdims=True)
    o_ref[...] = (x * lax.rsqrt(ms + RMS_EPS) * g_ref[...]).astype(o_ref.dtype)


def rmsnorm(x2d, g, out_dtype):
    m, d = x2d.shape
    tm = _tile(m, 256)
    return pl.pallas_call(
        _rmsnorm_kernel,
        grid=(m // tm,),
        in_specs=[pl.BlockSpec((tm, d), lambda i: (i, 0)),
                  pl.BlockSpec((1, d), lambda i: (0, 0))],
        out_specs=pl.BlockSpec((tm, d), lambda i: (i, 0)),
        out_shape=jax.ShapeDtypeStruct((m, d), out_dtype),
        compiler_params=_cparams("parallel"),
        name="rmsnorm",
    )(x2d, g.reshape(1, d))


def _mm_kernel(a_ref, b_ref, o_ref):
    acc = jnp.dot(a_ref[...], b_ref[...].astype(BF16), preferred_element_type=F32)
    o_ref[...] = acc.astype(o_ref.dtype)


def _mm_res_kernel(a_ref, b_ref, r_ref, o_ref):
    acc = jnp.dot(a_ref[...], b_ref[...].astype(BF16), preferred_element_type=F32)
    o_ref[...] = (r_ref[...] + acc).astype(o_ref.dtype)


def matmul(a, b, out_dtype, tm, tn, residual=None, name="matmul"):
    m, k = a.shape
    _, n = b.shape
    tm, tn = _tile(m, tm), _tile(n, tn)
    in_specs = [pl.BlockSpec((tm, k), lambda j, i: (i, 0)),
                pl.BlockSpec((k, tn), lambda j, i: (0, j))]
    args = [a, b]
    kern = _mm_kernel
    if residual is not None:
        in_specs.append(pl.BlockSpec((tm, tn), lambda j, i: (i, j)))
        args.append(residual)
        kern = _mm_res_kernel
    return pl.pallas_call(
        kern,
        grid=(n // tn, m // tm),
        in_specs=in_specs,
        out_specs=pl.BlockSpec((tm, tn), lambda j, i: (i, j)),
        out_shape=jax.ShapeDtypeStruct((m, n), out_dtype),
        compiler_params=_cparams("parallel", "parallel"),
        name=name,
    )(*args)


def _merge_kernel(ya_ref, yb_ref, yc_ref, yx_ref, wa_ref, wb_ref, wc_ref, wx_ref,
                  g0_ref, g1_ref, g2_ref, g3_ref, o_ref):
    def term(y_ref, w_ref, g_ref):
        return jax.nn.sigmoid(g_ref[...]) * jnp.dot(y_ref[...], w_ref[...], preferred_element_type=F32)

    acc = term(ya_ref, wa_ref, g0_ref)
    acc = acc + term(yb_ref, wb_ref, g1_ref)
    acc = acc + term(yc_ref, wc_ref, g2_ref)
    acc = acc + term(yx_ref, wx_ref, g3_ref)
    o_ref[...] = acc.astype(o_ref.dtype)


def merge(ys, ws, proj, gate_off, d):
    m = proj.shape[0]
    tm, tn = _tile(m, 512), _tile(d, 512)
    assert gate_off % tn == 0
    y_specs = [pl.BlockSpec((tm, y.shape[1]), lambda j, i: (i, 0)) for y in ys]
    w_specs = [pl.BlockSpec((w.shape[0], tn), lambda j, i: (0, j)) for w in ws]
    g_specs = [pl.BlockSpec((tm, tn), functools.partial(
        lambda j, i, base: (i, base + j), base=(gate_off + k * d) // tn)) for k in range(4)]
    return pl.pallas_call(
        _merge_kernel,
        grid=(d // tn, m // tm),
        in_specs=y_specs + w_specs + g_specs,
        out_specs=pl.BlockSpec((tm, tn), lambda j, i: (i, j)),
        out_shape=jax.ShapeDtypeStruct((m, d), BF16),
        compiler_params=_cparams("parallel", "parallel"),
        name="merge",
    )(*ys, *ws, proj, proj, proj, proj)


def _shift_rows(x, shift, hist):
    rolled = pltpu.roll(x, shift, axis=0)
    head = pltpu.roll(hist, shift, axis=0)
    row = lax.broadcasted_iota(jnp.int32, (SUBLANES, x.shape[1]), 0)
    fixed = jnp.where(row < shift, head, rolled[:SUBLANES])
    return jnp.concatenate([fixed, rolled[SUBLANES:]], axis=0)


def _causal_conv(x, hist, w, b):
    kw = w.shape[0]
    out = x * w[kw - 1:kw] + b
    for j in range(kw - 1):
        out = out + _shift_rows(x, kw - 1 - j, hist) * w[j:j + 1]
    return out


def _lru_kernel(ax_ref, ag_ref, cw_ref, cb_ref, wa_ref, ba_ref, wx_ref, bx_ref, lam_ref, o_ref,
                hist_sc, hcar_sc, a_sc, b_sc):
    t = pl.program_id(2)

    @pl.when(t == 0)
    def _():
        hist_sc[...] = jnp.zeros_like(hist_sc)
        hcar_sc[...] = jnp.zeros_like(hcar_sc)

    x = ax_ref[...]
    rows = x.shape[0]
    u = _causal_conv(x, hist_sc[...], cw_ref[...], cb_ref[...])
    hist_sc[...] = x[rows - SUBLANES:]
    ub = u.astype(BF16)
    r = jax.nn.sigmoid(jnp.dot(ub, wa_ref[...], preferred_element_type=F32) + ba_ref[...])
    i = jax.nn.sigmoid(jnp.dot(ub, wx_ref[...], preferred_element_type=F32) + bx_ref[...])
    log_a = -LRU_C * r * jax.nn.softplus(-lam_ref[...])
    a = jnp.exp(log_a)
    a_sc[...] = a
    b_sc[...] = jnp.sqrt(-jnp.tanh(log_a) * (a * a + 1.0)) * (i * u)

    row = lax.broadcasted_iota(jnp.int32, (SUBLANES, x.shape[1]), 0)

    def group(g, h_prev):
        sl = pl.ds(pl.multiple_of(g * SUBLANES, SUBLANES), SUBLANES)
        a8, b8 = a_sc[sl, :], b_sc[sl, :]
        for s in (1, 2, 4):
            a_sh = pltpu.roll(a8, s, axis=0)
            b_sh = pltpu.roll(b8, s, axis=0)
            keep = row >= s
            b8 = jnp.where(keep, a8 * b_sh + b8, b8)
            a8 = jnp.where(keep, a8 * a_sh, a8)
        h8 = a8 * h_prev + b8
        b_sc[sl, :] = h8
        return h8[SUBLANES - 1:SUBLANES]

    hcar_sc[...] = lax.fori_loop(0, rows // SUBLANES, group, hcar_sc[...])
    o_ref[...] = (b_sc[...] * _silu(ag_ref[...])).astype(o_ref.dtype)


def lru_branch(proj, off_x, off_g, bsz, s, cw, cb, wa, ba, wx, bx, lam):
    nb, bs, _ = wa.shape
    w = nb * bs
    tt = _tile(s, 512)
    nt = s // tt
    assert off_x % bs == 0 and off_g % bs == 0
    vec = lambda v: v.reshape(1, w)
    vspec = pl.BlockSpec((1, bs), lambda b, n, t: (0, n))
    return pl.pallas_call(
        _lru_kernel,
        grid=(bsz, nb, nt),
        in_specs=[
            pl.BlockSpec((tt, bs), lambda b, n, t: (b * nt + t, off_x // bs + n)),
            pl.BlockSpec((tt, bs), lambda b, n, t: (b * nt + t, off_g // bs + n)),
            pl.BlockSpec((cw.shape[0], bs), lambda b, n, t: (0, n)),
            vspec,
            pl.BlockSpec((None, bs, bs), lambda b, n, t: (n, 0, 0)),
            vspec,
            pl.BlockSpec((None, bs, bs), lambda b, n, t: (n, 0, 0)),
            vspec,
            vspec,
        ],
        out_specs=pl.BlockSpec((tt, bs), lambda b, n, t: (b * nt + t, n)),
        out_shape=jax.ShapeDtypeStruct((bsz * s, w), BF16),
        scratch_shapes=[pltpu.VMEM((SUBLANES, bs), F32), pltpu.VMEM((1, bs), F32),
                        pltpu.VMEM((tt, bs), F32), pltpu.VMEM((tt, bs), F32)],
        compiler_params=_cparams("parallel", "parallel", "arbitrary"),
        name="rg_lru",
    )(proj, proj, cw, vec(cb), wa.astype(BF16), vec(ba), wx.astype(BF16), vec(bx), vec(lam))


def _seg_sum(x, g):
    parts = [jnp.dot(x[:, c * LANES:(c + 1) * LANES], g, precision=HIGHEST, preferred_element_type=F32)
             for c in range(x.shape[1] // LANES)]
    return parts[0] if len(parts) == 1 else jnp.concatenate(parts, axis=1)


def _rwkv_prep_kernel(pr_ref, pk_ref, pv_ref, plo_ref, mur_ref, muk_ref, muv_ref, mulo_ref,
                      w0_ref, wup_ref, a0_ref, aup_ref, kk_ref, ka_ref, rk_ref, g_ref,
                      r_o, w_o, k_o, v_o, kk_o, kka_o, bonus_o,
                      lr_sc, lk_sc, lv_sc, llo_sc):
    t = pl.program_id(1)

    @pl.when(t == 0)
    def _():
        for sc in (lr_sc, lk_sc, lv_sc, llo_sc):
            sc[...] = jnp.zeros_like(sc)

    def shifted(p_ref, mu_ref, last_sc):
        p = p_ref[...]
        rows = p.shape[0]
        prev = _shift_rows(p, 1, last_sc[...])
        last_sc[...] = p[rows - SUBLANES:]
        return p + (prev - p) * mu_ref[...]

    r = shifted(pr_ref, mur_ref, lr_sc)
    k = shifted(pk_ref, muk_ref, lk_sc)
    v = shifted(pv_ref, muv_ref, lv_sc)
    lo = shifted(plo_ref, mulo_ref, llo_sc)

    w_pre = w0_ref[...] + jnp.dot(jnp.tanh(lo).astype(BF16), wup_ref[...], preferred_element_type=F32)
    w_log = -jax.nn.softplus(-w_pre) - 0.5
    decay = jnp.exp(-jnp.exp(w_log))
    a = jax.nn.sigmoid(a0_ref[...] + jnp.dot(lo.astype(BF16), aup_ref[...], preferred_element_type=F32))

    g = g_ref[...]
    kk = k * kk_ref[...]
    kk = kk / jnp.maximum(jnp.sqrt(_seg_sum(kk * kk, g)), 1e-12)
    k2 = k * (1.0 + (a - 1.0) * ka_ref[...])
    r_o[...] = r
    w_o[...] = decay
    k_o[...] = k2
    v_o[...] = v
    kk_o[...] = kk
    kka_o[...] = kk * a
    bonus_o[...] = _seg_sum(r * k2 * rk_ref[...], g) * v


def rwkv_prep(proj, offs, bsz, s, mu, w0, w_up, a0, a_up, k_k, k_a, r_k, n_head):
    off_r, off_k, off_v, off_lo, lo_w = offs
    r_w, w = w_up.shape
    r_a = a_up.shape[0]
    tt = _tile(s, 256)
    nt = s // tt
    assert off_r % w == 0 and off_k % w == 0 and off_v % w == 0 and off_lo % lo_w == 0
    assert LANES % n_head == 0 and w % LANES == 0
    vec = lambda x: x.reshape(1, -1)
    mu_lo = jnp.pad(mu[3 * w:], (0, lo_w - r_w - r_a))
    wup_pad = jnp.zeros((lo_w, w), F32).at[:r_w].set(w_up).astype(BF16)
    aup_pad = jnp.zeros((lo_w, w), F32).at[r_w:r_w + r_a].set(a_up).astype(BF16)
    gmat = jnp.asarray(np.kron(np.eye(LANES // n_head), np.ones((n_head, n_head))), F32)
    row_w = pl.BlockSpec((1, w), lambda b, t: (0, 0))
    blk = lambda off, width: pl.BlockSpec((tt, width), lambda b, t: (b * nt + t, off // width))
    out_spec = pl.BlockSpec((tt, w), lambda b, t: (b * nt + t, 0))
    out_shape = jax.ShapeDtypeStruct((bsz * s, w), F32)
    return pl.pallas_call(
        _rwkv_prep_kernel,
        grid=(bsz, nt),
        in_specs=[blk(off_r, w), blk(off_k, w), blk(off_v, w), blk(off_lo, lo_w),
                  row_w, row_w, row_w, pl.BlockSpec((1, lo_w), lambda b, t: (0, 0)),
                  row_w, pl.BlockSpec((lo_w, w), lambda b, t: (0, 0)),
                  row_w, pl.BlockSpec((lo_w, w), lambda b, t: (0, 0)),
                  row_w, row_w, row_w,
                  pl.BlockSpec((LANES, LANES), lambda b, t: (0, 0))],
        out_specs=[out_spec] * 7,
        out_shape=[out_shape] * 7,
        scratch_shapes=[pltpu.VMEM((SUBLANES, w), F32)] * 3 + [pltpu.VMEM((SUBLANES, lo_w), F32)],
        compiler_params=_cparams("parallel", "arbitrary"),
        name="rwkv_prep",
    )(proj, proj, proj, proj, vec(mu[:w]), vec(mu[w:2 * w]), vec(mu[2 * w:3 * w]), vec(mu_lo),
      vec(w0), wup_pad, vec(a0), aup_pad, vec(k_k), vec(k_a), vec(r_k), gmat)


def _rwkv_scan_kernel(kk_ref, w_ref, kka_ref, k_ref, r_ref, v_ref, y_ref, s_sc):
    c = pl.program_id(0)
    steps = kk_ref.shape[1]
    n_k, n_rows, _ = s_sc.shape
    blocks = [pl.ds(b * SUBLANES, SUBLANES) for b in range(n_rows // SUBLANES)]

    @pl.when(c == 0)
    def _():
        s_sc[...] = jnp.zeros_like(s_sc)

    def tree_sum(parts):
        while len(parts) > 1:
            parts = [parts[i] + parts[i + 1] for i in range(0, len(parts), 2)]
        return parts[0]

    sa0 = []
    for blk in blocks:
        acc = [None, None]
        for j in range(n_k):
            term = s_sc[j, blk, :] * kk_ref[j, 0:1, :]
            acc[j % 2] = term if acc[j % 2] is None else acc[j % 2] + term
        sa0.append(acc[0] + acc[1])

    def step(t, sa, with_next):
        v = [v_ref[t, blk, :] for blk in blocks]
        yacc = [[None, None] for _ in blocks]
        zacc = [[None, None] for _ in blocks]
        row = pl.ds(t, 1)
        for j in range(n_k):
            w_j, kka_j, k_j, r_j = w_ref[j, row, :], kka_ref[j, row, :], k_ref[j, row, :], r_ref[j, row, :]
            kk_j = kk_ref[j, pl.ds(t + 1, 1), :] if with_next else None
            for b, blk in enumerate(blocks):
                st = s_sc[j, blk, :] * w_j - sa[b] * kka_j + v[b] * k_j
                s_sc[j, blk, :] = st
                ya = st * r_j
                yacc[b][j % 2] = ya if yacc[b][j % 2] is None else yacc[b][j % 2] + ya
                if with_next:
                    za = st * kk_j
                    zacc[b][j % 2] = za if zacc[b][j % 2] is None else zacc[b][j % 2] + za
        for b, blk in enumerate(blocks):
            y_ref[t, blk, :] = tree_sum(yacc[b])
        return tuple(tree_sum(z) for z in zacc) if with_next else None

    sa = lax.fori_loop(0, steps - 1, lambda t, sa: step(t, sa, True), tuple(sa0))
    step(steps - 1, sa, False)


def rwkv_scan(kk_e, w_e, kka_e, k_e, r_e, v_e):
    n_k, s, lanes = kk_e.shape
    n_rows = v_e.shape[1]
    assert n_rows % SUBLANES == 0
    tc = _tile(s, 64)
    vec_spec = pl.BlockSpec((n_k, tc, lanes), lambda c: (0, c, 0))
    row_spec = pl.BlockSpec((tc, n_rows, lanes), lambda c: (c, 0, 0))
    return pl.pallas_call(
        _rwkv_scan_kernel,
        grid=(s // tc,),
        in_specs=[vec_spec] * 5 + [row_spec],
        out_specs=row_spec,
        out_shape=jax.ShapeDtypeStruct((s, n_rows, lanes), F32),
        scratch_shapes=[pltpu.VMEM((n_k, n_rows, lanes), F32)],
        compiler_params=_cparams("arbitrary"),
        name="rwkv_scan",
    )(kk_e, w_e, kka_e, k_e, r_e, v_e)


def _rwkv_out_kernel(y_ref, bonus_ref, gate_ref, gw_ref, gb_ref, g_ref, o_ref, *, n_head):
    y = y_ref[...]
    g = g_ref[...] * (1.0 / n_head)
    mu = _seg_sum(y, g)
    yc = y - mu
    var = _seg_sum(yc * yc, g)
    yn = yc * lax.rsqrt(var + RWKV_GN_EPS) * gw_ref[...] + gb_ref[...]
    o_ref[...] = ((yn + bonus_ref[...]) * _silu(gate_ref[...])).astype(o_ref.dtype)


def rwkv_out(y, bonus, proj, off_g, gn_w, gn_b, n_head):
    m, w = y.shape
    tt = _tile(m, 256)
    assert off_g % w == 0
    gmat = jnp.asarray(np.kron(np.eye(LANES // n_head), np.ones((n_head, n_head))), F32)
    blk = pl.BlockSpec((tt, w), lambda i: (i, 0))
    row = pl.BlockSpec((1, w), lambda i: (0, 0))
    return pl.pallas_call(
        functools.partial(_rwkv_out_kernel, n_head=n_head),
        grid=(m // tt,),
        in_specs=[blk, blk, pl.BlockSpec((tt, w), lambda i: (i, off_g // w)), row, row,
                  pl.BlockSpec((LANES, LANES), lambda i: (0, 0))],
        out_specs=blk,
        out_shape=jax.ShapeDtypeStruct((m, w), BF16),
        compiler_params=_cparams("parallel"),
        name="rwkv_out",
    )(y, bonus, proj, gn_w.reshape(1, w), gn_b.reshape(1, w), gmat)


def _to_chain_kernel(x_ref, o_ref, *, n_heads, n_head, dup, value_rows):
    bsz, t, _ = x_ref.shape
    parts = []
    for b in range(bsz):
        x3 = x_ref[b].T.reshape(n_heads, n_head, t)
        parts.append(pltpu.einshape("hjt->jht", x3))
    if value_rows:
        nr = n_head // dup
        e = jnp.concatenate([p[d * nr:(d + 1) * nr] for d in range(dup) for p in parts], axis=1)
        o_ref[...] = pltpu.einshape("itc->tic", jnp.swapaxes(e, 1, 2))
    else:
        e = jnp.concatenate(parts * dup, axis=1)
        o_ref[...] = jnp.swapaxes(e, 1, 2)


def to_chain_layout(x, bsz, s, n_heads, n_head, value_rows=False):
    w = n_heads * n_head
    assert LANES % (bsz * n_heads) == 0
    dup = LANES // (bsz * n_heads)
    assert n_head % dup == 0
    tt = _tile(s, LANES)
    if value_rows:
        out_shape, out_block, out_idx = (s, n_head // dup, LANES), (tt, n_head // dup, LANES), lambda i: (i, 0, 0)
    else:
        out_shape, out_block, out_idx = (n_head, s, LANES), (n_head, tt, LANES), lambda i: (0, i, 0)
    return pl.pallas_call(
        functools.partial(_to_chain_kernel, n_heads=n_heads, n_head=n_head, dup=dup, value_rows=value_rows),
        grid=(s // tt,),
        in_specs=[pl.BlockSpec((bsz, tt, w), lambda i: (0, i, 0))],
        out_specs=pl.BlockSpec(out_block, out_idx),
        out_shape=jax.ShapeDtypeStruct(out_shape, F32),
        compiler_params=_cparams("parallel"),
        name="to_chain_rows" if value_rows else "to_chain_keys",
    )(x.reshape(bsz, s, w))


def _from_chain_kernel(y_ref, o_ref, *, n_heads, n_head, dup):
    bsz = o_ref.shape[0]
    t = y_ref.shape[0]
    e = jnp.swapaxes(pltpu.einshape("tic->itc", y_ref[...]), 1, 2)
    bh = bsz * n_heads
    for b in range(bsz):
        rows = jnp.concatenate(
            [e[:, d * bh + b * n_heads:d * bh + (b + 1) * n_heads, :] for d in range(dup)], axis=0)
        o_ref[b] = pltpu.einshape("iht->hit", rows).reshape(n_heads * n_head, t).T


def from_chain_layout(y_e, bsz, n_heads, n_head):
    s, n_rows, _ = y_e.shape
    w = n_heads * n_head
    dup = LANES // (bsz * n_heads)
    tt = _tile(s, LANES)
    return pl.pallas_call(
        functools.partial(_from_chain_kernel, n_heads=n_heads, n_head=n_head, dup=dup),
        grid=(s // tt,),
        in_specs=[pl.BlockSpec((tt, n_rows, LANES), lambda i: (i, 0, 0))],
        out_specs=pl.BlockSpec((bsz, tt, w), lambda i: (0, i, 0)),
        out_shape=jax.ShapeDtypeStruct((bsz, s, w), F32),
        compiler_params=_cparams("parallel"),
        name="from_chain_rows",
    )(y_e).reshape(bsz * s, w)


def _mlstm_kernel(qk_ref, v_ref, o_ref, g_ref, if_ref, cw_ref, cb_ref, bias_ref, gw_ref, tril_ref,
                  out_ref, hist_sc, c_sc, m_sc, *, n_heads, dqk, dv):
    t = pl.program_id(1)

    @pl.when(t == 0)
    def _():
        hist_sc[...] = jnp.zeros_like(hist_sc)
        c_sc[...] = jnp.zeros_like(c_sc)
        m_sc[...] = jnp.full_like(m_sc, -jnp.inf)

    qk_in = qk_ref[...]
    rows = qk_in.shape[0]
    qk = _silu(_causal_conv(qk_in, hist_sc[...], cw_ref[...], cb_ref[...]))
    hist_sc[...] = qk_in[rows - SUBLANES:]

    pre = if_ref[...] + bias_ref[...]
    li = pre
    lf = jax.nn.log_sigmoid(pltpu.roll(pre, LANES - n_heads, axis=1))
    tril = tril_ref[...]
    bcum = jnp.dot(tril, lf, precision=HIGHEST, preferred_element_type=F32)
    g_tot = bcum[rows - 1:rows]
    m_prev = m_sc[...]
    le = g_tot - bcum + li
    m_new = jnp.maximum(g_tot + m_prev, jnp.max(le, axis=0, keepdims=True))
    keep = jnp.exp(g_tot + m_prev - m_new)
    wk = jnp.exp(le - m_new)
    m_sc[...] = m_new
    bcum_t = bcum.T
    li_t = li.T
    causal = tril > 0.5
    ones_blk = (lax.broadcasted_iota(jnp.int32, (rows, LANES), 1) == 0).astype(F32)

    for h in range(n_heads):
        q = qk[:, h * dqk:(h + 1) * dqk]
        k = qk[:, (n_heads + h) * dqk:(n_heads + h + 1) * dqk] * (dqk ** -0.5)
        v = v_ref[:, h * dv:(h + 1) * dv]
        v_aug = jnp.concatenate([v, ones_blk], axis=1).astype(BF16)
        b_col = bcum[:, h:h + 1]
        dmat = b_col - bcum_t[h:h + 1, :] + li_t[h:h + 1, :]
        dmat = jnp.where(causal, dmat, -jnp.inf)
        inter = b_col + m_prev[:, h:h + 1]
        m_t = jnp.maximum(inter, jnp.max(dmat, axis=-1, keepdims=True))
        w_intra = jnp.exp(dmat - m_t)
        w_inter = jnp.exp(inter - m_t)
        qb = q.astype(BF16)
        sc = lax.dot_general(qb, k.astype(BF16), (((1,), (1,)), ((), ())),
                             preferred_element_type=F32) * w_intra
        c_aug = c_sc[h]
        tot = (w_inter * jnp.dot(qb, c_aug.astype(BF16), preferred_element_type=F32)
               + jnp.dot(sc.astype(BF16), v_aug, preferred_element_type=F32))
        num = tot[:, :dv]
        den = tot[:, dv:dv + 1]
        hh = num / jnp.maximum(jnp.abs(den), jnp.exp(-m_t))
        kw = (k * wk[:, h:h + 1]).astype(BF16)
        c_sc[h] = keep[:, h:h + 1] * c_aug + lax.dot_general(
            kw, v_aug, (((0,), (0,)), ((), ())), preferred_element_type=F32)

        ho = hh * jax.nn.sigmoid(o_ref[:, h * dv:(h + 1) * dv])
        mu = jnp.mean(ho, axis=-1, keepdims=True)
        hc = ho - mu
        var = jnp.mean(hc * hc, axis=-1, keepdims=True)
        hn = hc * lax.rsqrt(var + MLSTM_GN_EPS) * gw_ref[:, h * dv:(h + 1) * dv]
        out_ref[:, h * dv:(h + 1) * dv] = (hn * _silu(g_ref[:, h * dv:(h + 1) * dv])).astype(out_ref.dtype)


def mlstm_branch(proj, offs, bsz, s, conv_w, conv_b, b_i, b_f, gn_w, w_c):
    off_qk, off_v, off_o, off_g, off_if, if_w = offs
    n_heads = b_i.shape[0]
    qk2 = conv_w.shape[1]
    dqk = qk2 // 2 // n_heads
    dv = w_c // n_heads
    ll = _tile(s, 256)
    nt = s // ll
    assert off_qk % qk2 == 0 and off_v % w_c == 0 and off_o % w_c == 0 and off_g % w_c == 0
    assert off_if % if_w == 0 and if_w == LANES and 2 * n_heads <= LANES
    bias = jnp.pad(jnp.concatenate([b_i, b_f]), (0, LANES - 2 * n_heads)).reshape(1, LANES)
    tril = jnp.asarray(np.tril(np.ones((ll, ll))), F32)
    blk = lambda off, width: pl.BlockSpec((ll, width), lambda b, t: (b * nt + t, off // width))
    full = lambda shape: pl.BlockSpec(shape, lambda b, t: (0,) * len(shape))
    return pl.pallas_call(
        functools.partial(_mlstm_kernel, n_heads=n_heads, dqk=dqk, dv=dv),
        grid=(bsz, nt),
        in_specs=[blk(off_qk, qk2), blk(off_v, w_c), blk(off_o, w_c), blk(off_g, w_c), blk(off_if, if_w),
                  full(conv_w.shape), full((1, qk2)), full((1, LANES)), full((1, w_c)), full((ll, ll))],
        out_specs=pl.BlockSpec((ll, w_c), lambda b, t: (b * nt + t, 0)),
        out_shape=jax.ShapeDtypeStruct((bsz * s, w_c), BF16),
        scratch_shapes=[pltpu.VMEM((SUBLANES, qk2), F32),
                        pltpu.VMEM((n_heads, dqk, dv + LANES), F32),
                        pltpu.VMEM((1, LANES), F32)],
        compiler_params=_cparams("parallel", "arbitrary"),
        name="mlstm",
    )(proj, proj, proj, proj, proj, conv_w, conv_b.reshape(1, qk2), bias, gn_w.reshape(1, w_c), tril)


def _xattn_kernel(q_ref, gate_ref, kv_ref, o_ref, *, n_heads):
    w_x = q_ref.shape[1]
    for h in range(n_heads):
        sl = slice(h * DH_X, (h + 1) * DH_X)
        q = q_ref[:, sl].astype(BF16)
        k = kv_ref[:, sl].astype(BF16)
        v = kv_ref[:, w_x + h * DH_X:w_x + (h + 1) * DH_X].astype(BF16)
        logits = lax.dot_general(q, k, (((1,), (1,)), ((), ())), preferred_element_type=F32) * (DH_X ** -0.5)
        e = jnp.exp(logits - jnp.max(logits, axis=-1, keepdims=True))
        probs = e / jnp.sum(e, axis=-1, keepdims=True)
        o = jnp.dot(probs.astype(BF16), v, preferred_element_type=F32)
        o_ref[:, sl] = (o * _silu(gate_ref[:, sl])).astype(o_ref.dtype)


def xattn_branch(proj, off_q, off_g, bsz, s, kv, w_x):
    m_len = kv.shape[0] // bsz
    tt = _tile(s, 512)
    nt = s // tt
    assert off_q % w_x == 0 and off_g % w_x == 0 and w_x % DH_X == 0
    return pl.pallas_call(
        functools.partial(_xattn_kernel, n_heads=w_x // DH_X),
        grid=(bsz, nt),
        in_specs=[pl.BlockSpec((tt, w_x), lambda b, t: (b * nt + t, off_q // w_x)),
                  pl.BlockSpec((tt, w_x), lambda b, t: (b * nt + t, off_g // w_x)),
                  pl.BlockSpec((m_len, 2 * w_x), lambda b, t: (b, 0))],
        out_specs=pl.BlockSpec((tt, w_x), lambda b, t: (b * nt + t, 0)),
        out_shape=jax.ShapeDtypeStruct((bsz * s, w_x), BF16),
        compiler_params=_cparams("parallel", "parallel"),
        name="mem_xattn",
    )(proj, proj, kv)


def _round_up(n, m):
    return (n + m - 1) // m * m


def _proj_layout(sizes, w_b, lo_w, if_w, n_tile):
    src = np.concatenate([[0], np.cumsum(sizes)])
    seg = {name: (int(src[i]), int(sizes[i])) for i, name in enumerate(
        ("a_x", "a_g", "b_s", "b_g", "c_qk", "c_v", "c_o", "c_g", "c_if", "x_q", "x_g", "gates"))}
    b0 = seg["b_s"][0]
    seg["b_r"], seg["b_k"], seg["b_v"] = (b0, w_b), (b0 + w_b, w_b), (b0 + 2 * w_b, w_b)
    seg["b_lo"] = (b0 + 3 * w_b, seg["b_s"][1] - 3 * w_b)
    order = ("a_x", "a_g", "b_g", "c_qk", "c_v", "c_o", "c_g", "b_r", "b_k", "b_v", "gates",
             "x_q", "x_g", "b_lo", "c_if")
    padded = {"b_lo": lo_w, "c_if": if_w}
    offs, pieces, pos = {}, [], 0
    for name in order:
        start, width = seg[name]
        offs[name] = pos
        pieces.append((start, width, padded.get(name, width)))
        pos += padded.get(name, width)
    total = _round_up(pos, n_tile)
    return offs, pieces, total


def _build_w_in(w_in, pieces, total):
    cols, pos = [], 0
    for start, width, padded in pieces:
        cols.append(w_in[:, start:start + width])
        if padded > width:
            cols.append(jnp.zeros((w_in.shape[0], padded - width), w_in.dtype))
        pos += padded
    if total > pos:
        cols.append(jnp.zeros((w_in.shape[0], total - pos), w_in.dtype))
    return jnp.concatenate(cols, axis=1).astype(BF16)


def _layer(x2d, memn_src, bsz, s, p):
    d = x2d.shape[1]
    w_a = p["lru_lambda"].shape[0]
    n_heads_b, n_head_b = p["rwkv_r_k"].shape
    w_b = n_heads_b * n_head_b
    w_c = p["w_branch_c"].shape[0]
    w_x = p["w_branch_x"].shape[0]
    qk2 = p["mlstm_conv_w"].shape[1]
    n_heads_c = p["mlstm_b_i"].shape[0]
    b_shift_w = p["rwkv_mu"].shape[0]
    sizes = (w_a, w_a, b_shift_w, w_b, qk2, w_c, w_c, w_c, 2 * n_heads_c, w_x, w_x, 4 * d)
    lo_w = _round_up(b_shift_w - 3 * w_b, 256)
    n_tile = 768
    offs, pieces, total = _proj_layout(sizes, w_b, lo_w, LANES, n_tile)

    h = rmsnorm(x2d, p["norm_g"], BF16)
    proj = matmul(h, _build_w_in(p["w_in"], pieces, total), F32, 1024, n_tile, name="in_proj")

    y_a = lru_branch(proj, offs["a_x"], offs["a_g"], bsz, s, p["lru_conv_w"], p["lru_conv_b"],
                     p["lru_wa"], p["lru_ba"], p["lru_wx"], p["lru_bx"], p["lru_lambda"])

    r, w, k, v, kk, kka, bonus = rwkv_prep(
        proj, (offs["b_r"], offs["b_k"], offs["b_v"], offs["b_lo"], lo_w), bsz, s, p["rwkv_mu"],
        p["rwkv_w0"], p["rwkv_w_up"], p["rwkv_a0"], p["rwkv_a_up"], p["rwkv_k_k"], p["rwkv_k_a"],
        p["rwkv_r_k"], n_head_b)
    to_e = functools.partial(to_chain_layout, bsz=bsz, s=s, n_heads=n_heads_b, n_head=n_head_b)
    y_e = rwkv_scan(to_e(kk), to_e(w), to_e(kka), to_e(k), to_e(r), to_e(v, value_rows=True))
    y_b = rwkv_out(from_chain_layout(y_e, bsz, n_heads_b, n_head_b), bonus, proj, offs["b_g"],
                   p["rwkv_gn_w"], p["rwkv_gn_b"], n_head_b)

    y_c = mlstm_branch(proj, (offs["c_qk"], offs["c_v"], offs["c_o"], offs["c_g"], offs["c_if"], LANES),
                       bsz, s, p["mlstm_conv_w"], p["mlstm_conv_b"], p["mlstm_b_i"], p["mlstm_b_f"],
                       p["mlstm_gn_w"], w_c)

    memn = rmsnorm(memn_src, p["mem_norm_g"], BF16)
    kv = matmul(memn, p["xattn_w_kv"], F32, 512, 512, name="mem_kv")
    y_x = xattn_branch(proj, offs["x_q"], offs["x_g"], bsz, s, kv, w_x)

    merged = merge((y_a, y_b, y_c, y_x),
                   tuple(p[n].astype(BF16) for n in ("w_branch_a", "w_branch_b", "w_branch_c", "w_branch_x")),
                   proj, offs["gates"], d)
    return matmul(merged, p["w_out"], F32, 1024, 512, residual=x2d, name="out_proj")


_LAYER_PARAMS = (
    "norm_g", "mem_norm_g", "w_in", "lru_conv_w", "lru_conv_b", "lru_wa", "lru_ba", "lru_wx", "lru_bx",
    "lru_lambda", "rwkv_mu", "rwkv_w0", "rwkv_w_up", "rwkv_a0", "rwkv_a_up", "rwkv_k_k", "rwkv_k_a",
    "rwkv_r_k", "rwkv_gn_w", "rwkv_gn_b", "mlstm_conv_w", "mlstm_conv_b", "mlstm_b_i", "mlstm_b_f",
    "mlstm_gn_w", "xattn_w_kv", "w_branch_a", "w_branch_b", "w_branch_c", "w_branch_x", "w_out")


def kernel(x, mem, norm_g, mem_norm_g, w_in, lru_conv_w, lru_conv_b, lru_wa, lru_ba, lru_wx, lru_bx, lru_lambda, rwkv_mu, rwkv_w0, rwkv_w_up, rwkv_a0, rwkv_a_up, rwkv_k_k, rwkv_k_a, rwkv_r_k, rwkv_gn_w, rwkv_gn_b, mlstm_conv_w, mlstm_conv_b, mlstm_b_i, mlstm_b_f, mlstm_gn_w, xattn_w_kv, w_branch_a, w_branch_b, w_branch_c, w_branch_x, w_out, final_norm_g):
    stacked = dict(zip(_LAYER_PARAMS, (
        norm_g, mem_norm_g, w_in, lru_conv_w, lru_conv_b, lru_wa, lru_ba, lru_wx, lru_bx, lru_lambda,
        rwkv_mu, rwkv_w0, rwkv_w_up, rwkv_a0, rwkv_a_up, rwkv_k_k, rwkv_k_a, rwkv_r_k, rwkv_gn_w,
        rwkv_gn_b, mlstm_conv_w, mlstm_conv_b, mlstm_b_i, mlstm_b_f, mlstm_gn_w, xattn_w_kv,
        w_branch_a, w_branch_b, w_branch_c, w_branch_x, w_out)))
    bsz, s, d = x.shape
    x2d = x.reshape(bsz * s, d)
    mem2d = mem.reshape(-1, d)
    for layer in range(norm_g.shape[0]):
        x2d = _layer(x2d, mem2d, bsz, s, {name: val[layer] for name, val in stacked.items()})
    return rmsnorm(x2d, final_norm_g, x.dtype).reshape(bsz, s, d)
```

```python
import functools

import jax
import jax.numpy as jnp
import numpy as np
from jax import lax
from jax.experimental import pallas as pl
from jax.experimental.pallas import tpu as pltpu

F32 = jnp.float32
BF16 = jnp.bfloat16
HIGHEST = lax.Precision.HIGHEST

LANES = 128
SUBLANES = 8
VMEM_LIMIT = 56 * 1024 * 1024

RMS_EPS = 1e-6
LRU_C = 8.0
RWKV_GN_EPS = 64e-5
MLSTM_GN_EPS = 1e-6
DH_X = 128


def _cparams(*sem):
    return pltpu.CompilerParams(dimension_semantics=sem, vmem_limit_bytes=VMEM_LIMIT)


def _tile(n, want):
    t = min(n, want)
    assert n % t == 0, (n, want)
    return t


def _silu(x):
    return x * jax.nn.sigmoid(x)


def _rmsnorm_kernel(x_ref, g_ref, o_ref):
    x = x_ref[...]
    ms = jnp.mean(x * x, axis=-1, keepdims=True)
    o_ref[...] = (x * lax.rsqrt(ms + RMS_EPS) * g_ref[...]).astype(o_ref.dtype)


def rmsnorm(x2d, g, out_dtype):
    m, d = x2d.shape
    tm = _tile(m, 256)
    return pl.pallas_call(
        _rmsnorm_kernel,
        grid=(m // tm,),
        in_specs=[pl.BlockSpec((tm, d), lambda i: (i, 0)),
                  pl.BlockSpec((1, d), lambda i: (0, 0))],
        out_specs=pl.BlockSpec((tm, d), lambda i: (i, 0)),
        out_shape=jax.ShapeDtypeStruct((m, d), out_dtype),
        compiler_params=_cparams("parallel"),
        name="rmsnorm",
    )(x2d, g.reshape(1, d))


def _mm_kernel(a_ref, b_ref, o_ref):
    acc = jnp.dot(a_ref[...], b_ref[...].astype(BF16), preferred_element_type=F32)
    o_ref[...] = acc.astype(o_ref.dtype)


def _mm_res_kernel(a_ref, b_ref, r_ref, o_ref):
    acc = jnp.dot(a_ref[...], b_ref[...].astype(BF16), preferred_element_type=F32)
    o_ref[...] = (r_ref[...] + acc).astype(o_ref.dtype)


def matmul(a, b, out_dtype, tm, tn, residual=None, name="matmul"):
    m, k = a.shape
    _, n = b.shape
    tm, tn = _tile(m, tm), _tile(n, tn)
    in_specs = [pl.BlockSpec((tm, k), lambda j, i: (i, 0)),
                pl.BlockSpec((k, tn), lambda j, i: (0, j))]
    args = [a, b]
    kern = _mm_kernel
    if residual is not None:
        in_specs.append(pl.BlockSpec((tm, tn), lambda j, i: (i, j)))
        args.append(residual)
        kern = _mm_res_kernel
    return pl.pallas_call(
        kern,
        grid=(n // tn, m // tm),
        in_specs=in_specs,
        out_specs=pl.BlockSpec((tm, tn), lambda j, i: (i, j)),
        out_shape=jax.ShapeDtypeStruct((m, n), out_dtype),
        compiler_params=_cparams("parallel", "parallel"),
        name=name,
    )(*args)


def _merge_kernel(ya_ref, yb_ref, yc_ref, yx_ref, wa_ref, wb_ref, wc_ref, wx_ref,
                  g0_ref, g1_ref, g2_ref, g3_ref, o_ref):
    def term(y_ref, w_ref, g_ref):
        return jax.nn.sigmoid(g_ref[...]) * jnp.dot(y_ref[...], w_ref[...], preferred_element_type=F32)

    acc = term(ya_ref, wa_ref, g0_ref)
    acc = acc + term(yb_ref, wb_ref, g1_ref)
    acc = acc + term(yc_ref, wc_ref, g2_ref)
    acc = acc + term(yx_ref, wx_ref, g3_ref)
    o_ref[...] = acc.astype(o_ref.dtype)


def merge(ys, ws, proj, gate_off, d):
    m = proj.shape[0]
    tm, tn = _tile(m, 512), _tile(d, 512)
    assert gate_off % tn == 0
    y_specs = [pl.BlockSpec((tm, y.shape[1]), lambda j, i: (i, 0)) for y in ys]
    w_specs = [pl.BlockSpec((w.shape[0], tn), lambda j, i: (0, j)) for w in ws]
    g_specs = [pl.BlockSpec((tm, tn), functools.partial(
        lambda j, i, base: (i, base + j), base=(gate_off + k * d) // tn)) for k in range(4)]
    return pl.pallas_call(
        _merge_kernel,
        grid=(d // tn, m // tm),
        in_specs=y_specs + w_specs + g_specs,
        out_specs=pl.BlockSpec((tm, tn), lambda j, i: (i, j)),
        out_shape=jax.ShapeDtypeStruct((m, d), BF16),
        compiler_params=_cparams("parallel", "parallel"),
        name="merge",
    )(*ys, *ws, proj, proj, proj, proj)


def _shift_rows(x, shift, hist):
    rolled = pltpu.roll(x, shift, axis=0)
    head = pltpu.roll(hist, shift, axis=0)
    row = lax.broadcasted_iota(jnp.int32, (SUBLANES, x.shape[1]), 0)
    fixed = jnp.where(row < shift, head, rolled[:SUBLANES])
    return jnp.concatenate([fixed, rolled[SUBLANES:]], axis=0)


def _causal_conv(x, hist, w, b):
    kw = w.shape[0]
    out = x * w[kw - 1:kw] + b
    for j in range(kw - 1):
        out = out + _shift_rows(x, kw - 1 - j, hist) * w[j:j + 1]
    return out


def _lru_kernel(ax_ref, ag_ref, cw_ref, cb_ref, wa_ref, ba_ref, wx_ref, bx_ref, lam_ref, o_ref,
                hist_sc, hcar_sc, a_sc, b_sc, h_sc):
    t = pl.program_id(2)

    @pl.when(t == 0)
    def _():
        hist_sc[...] = jnp.zeros_like(hist_sc)
        hcar_sc[...] = jnp.zeros_like(hcar_sc)

    x = ax_ref[...]
    rows = x.shape[0]
    u = _causal_conv(x, hist_sc[...], cw_ref[...], cb_ref[...])
    hist_sc[...] = x[rows - SUBLANES:]
    ub = u.astype(BF16)
    r = jax.nn.sigmoid(jnp.dot(ub, wa_ref[...], preferred_element_type=F32) + ba_ref[...])
    i = jax.nn.sigmoid(jnp.dot(ub, wx_ref[...], preferred_element_type=F32) + bx_ref[...])
    log_a = -LRU_C * r * jax.nn.softplus(-lam_ref[...])
    a = jnp.exp(log_a)
    b = jnp.sqrt(-jnp.tanh(log_a) * (a * a + 1.0)) * (i * u)

    in_group = lax.broadcasted_iota(jnp.int32, x.shape, 0) % SUBLANES
    for s in (1, 2, 4):
        keep = in_group >= s
        b = jnp.where(keep, a * pltpu.roll(b, s, axis=0) + b, b)
        a = jnp.where(keep, a * pltpu.roll(a, s, axis=0), a)
    a_sc[...] = a
    b_sc[...] = b

    def group(g, h_prev):
        sl = pl.ds(pl.multiple_of(g * SUBLANES, SUBLANES), SUBLANES)
        h8 = a_sc[sl, :] * h_prev + b_sc[sl, :]
        h_sc[sl, :] = h8
        return h8[SUBLANES - 1:SUBLANES]

    hcar_sc[...] = lax.fori_loop(0, rows // SUBLANES, group, hcar_sc[...], unroll=4)
    o_ref[...] = (h_sc[...] * _silu(ag_ref[...])).astype(o_ref.dtype)


def lru_branch(proj, off_x, off_g, bsz, s, cw, cb, wa, ba, wx, bx, lam):
    nb, bs, _ = wa.shape
    w = nb * bs
    tt = _tile(s, 512)
    nt = s // tt
    assert off_x % bs == 0 and off_g % bs == 0
    vec = lambda v: v.reshape(1, w)
    vspec = pl.BlockSpec((1, bs), lambda b, n, t: (0, n))
    return pl.pallas_call(
        _lru_kernel,
        grid=(bsz, nb, nt),
        in_specs=[
            pl.BlockSpec((tt, bs), lambda b, n, t: (b * nt + t, off_x // bs + n)),
            pl.BlockSpec((tt, bs), lambda b, n, t: (b * nt + t, off_g // bs + n)),
            pl.BlockSpec((cw.shape[0], bs), lambda b, n, t: (0, n)),
            vspec,
            pl.BlockSpec((None, bs, bs), lambda b, n, t: (n, 0, 0)),
            vspec,
            pl.BlockSpec((None, bs, bs), lambda b, n, t: (n, 0, 0)),
            vspec,
            vspec,
        ],
        out_specs=pl.BlockSpec((tt, bs), lambda b, n, t: (b * nt + t, n)),
        out_shape=jax.ShapeDtypeStruct((bsz * s, w), BF16),
        scratch_shapes=[pltpu.VMEM((SUBLANES, bs), F32), pltpu.VMEM((1, bs), F32),
                        pltpu.VMEM((tt, bs), F32), pltpu.VMEM((tt, bs), F32), pltpu.VMEM((tt, bs), F32)],
        compiler_params=_cparams("parallel", "parallel", "arbitrary"),
        name="rg_lru",
    )(proj, proj, cw, vec(cb), wa.astype(BF16), vec(ba), wx.astype(BF16), vec(bx), vec(lam))


def _seg_sum(x, g):
    parts = [jnp.dot(x[:, c * LANES:(c + 1) * LANES], g, precision=HIGHEST, preferred_element_type=F32)
             for c in range(x.shape[1] // LANES)]
    return parts[0] if len(parts) == 1 else jnp.concatenate(parts, axis=1)


def _rwkv_prep_kernel(pr_ref, pk_ref, pv_ref, plo_ref, mur_ref, muk_ref, muv_ref, mulo_ref,
                      w0_ref, wup_ref, a0_ref, aup_ref, kk_ref, ka_ref, rk_ref, g_ref,
                      r_o, w_o, k_o, v_o, kk_o, kka_o, bonus_o,
                      lr_sc, lk_sc, lv_sc, llo_sc):
    t = pl.program_id(1)

    @pl.when(t == 0)
    def _():
        for sc in (lr_sc, lk_sc, lv_sc, llo_sc):
            sc[...] = jnp.zeros_like(sc)

    def shifted(p_ref, mu_ref, last_sc):
        p = p_ref[...]
        rows = p.shape[0]
        prev = _shift_rows(p, 1, last_sc[...])
        last_sc[...] = p[rows - SUBLANES:]
        return p + (prev - p) * mu_ref[...]

    r = shifted(pr_ref, mur_ref, lr_sc)
    k = shifted(pk_ref, muk_ref, lk_sc)
    v = shifted(pv_ref, muv_ref, lv_sc)
    lo = shifted(plo_ref, mulo_ref, llo_sc)

    w_pre = w0_ref[...] + jnp.dot(jnp.tanh(lo).astype(BF16), wup_ref[...], preferred_element_type=F32)
    w_log = -jax.nn.softplus(-w_pre) - 0.5
    decay = jnp.exp(-jnp.exp(w_log))
    a = jax.nn.sigmoid(a0_ref[...] + jnp.dot(lo.astype(BF16), aup_ref[...], preferred_element_type=F32))

    g = g_ref[...]
    kk = k * kk_ref[...]
    kk = kk / jnp.maximum(jnp.sqrt(_seg_sum(kk * kk, g)), 1e-12)
    k2 = k * (1.0 + (a - 1.0) * ka_ref[...])
    r_o[...] = r
    w_o[...] = decay
    k_o[...] = k2
    v_o[...] = v
    kk_o[...] = kk
    kka_o[...] = kk * a
    bonus_o[...] = _seg_sum(r * k2 * rk_ref[...], g) * v


def rwkv_prep(proj, offs, bsz, s, mu, w0, w_up, a0, a_up, k_k, k_a, r_k, n_head):
    off_r, off_k, off_v, off_lo, lo_w = offs
    r_w, w = w_up.shape
    r_a = a_up.shape[0]
    tt = _tile(s, 256)
    nt = s // tt
    assert off_r % w == 0 and off_k % w == 0 and off_v % w == 0 and off_lo % lo_w == 0
    assert LANES % n_head == 0 and w % LANES == 0
    vec = lambda x: x.reshape(1, -1)
    mu_lo = jnp.pad(mu[3 * w:], (0, lo_w - r_w - r_a))
    wup_pad = jnp.zeros((lo_w, w), F32).at[:r_w].set(w_up).astype(BF16)
    aup_pad = jnp.zeros((lo_w, w), F32).at[r_w:r_w + r_a].set(a_up).astype(BF16)
    gmat = jnp.asarray(np.kron(np.eye(LANES // n_head), np.ones((n_head, n_head))), F32)
    row_w = pl.BlockSpec((1, w), lambda b, t: (0, 0))
    blk = lambda off, width: pl.BlockSpec((tt, width), lambda b, t: (b * nt + t, off // width))
    out_spec = pl.BlockSpec((tt, w), lambda b, t: (b * nt + t, 0))
    out_shape = jax.ShapeDtypeStruct((bsz * s, w), F32)
    return pl.pallas_call(
        _rwkv_prep_kernel,
        grid=(bsz, nt),
        in_specs=[blk(off_r, w), blk(off_k, w), blk(off_v, w), blk(off_lo, lo_w),
                  row_w, row_w, row_w, pl.BlockSpec((1, lo_w), lambda b, t: (0, 0)),
                  row_w, pl.BlockSpec((lo_w, w), lambda b, t: (0, 0)),
                  row_w, pl.BlockSpec((lo_w, w), lambda b, t: (0, 0)),
                  row_w, row_w, row_w,
                  pl.BlockSpec((LANES, LANES), lambda b, t: (0, 0))],
        out_specs=[out_spec] * 7,
        out_shape=[out_shape] * 7,
        scratch_shapes=[pltpu.VMEM((SUBLANES, w), F32)] * 3 + [pltpu.VMEM((SUBLANES, lo_w), F32)],
        compiler_params=_cparams("parallel", "arbitrary"),
        name="rwkv_prep",
    )(proj, proj, proj, proj, vec(mu[:w]), vec(mu[w:2 * w]), vec(mu[2 * w:3 * w]), vec(mu_lo),
      vec(w0), wup_pad, vec(a0), aup_pad, vec(k_k), vec(k_a), vec(r_k), gmat)


def _rwkv_scan_kernel(kk_ref, w_ref, kka_ref, k_ref, r_ref, v_ref, y_ref, s_sc):
    c = pl.program_id(0)
    steps = kk_ref.shape[1]
    n_k, n_rows, _ = s_sc.shape
    blocks = [pl.ds(b * SUBLANES, SUBLANES) for b in range(n_rows // SUBLANES)]

    @pl.when(c == 0)
    def _():
        s_sc[...] = jnp.zeros_like(s_sc)

    def tree_sum(parts):
        while len(parts) > 1:
            parts = [parts[i] + parts[i + 1] for i in range(0, len(parts), 2)]
        return parts[0]

    sa0 = []
    for blk in blocks:
        acc = [None, None]
        for j in range(n_k):
            term = s_sc[j, blk, :] * kk_ref[j, 0:1, :]
            acc[j % 2] = term if acc[j % 2] is None else acc[j % 2] + term
        sa0.append(acc[0] + acc[1])

    def step(t, sa, with_next):
        v = [v_ref[t, blk, :] for blk in blocks]
        yacc = [[None, None] for _ in blocks]
        zacc = [[None, None] for _ in blocks]
        row = pl.ds(t, 1)
        for j in range(n_k):
            w_j, kka_j, k_j, r_j = w_ref[j, row, :], kka_ref[j, row, :], k_ref[j, row, :], r_ref[j, row, :]
            kk_j = kk_ref[j, pl.ds(t + 1, 1), :] if with_next else None
            for b, blk in enumerate(blocks):
                st = s_sc[j, blk, :] * w_j - sa[b] * kka_j + v[b] * k_j
                s_sc[j, blk, :] = st
                ya = st * r_j
                yacc[b][j % 2] = ya if yacc[b][j % 2] is None else yacc[b][j % 2] + ya
                if with_next:
                    za = st * kk_j
                    zacc[b][j % 2] = za if zacc[b][j % 2] is None else zacc[b][j % 2] + za
        for b, blk in enumerate(blocks):
            y_ref[t, blk, :] = tree_sum(yacc[b])
        return tuple(tree_sum(z) for z in zacc) if with_next else None

    sa = lax.fori_loop(0, steps - 1, lambda t, sa: step(t, sa, True), tuple(sa0))
    step(steps - 1, sa, False)


def rwkv_scan(kk_e, w_e, kka_e, k_e, r_e, v_e):
    n_k, s, lanes = kk_e.shape
    n_rows = v_e.shape[1]
    assert n_rows % SUBLANES == 0
    tc = _tile(s, 64)
    vec_spec = pl.BlockSpec((n_k, tc, lanes), lambda c: (0, c, 0))
    row_spec = pl.BlockSpec((tc, n_rows, lanes), lambda c: (c, 0, 0))
    return pl.pallas_call(
        _rwkv_scan_kernel,
        grid=(s // tc,),
        in_specs=[vec_spec] * 5 + [row_spec],
        out_specs=row_spec,
        out_shape=jax.ShapeDtypeStruct((s, n_rows, lanes), F32),
        scratch_shapes=[pltpu.VMEM((n_k, n_rows, lanes), F32)],
        compiler_params=_cparams("arbitrary"),
        name="rwkv_scan",
    )(kk_e, w_e, kka_e, k_e, r_e, v_e)


def _rwkv_out_kernel(y_ref, bonus_ref, gate_ref, gw_ref, gb_ref, g_ref, o_ref, *, n_head):
    y = y_ref[...]
    g = g_ref[...] * (1.0 / n_head)
    mu = _seg_sum(y, g)
    yc = y - mu
    var = _seg_sum(yc * yc, g)
    yn = yc * lax.rsqrt(var + RWKV_GN_EPS) * gw_ref[...] + gb_ref[...]
    o_ref[...] = ((yn + bonus_ref[...]) * _silu(gate_ref[...])).astype(o_ref.dtype)


def rwkv_out(y, bonus, proj, off_g, gn_w, gn_b, n_head):
    m, w = y.shape
    tt = _tile(m, 256)
    assert off_g % w == 0
    gmat = jnp.asarray(np.kron(np.eye(LANES // n_head), np.ones((n_head, n_head))), F32)
    blk = pl.BlockSpec((tt, w), lambda i: (i, 0))
    row = pl.BlockSpec((1, w), lambda i: (0, 0))
    return pl.pallas_call(
        functools.partial(_rwkv_out_kernel, n_head=n_head),
        grid=(m // tt,),
        in_specs=[blk, blk, pl.BlockSpec((tt, w), lambda i: (i, off_g // w)), row, row,
                  pl.BlockSpec((LANES, LANES), lambda i: (0, 0))],
        out_specs=blk,
        out_shape=jax.ShapeDtypeStruct((m, w), BF16),
        compiler_params=_cparams("parallel"),
        name="rwkv_out",
    )(y, bonus, proj, gn_w.reshape(1, w), gn_b.reshape(1, w), gmat)


def _to_chain_kernel(x_ref, o_ref, *, n_heads, n_head, dup, value_rows):
    bsz, t, _ = x_ref.shape
    parts = []
    for b in range(bsz):
        x3 = x_ref[b].T.reshape(n_heads, n_head, t)
        parts.append(jnp.transpose(x3, (1, 0, 2)))
    if value_rows:
        nr = n_head // dup
        e = jnp.concatenate([p[d * nr:(d + 1) * nr] for d in range(dup) for p in parts], axis=1)
        o_ref[...] = jnp.transpose(jnp.swapaxes(e, 1, 2), (1, 0, 2))
    else:
        e = jnp.concatenate(parts * dup, axis=1)
        o_ref[...] = jnp.swapaxes(e, 1, 2)


def to_chain_layout(x, bsz, s, n_heads, n_head, value_rows=False):
    w = n_heads * n_head
    assert LANES % (bsz * n_heads) == 0
    dup = LANES // (bsz * n_heads)
    assert n_head % dup == 0
    tt = _tile(s, LANES)
    if value_rows:
        out_shape, out_block, out_idx = (s, n_head // dup, LANES), (tt, n_head // dup, LANES), lambda i: (i, 0, 0)
    else:
        out_shape, out_block, out_idx = (n_head, s, LANES), (n_head, tt, LANES), lambda i: (0, i, 0)
    return pl.pallas_call(
        functools.partial(_to_chain_kernel, n_heads=n_heads, n_head=n_head, dup=dup, value_rows=value_rows),
        grid=(s // tt,),
        in_specs=[pl.BlockSpec((bsz, tt, w), lambda i: (0, i, 0))],
        out_specs=pl.BlockSpec(out_block, out_idx),
        out_shape=jax.ShapeDtypeStruct(out_shape, F32),
        compiler_params=_cparams("parallel"),
        name="to_chain_rows" if value_rows else "to_chain_keys",
    )(x.reshape(bsz, s, w))


def _from_chain_kernel(y_ref, o_ref, *, n_heads, n_head, dup):
    bsz = o_ref.shape[0]
    t = y_ref.shape[0]
    e = jnp.swapaxes(jnp.transpose(y_ref[...], (1, 0, 2)), 1, 2)
    bh = bsz * n_heads
    for b in range(bsz):
        rows = jnp.concatenate(
            [e[:, d * bh + b * n_heads:d * bh + (b + 1) * n_heads, :] for d in range(dup)], axis=0)
        o_ref[b] = jnp.transpose(rows, (1, 0, 2)).reshape(n_heads * n_head, t).T


def from_chain_layout(y_e, bsz, n_heads, n_head):
    s, n_rows, _ = y_e.shape
    w = n_heads * n_head
    dup = LANES // (bsz * n_heads)
    tt = _tile(s, LANES)
    return pl.pallas_call(
        functools.partial(_from_chain_kernel, n_heads=n_heads, n_head=n_head, dup=dup),
        grid=(s // tt,),
        in_specs=[pl.BlockSpec((tt, n_rows, LANES), lambda i: (i, 0, 0))],
        out_specs=pl.BlockSpec((bsz, tt, w), lambda i: (0, i, 0)),
        out_shape=jax.ShapeDtypeStruct((bsz, s, w), F32),
        compiler_params=_cparams("parallel"),
        name="from_chain_rows",
    )(y_e).reshape(bsz * s, w)


def _mlstm_kernel(qk_ref, v_ref, o_ref, g_ref, if_ref, cw_ref, cb_ref, bias_ref, gw_ref, tril_ref,
                  out_ref, hist_sc, c_sc, m_sc, *, n_heads, dqk, dv):
    t = pl.program_id(1)

    @pl.when(t == 0)
    def _():
        hist_sc[...] = jnp.zeros_like(hist_sc)
        c_sc[...] = jnp.zeros_like(c_sc)
        m_sc[...] = jnp.full_like(m_sc, -jnp.inf)

    qk_in = qk_ref[...]
    rows = qk_in.shape[0]
    qk = _silu(_causal_conv(qk_in, hist_sc[...], cw_ref[...], cb_ref[...]))
    hist_sc[...] = qk_in[rows - SUBLANES:]

    pre = if_ref[...] + bias_ref[...]
    li = pre
    lf = jax.nn.log_sigmoid(pltpu.roll(pre, LANES - n_heads, axis=1))
    tril = tril_ref[...]
    bcum = jnp.dot(tril, lf, precision=HIGHEST, preferred_element_type=F32)
    g_tot = bcum[rows - 1:rows]
    m_prev = m_sc[...]
    le = g_tot - bcum + li
    m_new = jnp.maximum(g_tot + m_prev, jnp.max(le, axis=0, keepdims=True))
    keep = jnp.exp(g_tot + m_prev - m_new)
    wk = jnp.exp(le - m_new)
    m_sc[...] = m_new
    bcum_t = bcum.T
    li_t = li.T
    causal = tril > 0.5
    ones_blk = (lax.broadcasted_iota(jnp.int32, (rows, LANES), 1) == 0).astype(F32)

    for h in range(n_heads):
        q = qk[:, h * dqk:(h + 1) * dqk]
        k = qk[:, (n_heads + h) * dqk:(n_heads + h + 1) * dqk] * (dqk ** -0.5)
        v = v_ref[:, h * dv:(h + 1) * dv]
        v_aug = jnp.concatenate([v, ones_blk], axis=1).astype(BF16)
        b_col = bcum[:, h:h + 1]
        dmat = b_col - bcum_t[h:h + 1, :] + li_t[h:h + 1, :]
        dmat = jnp.where(causal, dmat, -jnp.inf)
        inter = b_col + m_prev[:, h:h + 1]
        m_t = jnp.maximum(inter, jnp.max(dmat, axis=-1, keepdims=True))
        w_intra = jnp.exp(dmat - m_t)
        w_inter = jnp.exp(inter - m_t)
        qb = q.astype(BF16)
        sc = lax.dot_general(qb, k.astype(BF16), (((1,), (1,)), ((), ())),
                             preferred_element_type=F32) * w_intra
        c_aug = c_sc[h]
        tot = (w_inter * jnp.dot(qb, c_aug.astype(BF16), preferred_element_type=F32)
               + jnp.dot(sc.astype(BF16), v_aug, preferred_element_type=F32))
        num = tot[:, :dv]
        den = tot[:, dv:dv + 1]
        hh = num / jnp.maximum(jnp.abs(den), jnp.exp(-m_t))
        kw = (k * wk[:, h:h + 1]).astype(BF16)
        c_sc[h] = keep[:, h:h + 1] * c_aug + lax.dot_general(
            kw, v_aug, (((0,), (0,)), ((), ())), preferred_element_type=F32)

        ho = hh * jax.nn.sigmoid(o_ref[:, h * dv:(h + 1) * dv])
        mu = jnp.mean(ho, axis=-1, keepdims=True)
        hc = ho - mu
        var = jnp.mean(hc * hc, axis=-1, keepdims=True)
        hn = hc * lax.rsqrt(var + MLSTM_GN_EPS) * gw_ref[:, h * dv:(h + 1) * dv]
        out_ref[:, h * dv:(h + 1) * dv] = (hn * _silu(g_ref[:, h * dv:(h + 1) * dv])).astype(out_ref.dtype)


def mlstm_branch(proj, offs, bsz, s, conv_w, conv_b, b_i, b_f, gn_w, w_c):
    off_qk, off_v, off_o, off_g, off_if, if_w = offs
    n_heads = b_i.shape[0]
    qk2 = conv_w.shape[1]
    dqk = qk2 // 2 // n_heads
    dv = w_c // n_heads
    ll = _tile(s, 256)
    nt = s // ll
    assert off_qk % qk2 == 0 and off_v % w_c == 0 and off_o % w_c == 0 and off_g % w_c == 0
    assert off_if % if_w == 0 and if_w == LANES and 2 * n_heads <= LANES
    bias = jnp.pad(jnp.concatenate([b_i, b_f]), (0, LANES - 2 * n_heads)).reshape(1, LANES)
    tril = jnp.asarray(np.tril(np.ones((ll, ll))), F32)
    blk = lambda off, width: pl.BlockSpec((ll, width), lambda b, t: (b * nt + t, off // width))
    full = lambda shape: pl.BlockSpec(shape, lambda b, t: (0,) * len(shape))
    return pl.pallas_call(
        functools.partial(_mlstm_kernel, n_heads=n_heads, dqk=dqk, dv=dv),
        grid=(bsz, nt),
        in_specs=[blk(off_qk, qk2), blk(off_v, w_c), blk(off_o, w_c), blk(off_g, w_c), blk(off_if, if_w),
                  full(conv_w.shape), full((1, qk2)), full((1, LANES)), full((1, w_c)), full((ll, ll))],
        out_specs=pl.BlockSpec((ll, w_c), lambda b, t: (b * nt + t, 0)),
        out_shape=jax.ShapeDtypeStruct((bsz * s, w_c), BF16),
        scratch_shapes=[pltpu.VMEM((SUBLANES, qk2), F32),
                        pltpu.VMEM((n_heads, dqk, dv + LANES), F32),
                        pltpu.VMEM((1, LANES), F32)],
        compiler_params=_cparams("parallel", "arbitrary"),
        name="mlstm",
    )(proj, proj, proj, proj, proj, conv_w, conv_b.reshape(1, qk2), bias, gn_w.reshape(1, w_c), tril)


def _xattn_kernel(q_ref, gate_ref, kv_ref, o_ref, *, n_heads):
    w_x = q_ref.shape[1]
    for h in range(n_heads):
        sl = slice(h * DH_X, (h + 1) * DH_X)
        q = q_ref[:, sl].astype(BF16)
        k = kv_ref[:, sl].astype(BF16)
        v = kv_ref[:, w_x + h * DH_X:w_x + (h + 1) * DH_X].astype(BF16)
        logits = lax.dot_general(q, k, (((1,), (1,)), ((), ())), preferred_element_type=F32) * (DH_X ** -0.5)
        e = jnp.exp(logits - jnp.max(logits, axis=-1, keepdims=True))
        probs = e / jnp.sum(e, axis=-1, keepdims=True)
        o = jnp.dot(probs.astype(BF16), v, preferred_element_type=F32)
        o_ref[:, sl] = (o * _silu(gate_ref[:, sl])).astype(o_ref.dtype)


def xattn_branch(proj, off_q, off_g, bsz, s, kv, w_x):
    m_len = kv.shape[0] // bsz
    tt = _tile(s, 512)
    nt = s // tt
    assert off_q % w_x == 0 and off_g % w_x == 0 and w_x % DH_X == 0
    return pl.pallas_call(
        functools.partial(_xattn_kernel, n_heads=w_x // DH_X),
        grid=(bsz, nt),
        in_specs=[pl.BlockSpec((tt, w_x), lambda b, t: (b * nt + t, off_q // w_x)),
                  pl.BlockSpec((tt, w_x), lambda b, t: (b * nt + t, off_g // w_x)),
                  pl.BlockSpec((m_len, 2 * w_x), lambda b, t: (b, 0))],
        out_specs=pl.BlockSpec((tt, w_x), lambda b, t: (b * nt + t, 0)),
        out_shape=jax.ShapeDtypeStruct((bsz * s, w_x), BF16),
        compiler_params=_cparams("parallel", "parallel"),
        name="mem_xattn",
    )(proj, proj, kv)


def _round_up(n, m):
    return (n + m - 1) // m * m


def _proj_layout(sizes, w_b, lo_w, if_w, n_tile):
    src = np.concatenate([[0], np.cumsum(sizes)])
    seg = {name: (int(src[i]), int(sizes[i])) for i, name in enumerate(
        ("a_x", "a_g", "b_s", "b_g", "c_qk", "c_v", "c_o", "c_g", "c_if", "x_q", "x_g", "gates"))}
    b0 = seg["b_s"][0]
    seg["b_r"], seg["b_k"], seg["b_v"] = (b0, w_b), (b0 + w_b, w_b), (b0 + 2 * w_b, w_b)
    seg["b_lo"] = (b0 + 3 * w_b, seg["b_s"][1] - 3 * w_b)
    order = ("a_x", "a_g", "b_g", "c_qk", "c_v", "c_o", "c_g", "b_r", "b_k", "b_v", "gates",
             "x_q", "x_g", "b_lo", "c_if")
    padded = {"b_lo": lo_w, "c_if": if_w}
    offs, pieces, pos = {}, [], 0
    for name in order:
        start, width = seg[name]
        offs[name] = pos
        pieces.append((start, width, padded.get(name, width)))
        pos += padded.get(name, width)
    total = _round_up(pos, n_tile)
    return offs, pieces, total


def _format_w_in_kernel(w_ref, o_ref, *, pieces):
    rows = w_ref.shape[0]
    pos = 0
    for start, width, padded in pieces:
        blk = w_ref[:, start:start + width]
        if padded > width:
            blk = jnp.concatenate([blk, jnp.zeros((rows, padded - width), F32)], axis=1)
        o_ref[:, pos:pos + padded] = blk.astype(o_ref.dtype)
        pos += padded
    if o_ref.shape[1] > pos:
        o_ref[:, pos:] = jnp.zeros((rows, o_ref.shape[1] - pos), o_ref.dtype)


def format_w_in(w_in_stacked, layer, pieces, total):
    _, k, c_in = w_in_stacked.shape
    tk = _tile(k, 64)
    return pl.pallas_call(
        functools.partial(_format_w_in_kernel, pieces=tuple(pieces)),
        grid=(k // tk,),
        in_specs=[pl.BlockSpec((None, tk, c_in), lambda i: (layer, i, 0))],
        out_specs=pl.BlockSpec((tk, total), lambda i: (i, 0)),
        out_shape=jax.ShapeDtypeStruct((k, total), BF16),
        compiler_params=_cparams("parallel"),
        name="format_w_in",
    )(w_in_stacked)


def _layer(x2d, memn_src, bsz, s, p, w_in_stacked, layer):
    d = x2d.shape[1]
    w_a = p["lru_lambda"].shape[0]
    n_heads_b, n_head_b = p["rwkv_r_k"].shape
    w_b = n_heads_b * n_head_b
    w_c = p["w_branch_c"].shape[0]
    w_x = p["w_branch_x"].shape[0]
    qk2 = p["mlstm_conv_w"].shape[1]
    n_heads_c = p["mlstm_b_i"].shape[0]
    b_shift_w = p["rwkv_mu"].shape[0]
    sizes = (w_a, w_a, b_shift_w, w_b, qk2, w_c, w_c, w_c, 2 * n_heads_c, w_x, w_x, 4 * d)
    lo_w = _round_up(b_shift_w - 3 * w_b, 256)
    n_tile = 768
    offs, pieces, total = _proj_layout(sizes, w_b, lo_w, LANES, n_tile)

    h = rmsnorm(x2d, p["norm_g"], BF16)
    proj = matmul(h, format_w_in(w_in_stacked, layer, pieces, total), F32, 1024, n_tile, name="in_proj")

    y_a = lru_branch(proj, offs["a_x"], offs["a_g"], bsz, s, p["lru_conv_w"], p["lru_conv_b"],
                     p["lru_wa"], p["lru_ba"], p["lru_wx"], p["lru_bx"], p["lru_lambda"])

    r, w, k, v, kk, kka, bonus = rwkv_prep(
        proj, (offs["b_r"], offs["b_k"], offs["b_v"], offs["b_lo"], lo_w), bsz, s, p["rwkv_mu"],
        p["rwkv_w0"], p["rwkv_w_up"], p["rwkv_a0"], p["rwkv_a_up"], p["rwkv_k_k"], p["rwkv_k_a"],
        p["rwkv_r_k"], n_head_b)
    to_e = functools.partial(to_chain_layout, bsz=bsz, s=s, n_heads=n_heads_b, n_head=n_head_b)
    y_e = rwkv_scan(to_e(kk), to_e(w), to_e(kka), to_e(k), to_e(r), to_e(v, value_rows=True))
    y_b = rwkv_out(from_chain_layout(y_e, bsz, n_heads_b, n_head_b), bonus, proj, offs["b_g"],
                   p["rwkv_gn_w"], p["rwkv_gn_b"], n_head_b)

    y_c = mlstm_branch(proj, (offs["c_qk"], offs["c_v"], offs["c_o"], offs["c_g"], offs["c_if"], LANES),
                       bsz, s, p["mlstm_conv_w"], p["mlstm_conv_b"], p["mlstm_b_i"], p["mlstm_b_f"],
                       p["mlstm_gn_w"], w_c)

    memn = rmsnorm(memn_src, p["mem_norm_g"], BF16)
    kv = matmul(memn, p["xattn_w_kv"], F32, 512, 512, name="mem_kv")
    y_x = xattn_branch(proj, offs["x_q"], offs["x_g"], bsz, s, kv, w_x)

    merged = merge((y_a, y_b, y_c, y_x),
                   tuple(p[n].astype(BF16) for n in ("w_branch_a", "w_branch_b", "w_branch_c", "w_branch_x")),
                   proj, offs["gates"], d)
    return matmul(merged, p["w_out"], F32, 1024, 512, residual=x2d, name="out_proj")


_LAYER_PARAMS = (
    "norm_g", "mem_norm_g", "w_in", "lru_conv_w", "lru_conv_b", "lru_wa", "lru_ba", "lru_wx", "lru_bx",
    "lru_lambda", "rwkv_mu", "rwkv_w0", "rwkv_w_up", "rwkv_a0", "rwkv_a_up", "rwkv_k_k", "rwkv_k_a",
    "rwkv_r_k", "rwkv_gn_w", "rwkv_gn_b", "mlstm_conv_w", "mlstm_conv_b", "mlstm_b_i", "mlstm_b_f",
    "mlstm_gn_w", "xattn_w_kv", "w_branch_a", "w_branch_b", "w_branch_c", "w_branch_x", "w_out")


def kernel(x, mem, norm_g, mem_norm_g, w_in, lru_conv_w, lru_conv_b, lru_wa, lru_ba, lru_wx, lru_bx, lru_lambda, rwkv_mu, rwkv_w0, rwkv_w_up, rwkv_a0, rwkv_a_up, rwkv_k_k, rwkv_k_a, rwkv_r_k, rwkv_gn_w, rwkv_gn_b, mlstm_conv_w, mlstm_conv_b, mlstm_b_i, mlstm_b_f, mlstm_gn_w, xattn_w_kv, w_branch_a, w_branch_b, w_branch_c, w_branch_x, w_out, final_norm_g):
    stacked = dict(zip(_LAYER_PARAMS, (
        norm_g, mem_norm_g, w_in, lru_conv_w, lru_conv_b, lru_wa, lru_ba, lru_wx, lru_bx, lru_lambda,
        rwkv_mu, rwkv_w0, rwkv_w_up, rwkv_a0, rwkv_a_up, rwkv_k_k, rwkv_k_a, rwkv_r_k, rwkv_gn_w,
        rwkv_gn_b, mlstm_conv_w, mlstm_conv_b, mlstm_b_i, mlstm_b_f, mlstm_gn_w, xattn_w_kv,
        w_branch_a, w_branch_b, w_branch_c, w_branch_x, w_out)))
    bsz, s, d = x.shape
    x2d = x.reshape(bsz * s, d)
    mem2d = mem.reshape(-1, d)
    for layer in range(norm_g.shape[0]):
        params = {name: val[layer] for name, val in stacked.items() if name != "w_in"}
        x2d = _layer(x2d, mem2d, bsz, s, params, w_in, layer)
    return rmsnorm(x2d, final_norm_g, x.dtype).reshape(bsz, s, d)
```

```python
import functools

import jax
import jax.numpy as jnp
import numpy as np
from jax import lax
from jax.experimental import pallas as pl
from jax.experimental.pallas import tpu as pltpu

F32 = jnp.float32
BF16 = jnp.bfloat16
HIGHEST = lax.Precision.HIGHEST

LANES = 128
SUBLANES = 8
VMEM_LIMIT = 56 * 1024 * 1024

RMS_EPS = 1e-6
LRU_C = 8.0
RWKV_GN_EPS = 64e-5
MLSTM_GN_EPS = 1e-6
DH_X = 128


def _cparams(*sem):
    return pltpu.CompilerParams(dimension_semantics=sem, vmem_limit_bytes=VMEM_LIMIT)


def _tile(n, want):
    t = min(n, want)
    assert n % t == 0, (n, want)
    return t


def _silu(x):
    return x * jax.nn.sigmoid(x)


def _rmsnorm_kernel(x_ref, g_ref, o_ref):
    x = x_ref[...]
    ms = jnp.mean(x * x, axis=-1, keepdims=True)
    o_ref[...] = (x * lax.rsqrt(ms + RMS_EPS) * g_ref[...]).astype(o_ref.dtype)


def rmsnorm(x2d, g, out_dtype):
    m, d = x2d.shape
    tm = _tile(m, 256)
    return pl.pallas_call(
        _rmsnorm_kernel,
        grid=(m // tm,),
        in_specs=[pl.BlockSpec((tm, d), lambda i: (i, 0)),
                  pl.BlockSpec((1, d), lambda i: (0, 0))],
        out_specs=pl.BlockSpec((tm, d), lambda i: (i, 0)),
        out_shape=jax.ShapeDtypeStruct((m, d), out_dtype),
        compiler_params=_cparams("parallel"),
        name="rmsnorm",
    )(x2d, g.reshape(1, d))


def _mm_kernel(a_ref, b_ref, o_ref):
    acc = jnp.dot(a_ref[...], b_ref[...].astype(BF16), preferred_element_type=F32)
    o_ref[...] = acc.astype(o_ref.dtype)


def _mm_res_kernel(a_ref, b_ref, r_ref, o_ref):
    acc = jnp.dot(a_ref[...], b_ref[...].astype(BF16), preferred_element_type=F32)
    o_ref[...] = (r_ref[...] + acc).astype(o_ref.dtype)


def matmul(a, b, out_dtype, tm, tn, residual=None, name="matmul"):
    m, k = a.shape
    _, n = b.shape
    tm, tn = _tile(m, tm), _tile(n, tn)
    in_specs = [pl.BlockSpec((tm, k), lambda j, i: (i, 0)),
                pl.BlockSpec((k, tn), lambda j, i: (0, j))]
    args = [a, b]
    kern = _mm_kernel
    if residual is not None:
        in_specs.append(pl.BlockSpec((tm, tn), lambda j, i: (i, j)))
        args.append(residual)
        kern = _mm_res_kernel
    return pl.pallas_call(
        kern,
        grid=(n // tn, m // tm),
        in_specs=in_specs,
        out_specs=pl.BlockSpec((tm, tn), lambda j, i: (i, j)),
        out_shape=jax.ShapeDtypeStruct((m, n), out_dtype),
        compiler_params=_cparams("parallel", "parallel"),
        name=name,
    )(*args)


def _merge_kernel(ya_ref, yb_ref, yc_ref, yx_ref, wa_ref, wb_ref, wc_ref, wx_ref,
                  g0_ref, g1_ref, g2_ref, g3_ref, o_ref):
    def term(y_ref, w_ref, g_ref):
        return jax.nn.sigmoid(g_ref[...]) * jnp.dot(y_ref[...], w_ref[...], preferred_element_type=F32)

    acc = term(ya_ref, wa_ref, g0_ref)
    acc = acc + term(yb_ref, wb_ref, g1_ref)
    acc = acc + term(yc_ref, wc_ref, g2_ref)
    acc = acc + term(yx_ref, wx_ref, g3_ref)
    o_ref[...] = acc.astype(o_ref.dtype)


def merge(ys, ws, proj, gate_off, d):
    m = proj.shape[0]
    tm, tn = _tile(m, 512), _tile(d, 512)
    assert gate_off % tn == 0
    y_specs = [pl.BlockSpec((tm, y.shape[1]), lambda j, i: (i, 0)) for y in ys]
    w_specs = [pl.BlockSpec((w.shape[0], tn), lambda j, i: (0, j)) for w in ws]
    g_specs = [pl.BlockSpec((tm, tn), functools.partial(
        lambda j, i, base: (i, base + j), base=(gate_off + k * d) // tn)) for k in range(4)]
    return pl.pallas_call(
        _merge_kernel,
        grid=(d // tn, m // tm),
        in_specs=y_specs + w_specs + g_specs,
        out_specs=pl.BlockSpec((tm, tn), lambda j, i: (i, j)),
        out_shape=jax.ShapeDtypeStruct((m, d), BF16),
        compiler_params=_cparams("parallel", "parallel"),
        name="merge",
    )(*ys, *ws, proj, proj, proj, proj)


def _shift_rows(x, shift, hist):
    rolled = pltpu.roll(x, shift, axis=0)
    head = pltpu.roll(hist, shift, axis=0)
    row = lax.broadcasted_iota(jnp.int32, (SUBLANES, x.shape[1]), 0)
    fixed = jnp.where(row < shift, head, rolled[:SUBLANES])
    return jnp.concatenate([fixed, rolled[SUBLANES:]], axis=0)


def _causal_conv(x, hist, w, b):
    kw = w.shape[0]
    out = x * w[kw - 1:kw] + b
    for j in range(kw - 1):
        out = out + _shift_rows(x, kw - 1 - j, hist) * w[j:j + 1]
    return out


def _lru_kernel(ax_ref, ag_ref, cw_ref, cb_ref, wa_ref, ba_ref, wx_ref, bx_ref, lam_ref, o_ref,
                hist_sc, hcar_sc, a_sc, b_sc, h_sc):
    t = pl.program_id(2)

    @pl.when(t == 0)
    def _():
        hist_sc[...] = jnp.zeros_like(hist_sc)
        hcar_sc[...] = jnp.zeros_like(hcar_sc)

    x = ax_ref[...]
    rows = x.shape[0]
    u = _causal_conv(x, hist_sc[...], cw_ref[...], cb_ref[...])
    hist_sc[...] = x[rows - SUBLANES:]
    ub = u.astype(BF16)
    r = jax.nn.sigmoid(jnp.dot(ub, wa_ref[...], preferred_element_type=F32) + ba_ref[...])
    i = jax.nn.sigmoid(jnp.dot(ub, wx_ref[...], preferred_element_type=F32) + bx_ref[...])
    log_a = -LRU_C * r * jax.nn.softplus(-lam_ref[...])
    a = jnp.exp(log_a)
    b = jnp.sqrt(-jnp.tanh(log_a) * (a * a + 1.0)) * (i * u)

    in_group = lax.broadcasted_iota(jnp.int32, x.shape, 0) % SUBLANES
    for s in (1, 2, 4):
        keep = in_group >= s
        b = jnp.where(keep, a * pltpu.roll(b, s, axis=0) + b, b)
        a = jnp.where(keep, a * pltpu.roll(a, s, axis=0), a)
    a_sc[...] = a
    b_sc[...] = b

    def group(g, h_prev):
        sl = pl.ds(pl.multiple_of(g * SUBLANES, SUBLANES), SUBLANES)
        h8 = a_sc[sl, :] * h_prev + b_sc[sl, :]
        h_sc[sl, :] = h8
        return h8[SUBLANES - 1:SUBLANES]

    hcar_sc[...] = lax.fori_loop(0, rows // SUBLANES, group, hcar_sc[...], unroll=4)
    o_ref[...] = (h_sc[...] * _silu(ag_ref[...])).astype(o_ref.dtype)


def lru_branch(proj, off_x, off_g, bsz, s, cw, cb, wa, ba, wx, bx, lam):
    nb, bs, _ = wa.shape
    w = nb * bs
    tt = _tile(s, 512)
    nt = s // tt
    assert off_x % bs == 0 and off_g % bs == 0
    vec = lambda v: v.reshape(1, w)
    vspec = pl.BlockSpec((1, bs), lambda b, n, t: (0, n))
    return pl.pallas_call(
        _lru_kernel,
        grid=(bsz, nb, nt),
        in_specs=[
            pl.BlockSpec((tt, bs), lambda b, n, t: (b * nt + t, off_x // bs + n)),
            pl.BlockSpec((tt, bs), lambda b, n, t: (b * nt + t, off_g // bs + n)),
            pl.BlockSpec((cw.shape[0], bs), lambda b, n, t: (0, n)),
            vspec,
            pl.BlockSpec((None, bs, bs), lambda b, n, t: (n, 0, 0)),
            vspec,
            pl.BlockSpec((None, bs, bs), lambda b, n, t: (n, 0, 0)),
            vspec,
            vspec,
        ],
        out_specs=pl.BlockSpec((tt, bs), lambda b, n, t: (b * nt + t, n)),
        out_shape=jax.ShapeDtypeStruct((bsz * s, w), BF16),
        scratch_shapes=[pltpu.VMEM((SUBLANES, bs), F32), pltpu.VMEM((1, bs), F32),
                        pltpu.VMEM((tt, bs), F32), pltpu.VMEM((tt, bs), F32), pltpu.VMEM((tt, bs), F32)],
        compiler_params=_cparams("parallel", "parallel", "arbitrary"),
        name="rg_lru",
    )(proj, proj, cw, vec(cb), wa.astype(BF16), vec(ba), wx.astype(BF16), vec(bx), vec(lam))


def _seg_sum(x, g):
    parts = [jnp.dot(x[:, c * LANES:(c + 1) * LANES], g, precision=HIGHEST, preferred_element_type=F32)
             for c in range(x.shape[1] // LANES)]
    return parts[0] if len(parts) == 1 else jnp.concatenate(parts, axis=1)


def _rwkv_prep_kernel(pr_ref, pk_ref, pv_ref, plo_ref, mur_ref, muk_ref, muv_ref, mulo_ref,
                      w0_ref, wup_ref, a0_ref, aup_ref, kk_ref, ka_ref, rk_ref, g_ref,
                      r_o, w_o, k_o, v_o, kk_o, kka_o, bonus_o,
                      lr_sc, lk_sc, lv_sc, llo_sc):
    t = pl.program_id(1)

    @pl.when(t == 0)
    def _():
        for sc in (lr_sc, lk_sc, lv_sc, llo_sc):
            sc[...] = jnp.zeros_like(sc)

    def shifted(p_ref, mu_ref, last_sc):
        p = p_ref[...]
        rows = p.shape[0]
        prev = _shift_rows(p, 1, last_sc[...])
        last_sc[...] = p[rows - SUBLANES:]
        return p + (prev - p) * mu_ref[...]

    r = shifted(pr_ref, mur_ref, lr_sc)
    k = shifted(pk_ref, muk_ref, lk_sc)
    v = shifted(pv_ref, muv_ref, lv_sc)
    lo = shifted(plo_ref, mulo_ref, llo_sc)

    w_pre = w0_ref[...] + jnp.dot(jnp.tanh(lo).astype(BF16), wup_ref[...], preferred_element_type=F32)
    w_log = -jax.nn.softplus(-w_pre) - 0.5
    decay = jnp.exp(-jnp.exp(w_log))
    a = jax.nn.sigmoid(a0_ref[...] + jnp.dot(lo.astype(BF16), aup_ref[...], preferred_element_type=F32))

    g = g_ref[...]
    kk = k * kk_ref[...]
    kk = kk / jnp.maximum(jnp.sqrt(_seg_sum(kk * kk, g)), 1e-12)
    k2 = k * (1.0 + (a - 1.0) * ka_ref[...])
    r_o[...] = r
    w_o[...] = decay
    k_o[...] = k2
    v_o[...] = v
    kk_o[...] = kk
    kka_o[...] = kk * a
    bonus_o[...] = _seg_sum(r * k2 * rk_ref[...], g) * v


def rwkv_prep(proj, offs, bsz, s, mu, w0, w_up, a0, a_up, k_k, k_a, r_k, n_head):
    off_r, off_k, off_v, off_lo, lo_w = offs
    r_w, w = w_up.shape
    r_a = a_up.shape[0]
    tt = _tile(s, 256)
    nt = s // tt
    assert off_r % w == 0 and off_k % w == 0 and off_v % w == 0 and off_lo % lo_w == 0
    assert LANES % n_head == 0 and w % LANES == 0
    vec = lambda x: x.reshape(1, -1)
    mu_lo = jnp.pad(mu[3 * w:], (0, lo_w - r_w - r_a))
    wup_pad = jnp.zeros((lo_w, w), F32).at[:r_w].set(w_up).astype(BF16)
    aup_pad = jnp.zeros((lo_w, w), F32).at[r_w:r_w + r_a].set(a_up).astype(BF16)
    gmat = jnp.asarray(np.kron(np.eye(LANES // n_head), np.ones((n_head, n_head))), F32)
    row_w = pl.BlockSpec((1, w), lambda b, t: (0, 0))
    blk = lambda off, width: pl.BlockSpec((tt, width), lambda b, t: (b * nt + t, off // width))
    out_spec = pl.BlockSpec((tt, w), lambda b, t: (b * nt + t, 0))
    out_shape = jax.ShapeDtypeStruct((bsz * s, w), F32)
    return pl.pallas_call(
        _rwkv_prep_kernel,
        grid=(bsz, nt),
        in_specs=[blk(off_r, w), blk(off_k, w), blk(off_v, w), blk(off_lo, lo_w),
                  row_w, row_w, row_w, pl.BlockSpec((1, lo_w), lambda b, t: (0, 0)),
                  row_w, pl.BlockSpec((lo_w, w), lambda b, t: (0, 0)),
                  row_w, pl.BlockSpec((lo_w, w), lambda b, t: (0, 0)),
                  row_w, row_w, row_w,
                  pl.BlockSpec((LANES, LANES), lambda b, t: (0, 0))],
        out_specs=[out_spec] * 7,
        out_shape=[out_shape] * 7,
        scratch_shapes=[pltpu.VMEM((SUBLANES, w), F32)] * 3 + [pltpu.VMEM((SUBLANES, lo_w), F32)],
        compiler_params=_cparams("parallel", "arbitrary"),
        name="rwkv_prep",
    )(proj, proj, proj, proj, vec(mu[:w]), vec(mu[w:2 * w]), vec(mu[2 * w:3 * w]), vec(mu_lo),
      vec(w0), wup_pad, vec(a0), aup_pad, vec(k_k), vec(k_a), vec(r_k), gmat)


def _rwkv_scan_kernel(kk_ref, w_ref, kka_ref, k_ref, r_ref, v_ref, y_ref, s_sc):
    c = pl.program_id(0)
    steps = kk_ref.shape[1]
    n_k, n_rows, _ = s_sc.shape
    blocks = [pl.ds(b * SUBLANES, SUBLANES) for b in range(n_rows // SUBLANES)]

    @pl.when(c == 0)
    def _():
        s_sc[...] = jnp.zeros_like(s_sc)

    def tree_sum(parts):
        while len(parts) > 1:
            parts = [parts[i] + parts[i + 1] for i in range(0, len(parts), 2)]
        return parts[0]

    sa0 = []
    for blk in blocks:
        acc = [None, None]
        for j in range(n_k):
            term = s_sc[j, blk, :] * kk_ref[j, 0:1, :]
            acc[j % 2] = term if acc[j % 2] is None else acc[j % 2] + term
        sa0.append(acc[0] + acc[1])

    def step(t, sa, with_next):
        v = [v_ref[t, blk, :] for blk in blocks]
        yacc = [[None, None] for _ in blocks]
        zacc = [[None, None] for _ in blocks]
        row = pl.ds(t, 1)
        for j in range(n_k):
            w_j, kka_j, k_j, r_j = w_ref[j, row, :], kka_ref[j, row, :], k_ref[j, row, :], r_ref[j, row, :]
            kk_j = kk_ref[j, pl.ds(t + 1, 1), :] if with_next else None
            for b, blk in enumerate(blocks):
                st = s_sc[j, blk, :] * w_j - sa[b] * kka_j + v[b] * k_j
                s_sc[j, blk, :] = st
                ya = st * r_j
                yacc[b][j % 2] = ya if yacc[b][j % 2] is None else yacc[b][j % 2] + ya
                if with_next:
                    za = st * kk_j
                    zacc[b][j % 2] = za if zacc[b][j % 2] is None else zacc[b][j % 2] + za
        for b, blk in enumerate(blocks):
            y_ref[t, blk, :] = tree_sum(yacc[b])
        return tuple(tree_sum(z) for z in zacc) if with_next else None

    sa = lax.fori_loop(0, steps - 1, lambda t, sa: step(t, sa, True), tuple(sa0))
    step(steps - 1, sa, False)


def rwkv_scan(kk_e, w_e, kka_e, k_e, r_e, v_e):
    n_k, s, lanes = kk_e.shape
    n_rows = v_e.shape[1]
    assert n_rows % SUBLANES == 0
    tc = _tile(s, 64)
    vec_spec = pl.BlockSpec((n_k, tc, lanes), lambda c: (0, c, 0))
    row_spec = pl.BlockSpec((tc, n_rows, lanes), lambda c: (c, 0, 0))
    return pl.pallas_call(
        _rwkv_scan_kernel,
        grid=(s // tc,),
        in_specs=[vec_spec] * 5 + [row_spec],
        out_specs=row_spec,
        out_shape=jax.ShapeDtypeStruct((s, n_rows, lanes), F32),
        scratch_shapes=[pltpu.VMEM((n_k, n_rows, lanes), F32)],
        compiler_params=_cparams("arbitrary"),
        name="rwkv_scan",
    )(kk_e, w_e, kka_e, k_e, r_e, v_e)


def _rwkv_out_kernel(y_ref, bonus_ref, gate_ref, gw_ref, gb_ref, g_ref, o_ref, *, n_head):
    y = y_ref[...]
    g = g_ref[...] * (1.0 / n_head)
    mu = _seg_sum(y, g)
    yc = y - mu
    var = _seg_sum(yc * yc, g)
    yn = yc * lax.rsqrt(var + RWKV_GN_EPS) * gw_ref[...] + gb_ref[...]
    o_ref[...] = ((yn + bonus_ref[...]) * _silu(gate_ref[...])).astype(o_ref.dtype)


def rwkv_out(y, bonus, proj, off_g, gn_w, gn_b, n_head):
    m, w = y.shape
    tt = _tile(m, 256)
    assert off_g % w == 0
    gmat = jnp.asarray(np.kron(np.eye(LANES // n_head), np.ones((n_head, n_head))), F32)
    blk = pl.BlockSpec((tt, w), lambda i: (i, 0))
    row = pl.BlockSpec((1, w), lambda i: (0, 0))
    return pl.pallas_call(
        functools.partial(_rwkv_out_kernel, n_head=n_head),
        grid=(m // tt,),
        in_specs=[blk, blk, pl.BlockSpec((tt, w), lambda i: (i, off_g // w)), row, row,
                  pl.BlockSpec((LANES, LANES), lambda i: (0, 0))],
        out_specs=blk,
        out_shape=jax.ShapeDtypeStruct((m, w), BF16),
        compiler_params=_cparams("parallel"),
        name="rwkv_out",
    )(y, bonus, proj, gn_w.reshape(1, w), gn_b.reshape(1, w), gmat)


def _to_chain_kernel(x_ref, o_ref, *, n_heads, n_head, dup, value_rows):
    bsz, t, _ = x_ref.shape
    parts = []
    for b in range(bsz):
        x3 = x_ref[b].T.reshape(n_heads, n_head, t)
        parts.append(jnp.transpose(x3, (1, 0, 2)))
    if value_rows:
        nr = n_head // dup
        e = jnp.concatenate([p[d * nr:(d + 1) * nr] for d in range(dup) for p in parts], axis=1)
        o_ref[...] = jnp.transpose(jnp.swapaxes(e, 1, 2), (1, 0, 2))
    else:
        e = jnp.concatenate(parts * dup, axis=1)
        o_ref[...] = jnp.swapaxes(e, 1, 2)


def to_chain_layout(x, bsz, s, n_heads, n_head, value_rows=False):
    w = n_heads * n_head
    assert LANES % (bsz * n_heads) == 0
    dup = LANES // (bsz * n_heads)
    assert n_head % dup == 0
    tt = _tile(s, LANES)
    if value_rows:
        out_shape, out_block, out_idx = (s, n_head // dup, LANES), (tt, n_head // dup, LANES), lambda i: (i, 0, 0)
    else:
        out_shape, out_block, out_idx = (n_head, s, LANES), (n_head, tt, LANES), lambda i: (0, i, 0)
    return pl.pallas_call(
        functools.partial(_to_chain_kernel, n_heads=n_heads, n_head=n_head, dup=dup, value_rows=value_rows),
        grid=(s // tt,),
        in_specs=[pl.BlockSpec((bsz, tt, w), lambda i: (0, i, 0))],
        out_specs=pl.BlockSpec(out_block, out_idx),
        out_shape=jax.ShapeDtypeStruct(out_shape, F32),
        compiler_params=_cparams("parallel"),
        name="to_chain_rows" if value_rows else "to_chain_keys",
    )(x.reshape(bsz, s, w))


def _from_chain_kernel(y_ref, o_ref, *, n_heads, n_head, dup):
    bsz = o_ref.shape[0]
    t = y_ref.shape[0]
    e = jnp.swapaxes(jnp.transpose(y_ref[...], (1, 0, 2)), 1, 2)
    bh = bsz * n_heads
    for b in range(bsz):
        rows = jnp.concatenate(
            [e[:, d * bh + b * n_heads:d * bh + (b + 1) * n_heads, :] for d in range(dup)], axis=0)
        o_ref[b] = jnp.transpose(rows, (1, 0, 2)).reshape(n_heads * n_head, t).T


def from_chain_layout(y_e, bsz, n_heads, n_head):
    s, n_rows, _ = y_e.shape
    w = n_heads * n_head
    dup = LANES // (bsz * n_heads)
    tt = _tile(s, LANES)
    return pl.pallas_call(
        functools.partial(_from_chain_kernel, n_heads=n_heads, n_head=n_head, dup=dup),
        grid=(s // tt,),
        in_specs=[pl.BlockSpec((tt, n_rows, LANES), lambda i: (i, 0, 0))],
        out_specs=pl.BlockSpec((bsz, tt, w), lambda i: (0, i, 0)),
        out_shape=jax.ShapeDtypeStruct((bsz, s, w), F32),
        compiler_params=_cparams("parallel"),
        name="from_chain_rows",
    )(y_e).reshape(bsz * s, w)


def _mlstm_kernel(qk_ref, v_ref, o_ref, g_ref, if_ref, cw_ref, cb_ref, bias_ref, gw_ref, tril_ref,
                  out_ref, hist_sc, c_sc, m_sc, *, n_heads, dqk, dv):
    t = pl.program_id(1)

    @pl.when(t == 0)
    def _():
        hist_sc[...] = jnp.zeros_like(hist_sc)
        c_sc[...] = jnp.zeros_like(c_sc)
        m_sc[...] = jnp.full_like(m_sc, -jnp.inf)

    qk_in = qk_ref[...]
    rows = qk_in.shape[0]
    qk = _silu(_causal_conv(qk_in, hist_sc[...], cw_ref[...], cb_ref[...]))
    hist_sc[...] = qk_in[rows - SUBLANES:]

    pre = if_ref[...] + bias_ref[...]
    li = pre
    lf = jax.nn.log_sigmoid(pltpu.roll(pre, LANES - n_heads, axis=1))
    tril = tril_ref[...]
    bcum = jnp.dot(tril, lf, precision=HIGHEST, preferred_element_type=F32)
    g_tot = bcum[rows - 1:rows]
    m_prev = m_sc[...]
    le = g_tot - bcum + li
    m_new = jnp.maximum(g_tot + m_prev, jnp.max(le, axis=0, keepdims=True))
    keep = jnp.exp(g_tot + m_prev - m_new)
    wk = jnp.exp(le - m_new)
    m_sc[...] = m_new
    bcum_t = bcum.T
    li_t = li.T
    causal = tril > 0.5
    ones_blk = (lax.broadcasted_iota(jnp.int32, (rows, LANES), 1) == 0).astype(F32)

    for h in range(n_heads):
        q = qk[:, h * dqk:(h + 1) * dqk]
        k = qk[:, (n_heads + h) * dqk:(n_heads + h + 1) * dqk] * (dqk ** -0.5)
        v = v_ref[:, h * dv:(h + 1) * dv]
        v_aug = jnp.concatenate([v, ones_blk], axis=1).astype(BF16)
        b_col = bcum[:, h:h + 1]
        dmat = b_col - bcum_t[h:h + 1, :] + li_t[h:h + 1, :]
        dmat = jnp.where(causal, dmat, -jnp.inf)
        inter = b_col + m_prev[:, h:h + 1]
        m_t = jnp.maximum(inter, jnp.max(dmat, axis=-1, keepdims=True))
        w_intra = jnp.exp(dmat - m_t)
        w_inter = jnp.exp(inter - m_t)
        qb = q.astype(BF16)
        sc = lax.dot_general(qb, k.astype(BF16), (((1,), (1,)), ((), ())),
                             preferred_element_type=F32) * w_intra
        c_aug = c_sc[h]
        tot = (w_inter * jnp.dot(qb, c_aug.astype(BF16), preferred_element_type=F32)
               + jnp.dot(sc.astype(BF16), v_aug, preferred_element_type=F32))
        num = tot[:, :dv]
        den = tot[:, dv:dv + 1]
        hh = num / jnp.maximum(jnp.abs(den), jnp.exp(-m_t))
        kw = (k * wk[:, h:h + 1]).astype(BF16)
        c_sc[h] = keep[:, h:h + 1] * c_aug + lax.dot_general(
            kw, v_aug, (((0,), (0,)), ((), ())), preferred_element_type=F32)

        ho = hh * jax.nn.sigmoid(o_ref[:, h * dv:(h + 1) * dv])
        mu = jnp.mean(ho, axis=-1, keepdims=True)
        hc = ho - mu
        var = jnp.mean(hc * hc, axis=-1, keepdims=True)
        hn = hc * lax.rsqrt(var + MLSTM_GN_EPS) * gw_ref[:, h * dv:(h + 1) * dv]
        out_ref[:, h * dv:(h + 1) * dv] = (hn * _silu(g_ref[:, h * dv:(h + 1) * dv])).astype(out_ref.dtype)


def mlstm_branch(proj, offs, bsz, s, conv_w, conv_b, b_i, b_f, gn_w, w_c):
    off_qk, off_v, off_o, off_g, off_if, if_w = offs
    n_heads = b_i.shape[0]
    qk2 = conv_w.shape[1]
    dqk = qk2 // 2 // n_heads
    dv = w_c // n_heads
    ll = _tile(s, 256)
    nt = s // ll
    assert off_qk % qk2 == 0 and off_v % w_c == 0 and off_o % w_c == 0 and off_g % w_c == 0
    assert off_if % if_w == 0 and if_w == LANES and 2 * n_heads <= LANES
    bias = jnp.pad(jnp.concatenate([b_i, b_f]), (0, LANES - 2 * n_heads)).reshape(1, LANES)
    tril = jnp.asarray(np.tril(np.ones((ll, ll))), F32)
    blk = lambda off, width: pl.BlockSpec((ll, width), lambda b, t: (b * nt + t, off // width))
    full = lambda shape: pl.BlockSpec(shape, lambda b, t: (0,) * len(shape))
    return pl.pallas_call(
        functools.partial(_mlstm_kernel, n_heads=n_heads, dqk=dqk, dv=dv),
        grid=(bsz, nt),
        in_specs=[blk(off_qk, qk2), blk(off_v, w_c), blk(off_o, w_c), blk(off_g, w_c), blk(off_if, if_w),
                  full(conv_w.shape), full((1, qk2)), full((1, LANES)), full((1, w_c)), full((ll, ll))],
        out_specs=pl.BlockSpec((ll, w_c), lambda b, t: (b * nt + t, 0)),
        out_shape=jax.ShapeDtypeStruct((bsz * s, w_c), BF16),
        scratch_shapes=[pltpu.VMEM((SUBLANES, qk2), F32),
                        pltpu.VMEM((n_heads, dqk, dv + LANES), F32),
                        pltpu.VMEM((1, LANES), F32)],
        compiler_params=_cparams("parallel", "arbitrary"),
        name="mlstm",
    )(proj, proj, proj, proj, proj, conv_w, conv_b.reshape(1, qk2), bias, gn_w.reshape(1, w_c), tril)


def _xattn_kernel(q_ref, gate_ref, kv_ref, o_ref, *, n_heads):
    w_x = q_ref.shape[1]
    for h in range(n_heads):
        sl = slice(h * DH_X, (h + 1) * DH_X)
        q = q_ref[:, sl].astype(BF16)
        k = kv_ref[:, sl].astype(BF16)
        v = kv_ref[:, w_x + h * DH_X:w_x + (h + 1) * DH_X].astype(BF16)
        logits = lax.dot_general(q, k, (((1,), (1,)), ((), ())), preferred_element_type=F32) * (DH_X ** -0.5)
        e = jnp.exp(logits - jnp.max(logits, axis=-1, keepdims=True))
        probs = e / jnp.sum(e, axis=-1, keepdims=True)
        o = jnp.dot(probs.astype(BF16), v, preferred_element_type=F32)
        o_ref[:, sl] = (o * _silu(gate_ref[:, sl])).astype(o_ref.dtype)


def xattn_branch(proj, off_q, off_g, bsz, s, kv, w_x):
    m_len = kv.shape[0] // bsz
    tt = _tile(s, 512)
    nt = s // tt
    assert off_q % w_x == 0 and off_g % w_x == 0 and w_x % DH_X == 0
    return pl.pallas_call(
        functools.partial(_xattn_kernel, n_heads=w_x // DH_X),
        grid=(bsz, nt),
        in_specs=[pl.BlockSpec((tt, w_x), lambda b, t: (b * nt + t, off_q // w_x)),
                  pl.BlockSpec((tt, w_x), lambda b, t: (b * nt + t, off_g // w_x)),
                  pl.BlockSpec((m_len, 2 * w_x), lambda b, t: (b, 0))],
        out_specs=pl.BlockSpec((tt, w_x), lambda b, t: (b * nt + t, 0)),
        out_shape=jax.ShapeDtypeStruct((bsz * s, w_x), BF16),
        compiler_params=_cparams("parallel", "parallel"),
        name="mem_xattn",
    )(proj, proj, kv)


def _round_up(n, m):
    return (n + m - 1) // m * m


def _proj_layout(sizes, w_b, tn):
    src = np.concatenate([[0], np.cumsum(sizes)])
    seg = {name: (int(src[i]), int(sizes[i])) for i, name in enumerate(
        ("a_x", "a_g", "b_s", "b_g", "c_qk", "c_v", "c_o", "c_g", "c_if", "x_q", "x_g", "gates"))}
    b0 = seg["b_s"][0]
    seg["b_r"], seg["b_k"], seg["b_v"] = (b0, w_b), (b0 + w_b, w_b), (b0 + 2 * w_b, w_b)
    seg["b_lo"] = (b0 + 3 * w_b, seg["b_s"][1] - 3 * w_b)
    order = ("a_x", "a_g", "b_g", "c_qk", "c_v", "c_o", "c_g", "b_r", "b_k", "b_v", "gates",
             "x_q", "x_g", "b_lo", "c_if")
    offs, block_src = {}, []
    for name in order:
        start, width = seg[name]
        assert start % SUBLANES == 0 and (width % tn == 0 or width < tn), (name, start, width)
        offs[name] = len(block_src) * tn
        n_blocks = max(1, width // tn)
        assert start + n_blocks * tn <= src[-1]
        block_src.extend(start + b * tn for b in range(n_blocks))
    return offs, block_src


def _in_proj_kernel(src_ref, h_ref, w_ref, o_ref):
    del src_ref
    o_ref[...] = lax.dot_general(h_ref[...], w_ref[...].astype(BF16), (((1,), (1,)), ((), ())),
                                 preferred_element_type=F32)


def in_proj(h, w_in_stacked, layer, block_src, tn):
    m, k = h.shape
    n_layers, _, c_in = w_in_stacked.shape
    assert c_in % SUBLANES == 0
    w_t = jnp.swapaxes(w_in_stacked, 1, 2).reshape(n_layers * c_in, k)
    rows8 = jnp.asarray([(layer * c_in + c) // SUBLANES for c in block_src], jnp.int32)
    tm = _tile(m, 1024)
    n_blocks = len(block_src)
    grid_spec = pltpu.PrefetchScalarGridSpec(
        num_scalar_prefetch=1,
        grid=(n_blocks, m // tm),
        in_specs=[pl.BlockSpec((tm, k), lambda j, i, src: (i, 0)),
                  pl.BlockSpec((pl.Element(tn), pl.Element(k)), lambda j, i, src: (src[j] * SUBLANES, 0))],
        out_specs=pl.BlockSpec((tm, tn), lambda j, i, src: (i, j)))
    return pl.pallas_call(
        _in_proj_kernel,
        grid_spec=grid_spec,
        out_shape=jax.ShapeDtypeStruct((m, n_blocks * tn), F32),
        compiler_params=_cparams("parallel", "parallel"),
        name="in_proj",
    )(rows8, h, w_t)


def _layer(x2d, memn_src, bsz, s, p, w_in_stacked, layer):
    d = x2d.shape[1]
    w_a = p["lru_lambda"].shape[0]
    n_heads_b, n_head_b = p["rwkv_r_k"].shape
    w_b = n_heads_b * n_head_b
    w_c = p["w_branch_c"].shape[0]
    w_x = p["w_branch_x"].shape[0]
    qk2 = p["mlstm_conv_w"].shape[1]
    n_heads_c = p["mlstm_b_i"].shape[0]
    b_shift_w = p["rwkv_mu"].shape[0]
    sizes = (w_a, w_a, b_shift_w, w_b, qk2, w_c, w_c, w_c, 2 * n_heads_c, w_x, w_x, 4 * d)
    lo_w = _round_up(b_shift_w - 3 * w_b, 256)
    tn = min(512, w_a, w_b, w_c, w_x, qk2)
    assert tn % lo_w == 0
    offs, block_src = _proj_layout(sizes, w_b, tn)

    h = rmsnorm(x2d, p["norm_g"], BF16)
    proj = in_proj(h, w_in_stacked, layer, block_src, tn)

    y_a = lru_branch(proj, offs["a_x"], offs["a_g"], bsz, s, p["lru_conv_w"], p["lru_conv_b"],
                     p["lru_wa"], p["lru_ba"], p["lru_wx"], p["lru_bx"], p["lru_lambda"])

    r, w, k, v, kk, kka, bonus = rwkv_prep(
        proj, (offs["b_r"], offs["b_k"], offs["b_v"], offs["b_lo"], lo_w), bsz, s, p["rwkv_mu"],
        p["rwkv_w0"], p["rwkv_w_up"], p["rwkv_a0"], p["rwkv_a_up"], p["rwkv_k_k"], p["rwkv_k_a"],
        p["rwkv_r_k"], n_head_b)
    to_e = functools.partial(to_chain_layout, bsz=bsz, s=s, n_heads=n_heads_b, n_head=n_head_b)
    y_e = rwkv_scan(to_e(kk), to_e(w), to_e(kka), to_e(k), to_e(r), to_e(v, value_rows=True))
    y_b = rwkv_out(from_chain_layout(y_e, bsz, n_heads_b, n_head_b), bonus, proj, offs["b_g"],
                   p["rwkv_gn_w"], p["rwkv_gn_b"], n_head_b)

    y_c = mlstm_branch(proj, (offs["c_qk"], offs["c_v"], offs["c_o"], offs["c_g"], offs["c_if"], LANES),
                       bsz, s, p["mlstm_conv_w"], p["mlstm_conv_b"], p["mlstm_b_i"], p["mlstm_b_f"],
                       p["mlstm_gn_w"], w_c)

    memn = rmsnorm(memn_src, p["mem_norm_g"], BF16)
    kv = matmul(memn, p["xattn_w_kv"], F32, 512, 512, name="mem_kv")
    y_x = xattn_branch(proj, offs["x_q"], offs["x_g"], bsz, s, kv, w_x)

    merged = merge((y_a, y_b, y_c, y_x),
                   tuple(p[n].astype(BF16) for n in ("w_branch_a", "w_branch_b", "w_branch_c", "w_branch_x")),
                   proj, offs["gates"], d)
    return matmul(merged, p["w_out"], F32, 1024, 512, residual=x2d, name="out_proj")


_LAYER_PARAMS = (
    "norm_g", "mem_norm_g", "w_in", "lru_conv_w", "lru_conv_b", "lru_wa", "lru_ba", "lru_wx", "lru_bx",
    "lru_lambda", "rwkv_mu", "rwkv_w0", "rwkv_w_up", "rwkv_a0", "rwkv_a_up", "rwkv_k_k", "rwkv_k_a",
    "rwkv_r_k", "rwkv_gn_w", "rwkv_gn_b", "mlstm_conv_w", "mlstm_conv_b", "mlstm_b_i", "mlstm_b_f",
    "mlstm_gn_w", "xattn_w_kv", "w_branch_a", "w_branch_b", "w_branch_c", "w_branch_x", "w_out")


def kernel(x, mem, norm_g, mem_norm_g, w_in, lru_conv_w, lru_conv_b, lru_wa, lru_ba, lru_wx, lru_bx, lru_lambda, rwkv_mu, rwkv_w0, rwkv_w_up, rwkv_a0, rwkv_a_up, rwkv_k_k, rwkv_k_a, rwkv_r_k, rwkv_gn_w, rwkv_gn_b, mlstm_conv_w, mlstm_conv_b, mlstm_b_i, mlstm_b_f, mlstm_gn_w, xattn_w_kv, w_branch_a, w_branch_b, w_branch_c, w_branch_x, w_out, final_norm_g):
    stacked = dict(zip(_LAYER_PARAMS, (
        norm_g, mem_norm_g, w_in, lru_conv_w, lru_conv_b, lru_wa, lru_ba, lru_wx, lru_bx, lru_lambda,
        rwkv_mu, rwkv_w0, rwkv_w_up, rwkv_a0, rwkv_a_up, rwkv_k_k, rwkv_k_a, rwkv_r_k, rwkv_gn_w,
        rwkv_gn_b, mlstm_conv_w, mlstm_conv_b, mlstm_b_i, mlstm_b_f, mlstm_gn_w, xattn_w_kv,
        w_branch_a, w_branch_b, w_branch_c, w_branch_x, w_out)))
    bsz, s, d = x.shape
    x2d = x.reshape(bsz * s, d)
    mem2d = mem.reshape(-1, d)
    for layer in range(norm_g.shape[0]):
        params = {name: val[layer] for name, val in stacked.items() if name != "w_in"}
        x2d = _layer(x2d, mem2d, bsz, s, params, w_in, layer)
    return rmsnorm(x2d, final_norm_g, x.dtype).reshape(bsz, s, d)
```

```python
import functools

import jax
import jax.numpy as jnp
import numpy as np
from jax import lax
from jax.experimental import pallas as pl
from jax.experimental.pallas import tpu as pltpu

F32 = jnp.float32
BF16 = jnp.bfloat16
HIGHEST = lax.Precision.HIGHEST

LANES = 128
SUBLANES = 8
VMEM_LIMIT = 56 * 1024 * 1024

RMS_EPS = 1e-6
LRU_C = 8.0
RWKV_GN_EPS = 64e-5
MLSTM_GN_EPS = 1e-6
DH_X = 128


def _cparams(*sem):
    return pltpu.CompilerParams(dimension_semantics=sem, vmem_limit_bytes=VMEM_LIMIT)


def _tile(n, want):
    t = min(n, want)
    assert n % t == 0, (n, want)
    return t


def _silu(x):
    return x * jax.nn.sigmoid(x)


def _rmsnorm_kernel(x_ref, g_ref, o_ref):
    x = x_ref[...]
    ms = jnp.mean(x * x, axis=-1, keepdims=True)
    o_ref[...] = (x * lax.rsqrt(ms + RMS_EPS) * g_ref[...]).astype(o_ref.dtype)


def rmsnorm(x2d, g, out_dtype):
    m, d = x2d.shape
    tm = _tile(m, 256)
    return pl.pallas_call(
        _rmsnorm_kernel,
        grid=(m // tm,),
        in_specs=[pl.BlockSpec((tm, d), lambda i: (i, 0)),
                  pl.BlockSpec((1, d), lambda i: (0, 0))],
        out_specs=pl.BlockSpec((tm, d), lambda i: (i, 0)),
        out_shape=jax.ShapeDtypeStruct((m, d), out_dtype),
        compiler_params=_cparams("parallel"),
        name="rmsnorm",
    )(x2d, g.reshape(1, d))


def _mm_kernel(a_ref, b_ref, o_ref):
    acc = jnp.dot(a_ref[...], b_ref[...].astype(BF16), preferred_element_type=F32)
    o_ref[...] = acc.astype(o_ref.dtype)


def _mm_res_kernel(a_ref, b_ref, r_ref, o_ref):
    acc = jnp.dot(a_ref[...], b_ref[...].astype(BF16), preferred_element_type=F32)
    o_ref[...] = (r_ref[...] + acc).astype(o_ref.dtype)


def matmul(a, b, out_dtype, tm, tn, residual=None, name="matmul"):
    m, k = a.shape
    _, n = b.shape
    tm, tn = _tile(m, tm), _tile(n, tn)
    in_specs = [pl.BlockSpec((tm, k), lambda j, i: (i, 0)),
                pl.BlockSpec((k, tn), lambda j, i: (0, j))]
    args = [a, b]
    kern = _mm_kernel
    if residual is not None:
        in_specs.append(pl.BlockSpec((tm, tn), lambda j, i: (i, j)))
        args.append(residual)
        kern = _mm_res_kernel
    return pl.pallas_call(
        kern,
        grid=(n // tn, m // tm),
        in_specs=in_specs,
        out_specs=pl.BlockSpec((tm, tn), lambda j, i: (i, j)),
        out_shape=jax.ShapeDtypeStruct((m, n), out_dtype),
        compiler_params=_cparams("parallel", "parallel"),
        name=name,
    )(*args)


def _merge_kernel(ya_ref, yb_ref, yc_ref, yx_ref, wa_ref, wb_ref, wc_ref, wx_ref,
                  g0_ref, g1_ref, g2_ref, g3_ref, o_ref):
    def term(y_ref, w_ref, g_ref):
        return jax.nn.sigmoid(g_ref[...]) * jnp.dot(y_ref[...], w_ref[...], preferred_element_type=F32)

    acc = term(ya_ref, wa_ref, g0_ref)
    acc = acc + term(yb_ref, wb_ref, g1_ref)
    acc = acc + term(yc_ref, wc_ref, g2_ref)
    acc = acc + term(yx_ref, wx_ref, g3_ref)
    o_ref[...] = acc.astype(o_ref.dtype)


def merge(ys, ws, proj, gate_off, d):
    m = proj.shape[0]
    tm, tn = _tile(m, 512), _tile(d, 512)
    assert gate_off % tn == 0
    y_specs = [pl.BlockSpec((tm, y.shape[1]), lambda j, i: (i, 0)) for y in ys]
    w_specs = [pl.BlockSpec((w.shape[0], tn), lambda j, i: (0, j)) for w in ws]
    g_specs = [pl.BlockSpec((tm, tn), functools.partial(
        lambda j, i, base: (i, base + j), base=(gate_off + k * d) // tn)) for k in range(4)]
    return pl.pallas_call(
        _merge_kernel,
        grid=(d // tn, m // tm),
        in_specs=y_specs + w_specs + g_specs,
        out_specs=pl.BlockSpec((tm, tn), lambda j, i: (i, j)),
        out_shape=jax.ShapeDtypeStruct((m, d), BF16),
        compiler_params=_cparams("parallel", "parallel"),
        name="merge",
    )(*ys, *ws, proj, proj, proj, proj)


def _shift_rows(x, shift, hist):
    rolled = pltpu.roll(x, shift, axis=0)
    head = pltpu.roll(hist, shift, axis=0)
    row = lax.broadcasted_iota(jnp.int32, (SUBLANES, x.shape[1]), 0)
    fixed = jnp.where(row < shift, head, rolled[:SUBLANES])
    return jnp.concatenate([fixed, rolled[SUBLANES:]], axis=0)


def _causal_conv(x, hist, w, b):
    kw = w.shape[0]
    out = x * w[kw - 1:kw] + b
    for j in range(kw - 1):
        out = out + _shift_rows(x, kw - 1 - j, hist) * w[j:j + 1]
    return out


def _lru_kernel(ax_ref, ag_ref, cw_ref, cb_ref, wa_ref, ba_ref, wx_ref, bx_ref, lam_ref, o_ref,
                hist_sc, hcar_sc, a_sc, b_sc, h_sc):
    t = pl.program_id(2)

    @pl.when(t == 0)
    def _():
        hist_sc[...] = jnp.zeros_like(hist_sc)
        hcar_sc[...] = jnp.zeros_like(hcar_sc)

    x = ax_ref[...]
    rows = x.shape[0]
    u = _causal_conv(x, hist_sc[...], cw_ref[...], cb_ref[...])
    hist_sc[...] = x[rows - SUBLANES:]
    ub = u.astype(BF16)
    r = jax.nn.sigmoid(jnp.dot(ub, wa_ref[...], preferred_element_type=F32) + ba_ref[...])
    i = jax.nn.sigmoid(jnp.dot(ub, wx_ref[...], preferred_element_type=F32) + bx_ref[...])
    log_a = -LRU_C * r * jax.nn.softplus(-lam_ref[...])
    a = jnp.exp(log_a)
    b = jnp.sqrt(-jnp.tanh(log_a) * (a * a + 1.0)) * (i * u)

    in_group = lax.broadcasted_iota(jnp.int32, x.shape, 0) % SUBLANES
    for s in (1, 2, 4):
        keep = in_group >= s
        b = jnp.where(keep, a * pltpu.roll(b, s, axis=0) + b, b)
        a = jnp.where(keep, a * pltpu.roll(a, s, axis=0), a)
    a_sc[...] = a
    b_sc[...] = b

    def group(g, h_prev):
        sl = pl.ds(pl.multiple_of(g * SUBLANES, SUBLANES), SUBLANES)
        h8 = a_sc[sl, :] * h_prev + b_sc[sl, :]
        h_sc[sl, :] = h8
        return h8[SUBLANES - 1:SUBLANES]

    hcar_sc[...] = lax.fori_loop(0, rows // SUBLANES, group, hcar_sc[...], unroll=4)
    o_ref[...] = (h_sc[...] * _silu(ag_ref[...])).astype(o_ref.dtype)


def lru_branch(proj, off_x, off_g, bsz, s, cw, cb, wa, ba, wx, bx, lam):
    nb, bs, _ = wa.shape
    w = nb * bs
    tt = _tile(s, 512)
    nt = s // tt
    assert off_x % bs == 0 and off_g % bs == 0
    vec = lambda v: v.reshape(1, w)
    vspec = pl.BlockSpec((1, bs), lambda b, n, t: (0, n))
    return pl.pallas_call(
        _lru_kernel,
        grid=(bsz, nb, nt),
        in_specs=[
            pl.BlockSpec((tt, bs), lambda b, n, t: (b * nt + t, off_x // bs + n)),
            pl.BlockSpec((tt, bs), lambda b, n, t: (b * nt + t, off_g // bs + n)),
            pl.BlockSpec((cw.shape[0], bs), lambda b, n, t: (0, n)),
            vspec,
            pl.BlockSpec((None, bs, bs), lambda b, n, t: (n, 0, 0)),
            vspec,
            pl.BlockSpec((None, bs, bs), lambda b, n, t: (n, 0, 0)),
            vspec,
            vspec,
        ],
        out_specs=pl.BlockSpec((tt, bs), lambda b, n, t: (b * nt + t, n)),
        out_shape=jax.ShapeDtypeStruct((bsz * s, w), BF16),
        scratch_shapes=[pltpu.VMEM((SUBLANES, bs), F32), pltpu.VMEM((1, bs), F32),
                        pltpu.VMEM((tt, bs), F32), pltpu.VMEM((tt, bs), F32), pltpu.VMEM((tt, bs), F32)],
        compiler_params=_cparams("parallel", "parallel", "arbitrary"),
        name="rg_lru",
    )(proj, proj, cw, vec(cb), wa.astype(BF16), vec(ba), wx.astype(BF16), vec(bx), vec(lam))


def _seg_sum(x, g):
    parts = [jnp.dot(x[:, c * LANES:(c + 1) * LANES], g, precision=HIGHEST, preferred_element_type=F32)
             for c in range(x.shape[1] // LANES)]
    return parts[0] if len(parts) == 1 else jnp.concatenate(parts, axis=1)


def _rwkv_prep_kernel(pr_ref, pk_ref, pv_ref, plo_ref, mur_ref, muk_ref, muv_ref, mulo_ref,
                      w0_ref, wup_ref, a0_ref, aup_ref, kk_ref, ka_ref, rk_ref, g_ref,
                      r_o, w_o, k_o, v_o, kk_o, kka_o, bonus_o,
                      lr_sc, lk_sc, lv_sc, llo_sc):
    t = pl.program_id(1)

    @pl.when(t == 0)
    def _():
        for sc in (lr_sc, lk_sc, lv_sc, llo_sc):
            sc[...] = jnp.zeros_like(sc)

    def shifted(p_ref, mu_ref, last_sc):
        p = p_ref[...]
        rows = p.shape[0]
        prev = _shift_rows(p, 1, last_sc[...])
        last_sc[...] = p[rows - SUBLANES:]
        return p + (prev - p) * mu_ref[...]

    r = shifted(pr_ref, mur_ref, lr_sc)
    k = shifted(pk_ref, muk_ref, lk_sc)
    v = shifted(pv_ref, muv_ref, lv_sc)
    lo = shifted(plo_ref, mulo_ref, llo_sc)

    w_pre = w0_ref[...] + jnp.dot(jnp.tanh(lo).astype(BF16), wup_ref[...], preferred_element_type=F32)
    w_log = -jax.nn.softplus(-w_pre) - 0.5
    decay = jnp.exp(-jnp.exp(w_log))
    a = jax.nn.sigmoid(a0_ref[...] + jnp.dot(lo.astype(BF16), aup_ref[...], preferred_element_type=F32))

    g = g_ref[...]
    kk = k * kk_ref[...]
    kk = kk / jnp.maximum(jnp.sqrt(_seg_sum(kk * kk, g)), 1e-12)
    k2 = k * (1.0 + (a - 1.0) * ka_ref[...])
    r_o[...] = r
    w_o[...] = decay
    k_o[...] = k2
    v_o[...] = v
    kk_o[...] = kk
    kka_o[...] = kk * a
    bonus_o[...] = _seg_sum(r * k2 * rk_ref[...], g) * v


def rwkv_prep(proj, offs, bsz, s, mu, w0, w_up, a0, a_up, k_k, k_a, r_k, n_head):
    off_r, off_k, off_v, off_lo, lo_w = offs
    r_w, w = w_up.shape
    r_a = a_up.shape[0]
    tt = _tile(s, 256)
    nt = s // tt
    assert off_r % w == 0 and off_k % w == 0 and off_v % w == 0 and off_lo % lo_w == 0
    assert LANES % n_head == 0 and w % LANES == 0
    vec = lambda x: x.reshape(1, -1)
    mu_lo = jnp.pad(mu[3 * w:], (0, lo_w - r_w - r_a))
    wup_pad = jnp.zeros((lo_w, w), F32).at[:r_w].set(w_up).astype(BF16)
    aup_pad = jnp.zeros((lo_w, w), F32).at[r_w:r_w + r_a].set(a_up).astype(BF16)
    gmat = jnp.asarray(np.kron(np.eye(LANES // n_head), np.ones((n_head, n_head))), F32)
    row_w = pl.BlockSpec((1, w), lambda b, t: (0, 0))
    blk = lambda off, width: pl.BlockSpec((tt, width), lambda b, t: (b * nt + t, off // width))
    out_spec = pl.BlockSpec((tt, w), lambda b, t: (b * nt + t, 0))
    out_shape = jax.ShapeDtypeStruct((bsz * s, w), F32)
    return pl.pallas_call(
        _rwkv_prep_kernel,
        grid=(bsz, nt),
        in_specs=[blk(off_r, w), blk(off_k, w), blk(off_v, w), blk(off_lo, lo_w),
                  row_w, row_w, row_w, pl.BlockSpec((1, lo_w), lambda b, t: (0, 0)),
                  row_w, pl.BlockSpec((lo_w, w), lambda b, t: (0, 0)),
                  row_w, pl.BlockSpec((lo_w, w), lambda b, t: (0, 0)),
                  row_w, row_w, row_w,
                  pl.BlockSpec((LANES, LANES), lambda b, t: (0, 0))],
        out_specs=[out_spec] * 7,
        out_shape=[out_shape] * 7,
        scratch_shapes=[pltpu.VMEM((SUBLANES, w), F32)] * 3 + [pltpu.VMEM((SUBLANES, lo_w), F32)],
        compiler_params=_cparams("parallel", "arbitrary"),
        name="rwkv_prep",
    )(proj, proj, proj, proj, vec(mu[:w]), vec(mu[w:2 * w]), vec(mu[2 * w:3 * w]), vec(mu_lo),
      vec(w0), wup_pad, vec(a0), aup_pad, vec(k_k), vec(k_a), vec(r_k), gmat)


def _rwkv_scan_kernel(kk_ref, w_ref, kka_ref, k_ref, r_ref, v_ref, y_ref, s_sc):
    c = pl.program_id(0)
    steps = kk_ref.shape[1]
    n_k, n_rows, _ = s_sc.shape
    blocks = [pl.ds(b * SUBLANES, SUBLANES) for b in range(n_rows // SUBLANES)]

    @pl.when(c == 0)
    def _():
        s_sc[...] = jnp.zeros_like(s_sc)

    def tree_sum(parts):
        while len(parts) > 1:
            parts = [parts[i] + parts[i + 1] for i in range(0, len(parts), 2)]
        return parts[0]

    sa0 = []
    for blk in blocks:
        acc = [None, None]
        for j in range(n_k):
            term = s_sc[j, blk, :] * kk_ref[j, 0:1, :]
            acc[j % 2] = term if acc[j % 2] is None else acc[j % 2] + term
        sa0.append(acc[0] + acc[1])

    def step(t, sa, with_next):
        v = [v_ref[t, blk, :] for blk in blocks]
        yacc = [[None, None] for _ in blocks]
        zacc = [[None, None] for _ in blocks]
        row = pl.ds(t, 1)
        for j in range(n_k):
            w_j, kka_j, k_j, r_j = w_ref[j, row, :], kka_ref[j, row, :], k_ref[j, row, :], r_ref[j, row, :]
            kk_j = kk_ref[j, pl.ds(t + 1, 1), :] if with_next else None
            for b, blk in enumerate(blocks):
                st = s_sc[j, blk, :] * w_j - sa[b] * kka_j + v[b] * k_j
                s_sc[j, blk, :] = st
                ya = st * r_j
                yacc[b][j % 2] = ya if yacc[b][j % 2] is None else yacc[b][j % 2] + ya
                if with_next:
                    za = st * kk_j
                    zacc[b][j % 2] = za if zacc[b][j % 2] is None else zacc[b][j % 2] + za
        for b, blk in enumerate(blocks):
            y_ref[t, blk, :] = tree_sum(yacc[b])
        return tuple(tree_sum(z) for z in zacc) if with_next else None

    sa = lax.fori_loop(0, steps - 1, lambda t, sa: step(t, sa, True), tuple(sa0))
    step(steps - 1, sa, False)


def rwkv_scan(kk_e, w_e, kka_e, k_e, r_e, v_e):
    n_k, s, lanes = kk_e.shape
    n_rows = v_e.shape[1]
    assert n_rows % SUBLANES == 0
    tc = _tile(s, 64)
    vec_spec = pl.BlockSpec((n_k, tc, lanes), lambda c: (0, c, 0))
    row_spec = pl.BlockSpec((tc, n_rows, lanes), lambda c: (c, 0, 0))
    return pl.pallas_call(
        _rwkv_scan_kernel,
        grid=(s // tc,),
        in_specs=[vec_spec] * 5 + [row_spec],
        out_specs=row_spec,
        out_shape=jax.ShapeDtypeStruct((s, n_rows, lanes), F32),
        scratch_shapes=[pltpu.VMEM((n_k, n_rows, lanes), F32)],
        compiler_params=_cparams("arbitrary"),
        name="rwkv_scan",
    )(kk_e, w_e, kka_e, k_e, r_e, v_e)


def _rwkv_out_kernel(y_ref, bonus_ref, gate_ref, gw_ref, gb_ref, g_ref, o_ref, *, n_head):
    y = y_ref[...]
    g = g_ref[...] * (1.0 / n_head)
    mu = _seg_sum(y, g)
    yc = y - mu
    var = _seg_sum(yc * yc, g)
    yn = yc * lax.rsqrt(var + RWKV_GN_EPS) * gw_ref[...] + gb_ref[...]
    o_ref[...] = ((yn + bonus_ref[...]) * _silu(gate_ref[...])).astype(o_ref.dtype)


def rwkv_out(y, bonus, proj, off_g, gn_w, gn_b, n_head):
    m, w = y.shape
    tt = _tile(m, 256)
    assert off_g % w == 0
    gmat = jnp.asarray(np.kron(np.eye(LANES // n_head), np.ones((n_head, n_head))), F32)
    blk = pl.BlockSpec((tt, w), lambda i: (i, 0))
    row = pl.BlockSpec((1, w), lambda i: (0, 0))
    return pl.pallas_call(
        functools.partial(_rwkv_out_kernel, n_head=n_head),
        grid=(m // tt,),
        in_specs=[blk, blk, pl.BlockSpec((tt, w), lambda i: (i, off_g // w)), row, row,
                  pl.BlockSpec((LANES, LANES), lambda i: (0, 0))],
        out_specs=blk,
        out_shape=jax.ShapeDtypeStruct((m, w), BF16),
        compiler_params=_cparams("parallel"),
        name="rwkv_out",
    )(y, bonus, proj, gn_w.reshape(1, w), gn_b.reshape(1, w), gmat)


def _to_chain_kernel(x_ref, o_ref, *, n_heads, n_head, dup, value_rows):
    bsz, t, _ = x_ref.shape
    parts = []
    for b in range(bsz):
        x3 = x_ref[b].T.reshape(n_heads, n_head, t)
        parts.append(jnp.transpose(x3, (1, 0, 2)))
    if value_rows:
        nr = n_head // dup
        e = jnp.concatenate([p[d * nr:(d + 1) * nr] for d in range(dup) for p in parts], axis=1)
        o_ref[...] = jnp.transpose(jnp.swapaxes(e, 1, 2), (1, 0, 2))
    else:
        e = jnp.concatenate(parts * dup, axis=1)
        o_ref[...] = jnp.swapaxes(e, 1, 2)


def to_chain_layout(x, bsz, s, n_heads, n_head, value_rows=False):
    w = n_heads * n_head
    assert LANES % (bsz * n_heads) == 0
    dup = LANES // (bsz * n_heads)
    assert n_head % dup == 0
    tt = _tile(s, LANES)
    if value_rows:
        out_shape, out_block, out_idx = (s, n_head // dup, LANES), (tt, n_head // dup, LANES), lambda i: (i, 0, 0)
    else:
        out_shape, out_block, out_idx = (n_head, s, LANES), (n_head, tt, LANES), lambda i: (0, i, 0)
    return pl.pallas_call(
        functools.partial(_to_chain_kernel, n_heads=n_heads, n_head=n_head, dup=dup, value_rows=value_rows),
        grid=(s // tt,),
        in_specs=[pl.BlockSpec((bsz, tt, w), lambda i: (0, i, 0))],
        out_specs=pl.BlockSpec(out_block, out_idx),
        out_shape=jax.ShapeDtypeStruct(out_shape, F32),
        compiler_params=_cparams("parallel"),
        name="to_chain_rows" if value_rows else "to_chain_keys",
    )(x.reshape(bsz, s, w))


def _from_chain_kernel(y_ref, o_ref, *, n_heads, n_head, dup):
    bsz = o_ref.shape[0]
    t = y_ref.shape[0]
    e = jnp.swapaxes(jnp.transpose(y_ref[...], (1, 0, 2)), 1, 2)
    bh = bsz * n_heads
    for b in range(bsz):
        rows = jnp.concatenate(
            [e[:, d * bh + b * n_heads:d * bh + (b + 1) * n_heads, :] for d in range(dup)], axis=0)
        o_ref[b] = jnp.transpose(rows, (1, 0, 2)).reshape(n_heads * n_head, t).T


def from_chain_layout(y_e, bsz, n_heads, n_head):
    s, n_rows, _ = y_e.shape
    w = n_heads * n_head
    dup = LANES // (bsz * n_heads)
    tt = _tile(s, LANES)
    return pl.pallas_call(
        functools.partial(_from_chain_kernel, n_heads=n_heads, n_head=n_head, dup=dup),
        grid=(s // tt,),
        in_specs=[pl.BlockSpec((tt, n_rows, LANES), lambda i: (i, 0, 0))],
        out_specs=pl.BlockSpec((bsz, tt, w), lambda i: (0, i, 0)),
        out_shape=jax.ShapeDtypeStruct((bsz, s, w), F32),
        compiler_params=_cparams("parallel"),
        name="from_chain_rows",
    )(y_e).reshape(bsz * s, w)


def _mlstm_kernel(qk_ref, v_ref, o_ref, g_ref, if_ref, cw_ref, cb_ref, bias_ref, gw_ref, tril_ref,
                  out_ref, hist_sc, c_sc, m_sc, *, n_heads, dqk, dv):
    t = pl.program_id(1)

    @pl.when(t == 0)
    def _():
        hist_sc[...] = jnp.zeros_like(hist_sc)
        c_sc[...] = jnp.zeros_like(c_sc)
        m_sc[...] = jnp.full_like(m_sc, -jnp.inf)

    qk_in = qk_ref[...]
    rows = qk_in.shape[0]
    qk = _silu(_causal_conv(qk_in, hist_sc[...], cw_ref[...], cb_ref[...]))
    hist_sc[...] = qk_in[rows - SUBLANES:]

    pre = if_ref[...] + bias_ref[...]
    li = pre
    lf = jax.nn.log_sigmoid(pltpu.roll(pre, LANES - n_heads, axis=1))
    tril = tril_ref[...]
    bcum = jnp.dot(tril, lf, precision=HIGHEST, preferred_element_type=F32)
    g_tot = bcum[rows - 1:rows]
    m_prev = m_sc[...]
    le = g_tot - bcum + li
    m_new = jnp.maximum(g_tot + m_prev, jnp.max(le, axis=0, keepdims=True))
    keep = jnp.exp(g_tot + m_prev - m_new)
    wk = jnp.exp(le - m_new)
    m_sc[...] = m_new
    bcum_t = bcum.T
    li_t = li.T
    causal = tril > 0.5
    ones_blk = (lax.broadcasted_iota(jnp.int32, (rows, LANES), 1) == 0).astype(F32)

    for h in range(n_heads):
        q = qk[:, h * dqk:(h + 1) * dqk]
        k = qk[:, (n_heads + h) * dqk:(n_heads + h + 1) * dqk] * (dqk ** -0.5)
        v = v_ref[:, h * dv:(h + 1) * dv]
        v_aug = jnp.concatenate([v, ones_blk], axis=1).astype(BF16)
        b_col = bcum[:, h:h + 1]
        dmat = b_col - bcum_t[h:h + 1, :] + li_t[h:h + 1, :]
        dmat = jnp.where(causal, dmat, -jnp.inf)
        inter = b_col + m_prev[:, h:h + 1]
        m_t = jnp.maximum(inter, jnp.max(dmat, axis=-1, keepdims=True))
        w_intra = jnp.exp(dmat - m_t)
        w_inter = jnp.exp(inter - m_t)
        qb = q.astype(BF16)
        sc = lax.dot_general(qb, k.astype(BF16), (((1,), (1,)), ((), ())),
                             preferred_element_type=F32) * w_intra
        c_aug = c_sc[h]
        tot = (w_inter * jnp.dot(qb, c_aug.astype(BF16), preferred_element_type=F32)
               + jnp.dot(sc.astype(BF16), v_aug, preferred_element_type=F32))
        num = tot[:, :dv]
        den = tot[:, dv:dv + 1]
        hh = num / jnp.maximum(jnp.abs(den), jnp.exp(-m_t))
        kw = (k * wk[:, h:h + 1]).astype(BF16)
        c_sc[h] = keep[:, h:h + 1] * c_aug + lax.dot_general(
            kw, v_aug, (((0,), (0,)), ((), ())), preferred_element_type=F32)

        ho = hh * jax.nn.sigmoid(o_ref[:, h * dv:(h + 1) * dv])
        mu = jnp.mean(ho, axis=-1, keepdims=True)
        hc = ho - mu
        var = jnp.mean(hc * hc, axis=-1, keepdims=True)
        hn = hc * lax.rsqrt(var + MLSTM_GN_EPS) * gw_ref[:, h * dv:(h + 1) * dv]
        out_ref[:, h * dv:(h + 1) * dv] = (hn * _silu(g_ref[:, h * dv:(h + 1) * dv])).astype(out_ref.dtype)


def mlstm_branch(proj, offs, bsz, s, conv_w, conv_b, b_i, b_f, gn_w, w_c):
    off_qk, off_v, off_o, off_g, off_if, if_w = offs
    n_heads = b_i.shape[0]
    qk2 = conv_w.shape[1]
    dqk = qk2 // 2 // n_heads
    dv = w_c // n_heads
    ll = _tile(s, 256)
    nt = s // ll
    assert off_qk % qk2 == 0 and off_v % w_c == 0 and off_o % w_c == 0 and off_g % w_c == 0
    assert off_if % if_w == 0 and if_w == LANES and 2 * n_heads <= LANES
    bias = jnp.pad(jnp.concatenate([b_i, b_f]), (0, LANES - 2 * n_heads)).reshape(1, LANES)
    tril = jnp.asarray(np.tril(np.ones((ll, ll))), F32)
    blk = lambda off, width: pl.BlockSpec((ll, width), lambda b, t: (b * nt + t, off // width))
    full = lambda shape: pl.BlockSpec(shape, lambda b, t: (0,) * len(shape))
    return pl.pallas_call(
        functools.partial(_mlstm_kernel, n_heads=n_heads, dqk=dqk, dv=dv),
        grid=(bsz, nt),
        in_specs=[blk(off_qk, qk2), blk(off_v, w_c), blk(off_o, w_c), blk(off_g, w_c), blk(off_if, if_w),
                  full(conv_w.shape), full((1, qk2)), full((1, LANES)), full((1, w_c)), full((ll, ll))],
        out_specs=pl.BlockSpec((ll, w_c), lambda b, t: (b * nt + t, 0)),
        out_shape=jax.ShapeDtypeStruct((bsz * s, w_c), BF16),
        scratch_shapes=[pltpu.VMEM((SUBLANES, qk2), F32),
                        pltpu.VMEM((n_heads, dqk, dv + LANES), F32),
                        pltpu.VMEM((1, LANES), F32)],
        compiler_params=_cparams("parallel", "arbitrary"),
        name="mlstm",
    )(proj, proj, proj, proj, proj, conv_w, conv_b.reshape(1, qk2), bias, gn_w.reshape(1, w_c), tril)


def _xattn_kernel(q_ref, gate_ref, kv_ref, o_ref, *, n_heads):
    w_x = q_ref.shape[1]
    for h in range(n_heads):
        sl = slice(h * DH_X, (h + 1) * DH_X)
        q = q_ref[:, sl].astype(BF16)
        k = kv_ref[:, sl].astype(BF16)
        v = kv_ref[:, w_x + h * DH_X:w_x + (h + 1) * DH_X].astype(BF16)
        logits = lax.dot_general(q, k, (((1,), (1,)), ((), ())), preferred_element_type=F32) * (DH_X ** -0.5)
        e = jnp.exp(logits - jnp.max(logits, axis=-1, keepdims=True))
        probs = e / jnp.sum(e, axis=-1, keepdims=True)
        o = jnp.dot(probs.astype(BF16), v, preferred_element_type=F32)
        o_ref[:, sl] = (o * _silu(gate_ref[:, sl])).astype(o_ref.dtype)


def xattn_branch(proj, off_q, off_g, bsz, s, kv, w_x):
    m_len = kv.shape[0] // bsz
    tt = _tile(s, 512)
    nt = s // tt
    assert off_q % w_x == 0 and off_g % w_x == 0 and w_x % DH_X == 0
    return pl.pallas_call(
        functools.partial(_xattn_kernel, n_heads=w_x // DH_X),
        grid=(bsz, nt),
        in_specs=[pl.BlockSpec((tt, w_x), lambda b, t: (b * nt + t, off_q // w_x)),
                  pl.BlockSpec((tt, w_x), lambda b, t: (b * nt + t, off_g // w_x)),
                  pl.BlockSpec((m_len, 2 * w_x), lambda b, t: (b, 0))],
        out_specs=pl.BlockSpec((tt, w_x), lambda b, t: (b * nt + t, 0)),
        out_shape=jax.ShapeDtypeStruct((bsz * s, w_x), BF16),
        compiler_params=_cparams("parallel", "parallel"),
        name="mem_xattn",
    )(proj, proj, kv)


def _round_up(n, m):
    return (n + m - 1) // m * m


def _proj_layout(sizes, w_b, tn):
    src = np.concatenate([[0], np.cumsum(sizes)])
    seg = {name: (int(src[i]), int(sizes[i])) for i, name in enumerate(
        ("a_x", "a_g", "b_s", "b_g", "c_qk", "c_v", "c_o", "c_g", "c_if", "x_q", "x_g", "gates"))}
    b0 = seg["b_s"][0]
    seg["b_r"], seg["b_k"], seg["b_v"] = (b0, w_b), (b0 + w_b, w_b), (b0 + 2 * w_b, w_b)
    seg["b_lo"] = (b0 + 3 * w_b, seg["b_s"][1] - 3 * w_b)
    order = ("a_x", "a_g", "b_g", "c_qk", "c_v", "c_o", "c_g", "b_r", "b_k", "b_v", "gates",
             "x_q", "x_g", "b_lo", "c_if")
    offs, block_src = {}, []
    for name in order:
        start, width = seg[name]
        assert start % SUBLANES == 0 and (width % tn == 0 or width < tn), (name, start, width)
        offs[name] = len(block_src) * tn
        n_blocks = max(1, width // tn)
        assert start + n_blocks * tn <= src[-1]
        block_src.extend(start + b * tn for b in range(n_blocks))
    return offs, block_src


def _in_proj_kernel(src_ref, h_ref, w_ref, o_ref, w_sc):
    del src_ref

    @pl.when(pl.program_id(1) == 0)
    def _():
        w_sc[...] = w_ref[...].T.astype(BF16)

    o_ref[...] = jnp.dot(h_ref[...], w_sc[...], preferred_element_type=F32)


def in_proj(h, w_in_stacked, layer, block_src, tn):
    m, k = h.shape
    n_layers, _, c_in = w_in_stacked.shape
    assert c_in % SUBLANES == 0
    w_t = jnp.swapaxes(w_in_stacked, 1, 2).reshape(n_layers * c_in, k)
    rows8 = jnp.asarray([(layer * c_in + c) // SUBLANES for c in block_src], jnp.int32)
    tm = _tile(m, 1024)
    n_blocks = len(block_src)
    grid_spec = pltpu.PrefetchScalarGridSpec(
        num_scalar_prefetch=1,
        grid=(n_blocks, m // tm),
        in_specs=[pl.BlockSpec((tm, k), lambda j, i, src: (i, 0)),
                  pl.BlockSpec((pl.Element(tn), pl.Element(k)), lambda j, i, src: (src[j] * SUBLANES, 0))],
        out_specs=pl.BlockSpec((tm, tn), lambda j, i, src: (i, j)),
        scratch_shapes=[pltpu.VMEM((k, tn), BF16)])
    return pl.pallas_call(
        _in_proj_kernel,
        grid_spec=grid_spec,
        out_shape=jax.ShapeDtypeStruct((m, n_blocks * tn), F32),
        compiler_params=_cparams("parallel", "arbitrary"),
        name="in_proj",
    )(rows8, h, w_t)


def _layer(x2d, memn_src, bsz, s, p, w_in_stacked, layer):
    d = x2d.shape[1]
    w_a = p["lru_lambda"].shape[0]
    n_heads_b, n_head_b = p["rwkv_r_k"].shape
    w_b = n_heads_b * n_head_b
    w_c = p["w_branch_c"].shape[0]
    w_x = p["w_branch_x"].shape[0]
    qk2 = p["mlstm_conv_w"].shape[1]
    n_heads_c = p["mlstm_b_i"].shape[0]
    b_shift_w = p["rwkv_mu"].shape[0]
    sizes = (w_a, w_a, b_shift_w, w_b, qk2, w_c, w_c, w_c, 2 * n_heads_c, w_x, w_x, 4 * d)
    lo_w = _round_up(b_shift_w - 3 * w_b, 256)
    tn = min(512, w_a, w_b, w_c, w_x, qk2)
    assert tn % lo_w == 0
    offs, block_src = _proj_layout(sizes, w_b, tn)

    h = rmsnorm(x2d, p["norm_g"], BF16)
    proj = in_proj(h, w_in_stacked, layer, block_src, tn)

    y_a = lru_branch(proj, offs["a_x"], offs["a_g"], bsz, s, p["lru_conv_w"], p["lru_conv_b"],
                     p["lru_wa"], p["lru_ba"], p["lru_wx"], p["lru_bx"], p["lru_lambda"])

    r, w, k, v, kk, kka, bonus = rwkv_prep(
        proj, (offs["b_r"], offs["b_k"], offs["b_v"], offs["b_lo"], lo_w), bsz, s, p["rwkv_mu"],
        p["rwkv_w0"], p["rwkv_w_up"], p["rwkv_a0"], p["rwkv_a_up"], p["rwkv_k_k"], p["rwkv_k_a"],
        p["rwkv_r_k"], n_head_b)
    to_e = functools.partial(to_chain_layout, bsz=bsz, s=s, n_heads=n_heads_b, n_head=n_head_b)
    y_e = rwkv_scan(to_e(kk), to_e(w), to_e(kka), to_e(k), to_e(r), to_e(v, value_rows=True))
    y_b = rwkv_out(from_chain_layout(y_e, bsz, n_heads_b, n_head_b), bonus, proj, offs["b_g"],
                   p["rwkv_gn_w"], p["rwkv_gn_b"], n_head_b)

    y_c = mlstm_branch(proj, (offs["c_qk"], offs["c_v"], offs["c_o"], offs["c_g"], offs["c_if"], LANES),
                       bsz, s, p["mlstm_conv_w"], p["mlstm_conv_b"], p["mlstm_b_i"], p["mlstm_b_f"],
                       p["mlstm_gn_w"], w_c)

    memn = rmsnorm(memn_src, p["mem_norm_g"], BF16)
    kv = matmul(memn, p["xattn_w_kv"], F32, 512, 512, name="mem_kv")
    y_x = xattn_branch(proj, offs["x_q"], offs["x_g"], bsz, s, kv, w_x)

    merged = merge((y_a, y_b, y_c, y_x),
                   tuple(p[n].astype(BF16) for n in ("w_branch_a", "w_branch_b", "w_branch_c", "w_branch_x")),
                   proj, offs["gates"], d)
    return matmul(merged, p["w_out"], F32, 1024, 512, residual=x2d, name="out_proj")


_LAYER_PARAMS = (
    "norm_g", "mem_norm_g", "w_in", "lru_conv_w", "lru_conv_b", "lru_wa", "lru_ba", "lru_wx", "lru_bx",
    "lru_lambda", "rwkv_mu", "rwkv_w0", "rwkv_w_up", "rwkv_a0", "rwkv_a_up", "rwkv_k_k", "rwkv_k_a",
    "rwkv_r_k", "rwkv_gn_w", "rwkv_gn_b", "mlstm_conv_w", "mlstm_conv_b", "mlstm_b_i", "mlstm_b_f",
    "mlstm_gn_w", "xattn_w_kv", "w_branch_a", "w_branch_b", "w_branch_c", "w_branch_x", "w_out")


def kernel(x, mem, norm_g, mem_norm_g, w_in, lru_conv_w, lru_conv_b, lru_wa, lru_ba, lru_wx, lru_bx, lru_lambda, rwkv_mu, rwkv_w0, rwkv_w_up, rwkv_a0, rwkv_a_up, rwkv_k_k, rwkv_k_a, rwkv_r_k, rwkv_gn_w, rwkv_gn_b, mlstm_conv_w, mlstm_conv_b, mlstm_b_i, mlstm_b_f, mlstm_gn_w, xattn_w_kv, w_branch_a, w_branch_b, w_branch_c, w_branch_x, w_out, final_norm_g):
    stacked = dict(zip(_LAYER_PARAMS, (
        norm_g, mem_norm_g, w_in, lru_conv_w, lru_conv_b, lru_wa, lru_ba, lru_wx, lru_bx, lru_lambda,
        rwkv_mu, rwkv_w0, rwkv_w_up, rwkv_a0, rwkv_a_up, rwkv_k_k, rwkv_k_a, rwkv_r_k, rwkv_gn_w,
        rwkv_gn_b, mlstm_conv_w, mlstm_conv_b, mlstm_b_i, mlstm_b_f, mlstm_gn_w, xattn_w_kv,
        w_branch_a, w_branch_b, w_branch_c, w_branch_x, w_out)))
    bsz, s, d = x.shape
    x2d = x.reshape(bsz * s, d)
    mem2d = mem.reshape(-1, d)
    for layer in range(norm_g.shape[0]):
        params = {name: val[layer] for name, val in stacked.items() if name != "w_in"}
        x2d = _layer(x2d, mem2d, bsz, s, params, w_in, layer)
    return rmsnorm(x2d, final_norm_g, x.dtype).reshape(bsz, s, d)
```

```python
import functools

import jax
import jax.numpy as jnp
import numpy as np
from jax import lax
from jax.experimental import pallas as pl
from jax.experimental.pallas import tpu as pltpu

F32 = jnp.float32
BF16 = jnp.bfloat16
HIGHEST = lax.Precision.HIGHEST

LANES = 128
SUBLANES = 8
VMEM_LIMIT = 56 * 1024 * 1024

RMS_EPS = 1e-6
LRU_C = 8.0
RWKV_GN_EPS = 64e-5
MLSTM_GN_EPS = 1e-6
DH_X = 128


def _cparams(*sem):
    return pltpu.CompilerParams(dimension_semantics=sem, vmem_limit_bytes=VMEM_LIMIT)


def _tile(n, want):
    t = min(n, want)
    assert n % t == 0, (n, want)
    return t


def _silu(x):
    return x * jax.nn.sigmoid(x)


def _rmsnorm_kernel(x_ref, g_ref, o_ref):
    x = x_ref[...]
    ms = jnp.mean(x * x, axis=-1, keepdims=True)
    o_ref[...] = (x * lax.rsqrt(ms + RMS_EPS) * g_ref[...]).astype(o_ref.dtype)


def rmsnorm(x2d, g, out_dtype):
    m, d = x2d.shape
    tm = _tile(m, 256)
    return pl.pallas_call(
        _rmsnorm_kernel,
        grid=(m // tm,),
        in_specs=[pl.BlockSpec((tm, d), lambda i: (i, 0)),
                  pl.BlockSpec((1, d), lambda i: (0, 0))],
        out_specs=pl.BlockSpec((tm, d), lambda i: (i, 0)),
        out_shape=jax.ShapeDtypeStruct((m, d), out_dtype),
        compiler_params=_cparams("parallel"),
        name="rmsnorm",
    )(x2d, g.reshape(1, d))


def _mm_kernel(a_ref, b_ref, o_ref):
    acc = jnp.dot(a_ref[...], b_ref[...].astype(BF16), preferred_element_type=F32)
    o_ref[...] = acc.astype(o_ref.dtype)


def _mm_res_kernel(a_ref, b_ref, r_ref, o_ref):
    acc = jnp.dot(a_ref[...], b_ref[...].astype(BF16), preferred_element_type=F32)
    o_ref[...] = (r_ref[...] + acc).astype(o_ref.dtype)


def matmul(a, b, out_dtype, tm, tn, residual=None, name="matmul"):
    m, k = a.shape
    _, n = b.shape
    tm, tn = _tile(m, tm), _tile(n, tn)
    in_specs = [pl.BlockSpec((tm, k), lambda j, i: (i, 0)),
                pl.BlockSpec((k, tn), lambda j, i: (0, j))]
    args = [a, b]
    kern = _mm_kernel
    if residual is not None:
        in_specs.append(pl.BlockSpec((tm, tn), lambda j, i: (i, j)))
        args.append(residual)
        kern = _mm_res_kernel
    return pl.pallas_call(
        kern,
        grid=(n // tn, m // tm),
        in_specs=in_specs,
        out_specs=pl.BlockSpec((tm, tn), lambda j, i: (i, j)),
        out_shape=jax.ShapeDtypeStruct((m, n), out_dtype),
        compiler_params=_cparams("parallel", "parallel"),
        name=name,
    )(*args)


def _merge_kernel(ya_ref, yb_ref, yc_ref, yx_ref, wa_ref, wb_ref, wc_ref, wx_ref,
                  g0_ref, g1_ref, g2_ref, g3_ref, o_ref):
    def term(y_ref, w_ref, g_ref):
        return jax.nn.sigmoid(g_ref[...]) * jnp.dot(y_ref[...], w_ref[...], preferred_element_type=F32)

    acc = term(ya_ref, wa_ref, g0_ref)
    acc = acc + term(yb_ref, wb_ref, g1_ref)
    acc = acc + term(yc_ref, wc_ref, g2_ref)
    acc = acc + term(yx_ref, wx_ref, g3_ref)
    o_ref[...] = acc.astype(o_ref.dtype)


def merge(ys, ws, proj, gate_off, d):
    m = proj.shape[0]
    tm, tn = _tile(m, 512), _tile(d, 512)
    assert gate_off % tn == 0
    y_specs = [pl.BlockSpec((tm, y.shape[1]), lambda j, i: (i, 0)) for y in ys]
    w_specs = [pl.BlockSpec((w.shape[0], tn), lambda j, i: (0, j)) for w in ws]
    g_specs = [pl.BlockSpec((tm, tn), functools.partial(
        lambda j, i, base: (i, base + j), base=(gate_off + k * d) // tn)) for k in range(4)]
    return pl.pallas_call(
        _merge_kernel,
        grid=(d // tn, m // tm),
        in_specs=y_specs + w_specs + g_specs,
        out_specs=pl.BlockSpec((tm, tn), lambda j, i: (i, j)),
        out_shape=jax.ShapeDtypeStruct((m, d), BF16),
        compiler_params=_cparams("parallel", "parallel"),
        name="merge",
    )(*ys, *ws, proj, proj, proj, proj)


def _shift_rows(x, shift, hist):
    rolled = pltpu.roll(x, shift, axis=0)
    head = pltpu.roll(hist, shift, axis=0)
    row = lax.broadcasted_iota(jnp.int32, (SUBLANES, x.shape[1]), 0)
    fixed = jnp.where(row < shift, head, rolled[:SUBLANES])
    return jnp.concatenate([fixed, rolled[SUBLANES:]], axis=0)


def _causal_conv(x, hist, w, b):
    kw = w.shape[0]
    out = x * w[kw - 1:kw] + b
    for j in range(kw - 1):
        out = out + _shift_rows(x, kw - 1 - j, hist) * w[j:j + 1]
    return out


def _lru_kernel(ax_ref, ag_ref, cw_ref, cb_ref, wa_ref, ba_ref, wx_ref, bx_ref, lam_ref, o_ref,
                hist_sc, hcar_sc, a_sc, b_sc, h_sc):
    t = pl.program_id(2)

    @pl.when(t == 0)
    def _():
        hist_sc[...] = jnp.zeros_like(hist_sc)
        hcar_sc[...] = jnp.zeros_like(hcar_sc)

    x = ax_ref[...]
    rows = x.shape[0]
    u = _causal_conv(x, hist_sc[...], cw_ref[...], cb_ref[...])
    hist_sc[...] = x[rows - SUBLANES:]
    ub = u.astype(BF16)
    r = jax.nn.sigmoid(jnp.dot(ub, wa_ref[...], preferred_element_type=F32) + ba_ref[...])
    i = jax.nn.sigmoid(jnp.dot(ub, wx_ref[...], preferred_element_type=F32) + bx_ref[...])
    log_a = -LRU_C * r * jax.nn.softplus(-lam_ref[...])
    a = jnp.exp(log_a)
    b = jnp.sqrt(-jnp.tanh(log_a) * (a * a + 1.0)) * (i * u)

    in_group = lax.broadcasted_iota(jnp.int32, x.shape, 0) % SUBLANES
    for s in (1, 2, 4):
        keep = in_group >= s
        b = jnp.where(keep, a * pltpu.roll(b, s, axis=0) + b, b)
        a = jnp.where(keep, a * pltpu.roll(a, s, axis=0), a)
    a_sc[...] = a
    b_sc[...] = b

    def group(g, h_prev):
        sl = pl.ds(pl.multiple_of(g * SUBLANES, SUBLANES), SUBLANES)
        h8 = a_sc[sl, :] * h_prev + b_sc[sl, :]
        h_sc[sl, :] = h8
        return h8[SUBLANES - 1:SUBLANES]

    hcar_sc[...] = lax.fori_loop(0, rows // SUBLANES, group, hcar_sc[...], unroll=4)
    o_ref[...] = (h_sc[...] * _silu(ag_ref[...])).astype(o_ref.dtype)


def lru_branch(proj, off_x, off_g, bsz, s, cw, cb, wa, ba, wx, bx, lam):
    nb, bs, _ = wa.shape
    w = nb * bs
    tt = _tile(s, 512)
    nt = s // tt
    assert off_x % bs == 0 and off_g % bs == 0
    vec = lambda v: v.reshape(1, w)
    vspec = pl.BlockSpec((1, bs), lambda b, n, t: (0, n))
    return pl.pallas_call(
        _lru_kernel,
        grid=(bsz, nb, nt),
        in_specs=[
            pl.BlockSpec((tt, bs), lambda b, n, t: (b * nt + t, off_x // bs + n)),
            pl.BlockSpec((tt, bs), lambda b, n, t: (b * nt + t, off_g // bs + n)),
            pl.BlockSpec((cw.shape[0], bs), lambda b, n, t: (0, n)),
            vspec,
            pl.BlockSpec((None, bs, bs), lambda b, n, t: (n, 0, 0)),
            vspec,
            pl.BlockSpec((None, bs, bs), lambda b, n, t: (n, 0, 0)),
            vspec,
            vspec,
        ],
        out_specs=pl.BlockSpec((tt, bs), lambda b, n, t: (b * nt + t, n)),
        out_shape=jax.ShapeDtypeStruct((bsz * s, w), BF16),
        scratch_shapes=[pltpu.VMEM((SUBLANES, bs), F32), pltpu.VMEM((1, bs), F32),
                        pltpu.VMEM((tt, bs), F32), pltpu.VMEM((tt, bs), F32), pltpu.VMEM((tt, bs), F32)],
        compiler_params=_cparams("parallel", "parallel", "arbitrary"),
        name="rg_lru",
    )(proj, proj, cw, vec(cb), wa.astype(BF16), vec(ba), wx.astype(BF16), vec(bx), vec(lam))


def _seg_sum(x, g):
    parts = [jnp.dot(x[:, c * LANES:(c + 1) * LANES], g, precision=HIGHEST, preferred_element_type=F32)
             for c in range(x.shape[1] // LANES)]
    return parts[0] if len(parts) == 1 else jnp.concatenate(parts, axis=1)


def _rwkv_prep_kernel(pr_ref, pk_ref, pv_ref, plo_ref, mur_ref, muk_ref, muv_ref, mulo_ref,
                      w0_ref, wup_ref, a0_ref, aup_ref, kk_ref, ka_ref, rk_ref, g_ref,
                      r_o, w_o, k_o, v_o, kk_o, kka_o, bonus_o,
                      lr_sc, lk_sc, lv_sc, llo_sc):
    t = pl.program_id(1)

    @pl.when(t == 0)
    def _():
        for sc in (lr_sc, lk_sc, lv_sc, llo_sc):
            sc[...] = jnp.zeros_like(sc)

    def shifted(p_ref, mu_ref, last_sc):
        p = p_ref[...]
        rows = p.shape[0]
        prev = _shift_rows(p, 1, last_sc[...])
        last_sc[...] = p[rows - SUBLANES:]
        return p + (prev - p) * mu_ref[...]

    r = shifted(pr_ref, mur_ref, lr_sc)
    k = shifted(pk_ref, muk_ref, lk_sc)
    v = shifted(pv_ref, muv_ref, lv_sc)
    lo = shifted(plo_ref, mulo_ref, llo_sc)

    w_pre = w0_ref[...] + jnp.dot(jnp.tanh(lo).astype(BF16), wup_ref[...], preferred_element_type=F32)
    w_log = -jax.nn.softplus(-w_pre) - 0.5
    decay = jnp.exp(-jnp.exp(w_log))
    a = jax.nn.sigmoid(a0_ref[...] + jnp.dot(lo.astype(BF16), aup_ref[...], preferred_element_type=F32))

    g = g_ref[...]
    kk = k * kk_ref[...]
    kk = kk / jnp.maximum(jnp.sqrt(_seg_sum(kk * kk, g)), 1e-12)
    k2 = k * (1.0 + (a - 1.0) * ka_ref[...])
    r_o[...] = r
    w_o[...] = decay
    k_o[...] = k2
    v_o[...] = v
    kk_o[...] = kk
    kka_o[...] = kk * a
    bonus_o[...] = _seg_sum(r * k2 * rk_ref[...], g) * v


def rwkv_prep(proj, proj_lo, offs, bsz, s, mu, w0, w_up, a0, a_up, k_k, k_a, r_k, n_head):
    off_r, off_k, off_v, off_lo, lo_w = offs
    r_w, w = w_up.shape
    r_a = a_up.shape[0]
    tt = _tile(s, 256)
    nt = s // tt
    assert off_r % w == 0 and off_k % w == 0 and off_v % w == 0 and off_lo % lo_w == 0
    assert LANES % n_head == 0 and w % LANES == 0
    vec = lambda x: x.reshape(1, -1)
    mu_lo = jnp.pad(mu[3 * w:], (0, lo_w - r_w - r_a))
    wup_pad = jnp.zeros((lo_w, w), F32).at[:r_w].set(w_up).astype(BF16)
    aup_pad = jnp.zeros((lo_w, w), F32).at[r_w:r_w + r_a].set(a_up).astype(BF16)
    gmat = jnp.asarray(np.kron(np.eye(LANES // n_head), np.ones((n_head, n_head))), F32)
    row_w = pl.BlockSpec((1, w), lambda b, t: (0, 0))
    blk = lambda off, width: pl.BlockSpec((tt, width), lambda b, t: (b * nt + t, off // width))
    out_spec = pl.BlockSpec((tt, w), lambda b, t: (b * nt + t, 0))
    out_shape = jax.ShapeDtypeStruct((bsz * s, w), F32)
    return pl.pallas_call(
        _rwkv_prep_kernel,
        grid=(bsz, nt),
        in_specs=[blk(off_r, w), blk(off_k, w), blk(off_v, w), blk(off_lo, lo_w),
                  row_w, row_w, row_w, pl.BlockSpec((1, lo_w), lambda b, t: (0, 0)),
                  row_w, pl.BlockSpec((lo_w, w), lambda b, t: (0, 0)),
                  row_w, pl.BlockSpec((lo_w, w), lambda b, t: (0, 0)),
                  row_w, row_w, row_w,
                  pl.BlockSpec((LANES, LANES), lambda b, t: (0, 0))],
        out_specs=[out_spec] * 7,
        out_shape=[out_shape] * 7,
        scratch_shapes=[pltpu.VMEM((SUBLANES, w), F32)] * 3 + [pltpu.VMEM((SUBLANES, lo_w), F32)],
        compiler_params=_cparams("parallel", "arbitrary"),
        name="rwkv_prep",
    )(proj, proj, proj, proj_lo, vec(mu[:w]), vec(mu[w:2 * w]), vec(mu[2 * w:3 * w]), vec(mu_lo),
      vec(w0), wup_pad, vec(a0), aup_pad, vec(k_k), vec(k_a), vec(r_k), gmat)


def _rwkv_scan_kernel(kk_ref, w_ref, kka_ref, k_ref, r_ref, v_ref, y_ref, s_sc):
    c = pl.program_id(0)
    steps = kk_ref.shape[1]
    n_k, n_rows, _ = s_sc.shape
    blocks = [pl.ds(b * SUBLANES, SUBLANES) for b in range(n_rows // SUBLANES)]

    @pl.when(c == 0)
    def _():
        s_sc[...] = jnp.zeros_like(s_sc)

    def tree_sum(parts):
        while len(parts) > 1:
            parts = [parts[i] + parts[i + 1] for i in range(0, len(parts), 2)]
        return parts[0]

    sa0 = []
    for blk in blocks:
        acc = [None, None]
        for j in range(n_k):
            term = s_sc[j, blk, :] * kk_ref[j, 0:1, :]
            acc[j % 2] = term if acc[j % 2] is None else acc[j % 2] + term
        sa0.append(acc[0] + acc[1])

    def step(t, sa, with_next):
        v = [v_ref[t, blk, :] for blk in blocks]
        yacc = [[None, None] for _ in blocks]
        zacc = [[None, None] for _ in blocks]
        row = pl.ds(t, 1)
        for j in range(n_k):
            w_j, kka_j, k_j, r_j = w_ref[j, row, :], kka_ref[j, row, :], k_ref[j, row, :], r_ref[j, row, :]
            kk_j = kk_ref[j, pl.ds(t + 1, 1), :] if with_next else None
            for b, blk in enumerate(blocks):
                st = s_sc[j, blk, :] * w_j - sa[b] * kka_j + v[b] * k_j
                s_sc[j, blk, :] = st
                ya = st * r_j
                yacc[b][j % 2] = ya if yacc[b][j % 2] is None else yacc[b][j % 2] + ya
                if with_next:
                    za = st * kk_j
                    zacc[b][j % 2] = za if zacc[b][j % 2] is None else zacc[b][j % 2] + za
        for b, blk in enumerate(blocks):
            y_ref[t, blk, :] = tree_sum(yacc[b])
        return tuple(tree_sum(z) for z in zacc) if with_next else None

    sa = lax.fori_loop(0, steps - 1, lambda t, sa: step(t, sa, True), tuple(sa0))
    step(steps - 1, sa, False)


def rwkv_scan(kk_e, w_e, kka_e, k_e, r_e, v_e):
    n_k, s, lanes = kk_e.shape
    n_rows = v_e.shape[1]
    assert n_rows % SUBLANES == 0
    tc = _tile(s, 64)
    vec_spec = pl.BlockSpec((n_k, tc, lanes), lambda c: (0, c, 0))
    row_spec = pl.BlockSpec((tc, n_rows, lanes), lambda c: (c, 0, 0))
    return pl.pallas_call(
        _rwkv_scan_kernel,
        grid=(s // tc,),
        in_specs=[vec_spec] * 5 + [row_spec],
        out_specs=row_spec,
        out_shape=jax.ShapeDtypeStruct((s, n_rows, lanes), F32),
        scratch_shapes=[pltpu.VMEM((n_k, n_rows, lanes), F32)],
        compiler_params=_cparams("arbitrary"),
        name="rwkv_scan",
    )(kk_e, w_e, kka_e, k_e, r_e, v_e)


def _rwkv_out_kernel(y_ref, bonus_ref, gate_ref, gw_ref, gb_ref, g_ref, o_ref, *, n_head):
    y = y_ref[...]
    g = g_ref[...] * (1.0 / n_head)
    mu = _seg_sum(y, g)
    yc = y - mu
    var = _seg_sum(yc * yc, g)
    yn = yc * lax.rsqrt(var + RWKV_GN_EPS) * gw_ref[...] + gb_ref[...]
    o_ref[...] = ((yn + bonus_ref[...]) * _silu(gate_ref[...])).astype(o_ref.dtype)


def rwkv_out(y, bonus, proj, off_g, gn_w, gn_b, n_head):
    m, w = y.shape
    tt = _tile(m, 256)
    assert off_g % w == 0
    gmat = jnp.asarray(np.kron(np.eye(LANES // n_head), np.ones((n_head, n_head))), F32)
    blk = pl.BlockSpec((tt, w), lambda i: (i, 0))
    row = pl.BlockSpec((1, w), lambda i: (0, 0))
    return pl.pallas_call(
        functools.partial(_rwkv_out_kernel, n_head=n_head),
        grid=(m // tt,),
        in_specs=[blk, blk, pl.BlockSpec((tt, w), lambda i: (i, off_g // w)), row, row,
                  pl.BlockSpec((LANES, LANES), lambda i: (0, 0))],
        out_specs=blk,
        out_shape=jax.ShapeDtypeStruct((m, w), BF16),
        compiler_params=_cparams("parallel"),
        name="rwkv_out",
    )(y, bonus, proj, gn_w.reshape(1, w), gn_b.reshape(1, w), gmat)


def _to_chain_kernel(x_ref, o_ref, *, n_heads, n_head, dup, value_rows):
    bsz, t, _ = x_ref.shape
    parts = []
    for b in range(bsz):
        x3 = x_ref[b].T.reshape(n_heads, n_head, t)
        parts.append(jnp.transpose(x3, (1, 0, 2)))
    if value_rows:
        nr = n_head // dup
        e = jnp.concatenate([p[d * nr:(d + 1) * nr] for d in range(dup) for p in parts], axis=1)
        o_ref[...] = jnp.transpose(jnp.swapaxes(e, 1, 2), (1, 0, 2))
    else:
        e = jnp.concatenate(parts * dup, axis=1)
        o_ref[...] = jnp.swapaxes(e, 1, 2)


def to_chain_layout(x, bsz, s, n_heads, n_head, value_rows=False):
    w = n_heads * n_head
    assert LANES % (bsz * n_heads) == 0
    dup = LANES // (bsz * n_heads)
    assert n_head % dup == 0
    tt = _tile(s, LANES)
    if value_rows:
        out_shape, out_block, out_idx = (s, n_head // dup, LANES), (tt, n_head // dup, LANES), lambda i: (i, 0, 0)
    else:
        out_shape, out_block, out_idx = (n_head, s, LANES), (n_head, tt, LANES), lambda i: (0, i, 0)
    return pl.pallas_call(
        functools.partial(_to_chain_kernel, n_heads=n_heads, n_head=n_head, dup=dup, value_rows=value_rows),
        grid=(s // tt,),
        in_specs=[pl.BlockSpec((bsz, tt, w), lambda i: (0, i, 0))],
        out_specs=pl.BlockSpec(out_block, out_idx),
        out_shape=jax.ShapeDtypeStruct(out_shape, F32),
        compiler_params=_cparams("parallel"),
        name="to_chain_rows" if value_rows else "to_chain_keys",
    )(x.reshape(bsz, s, w))


def _from_chain_kernel(y_ref, o_ref, *, n_heads, n_head, dup):
    bsz = o_ref.shape[0]
    t = y_ref.shape[0]
    e = jnp.swapaxes(jnp.transpose(y_ref[...], (1, 0, 2)), 1, 2)
    bh = bsz * n_heads
    for b in range(bsz):
        rows = jnp.concatenate(
            [e[:, d * bh + b * n_heads:d * bh + (b + 1) * n_heads, :] for d in range(dup)], axis=0)
        o_ref[b] = jnp.transpose(rows, (1, 0, 2)).reshape(n_heads * n_head, t).T


def from_chain_layout(y_e, bsz, n_heads, n_head):
    s, n_rows, _ = y_e.shape
    w = n_heads * n_head
    dup = LANES // (bsz * n_heads)
    tt = _tile(s, LANES)
    return pl.pallas_call(
        functools.partial(_from_chain_kernel, n_heads=n_heads, n_head=n_head, dup=dup),
        grid=(s // tt,),
        in_specs=[pl.BlockSpec((tt, n_rows, LANES), lambda i: (i, 0, 0))],
        out_specs=pl.BlockSpec((bsz, tt, w), lambda i: (0, i, 0)),
        out_shape=jax.ShapeDtypeStruct((bsz, s, w), F32),
        compiler_params=_cparams("parallel"),
        name="from_chain_rows",
    )(y_e).reshape(bsz * s, w)


def _mlstm_kernel(qk_ref, v_ref, o_ref, g_ref, if_ref, cw_ref, cb_ref, bias_ref, gw_ref, tril_ref,
                  out_ref, hist_sc, c_sc, m_sc, *, n_heads, dqk, dv):
    t = pl.program_id(1)

    @pl.when(t == 0)
    def _():
        hist_sc[...] = jnp.zeros_like(hist_sc)
        c_sc[...] = jnp.zeros_like(c_sc)
        m_sc[...] = jnp.full_like(m_sc, -jnp.inf)

    qk_in = qk_ref[...]
    rows = qk_in.shape[0]
    qk = _silu(_causal_conv(qk_in, hist_sc[...], cw_ref[...], cb_ref[...]))
    hist_sc[...] = qk_in[rows - SUBLANES:]

    pre = if_ref[...] + bias_ref[...]
    li = pre
    lf = jax.nn.log_sigmoid(pltpu.roll(pre, LANES - n_heads, axis=1))
    tril = tril_ref[...]
    bcum = jnp.dot(tril, lf, precision=HIGHEST, preferred_element_type=F32)
    g_tot = bcum[rows - 1:rows]
    m_prev = m_sc[...]
    le = g_tot - bcum + li
    m_new = jnp.maximum(g_tot + m_prev, jnp.max(le, axis=0, keepdims=True))
    keep = jnp.exp(g_tot + m_prev - m_new)
    wk = jnp.exp(le - m_new)
    m_sc[...] = m_new
    bcum_t = bcum.T
    li_t = li.T
    causal = tril > 0.5
    ones_blk = (lax.broadcasted_iota(jnp.int32, (rows, LANES), 1) == 0).astype(F32)

    for h in range(n_heads):
        q = qk[:, h * dqk:(h + 1) * dqk]
        k = qk[:, (n_heads + h) * dqk:(n_heads + h + 1) * dqk] * (dqk ** -0.5)
        v = v_ref[:, h * dv:(h + 1) * dv]
        v_aug = jnp.concatenate([v, ones_blk], axis=1).astype(BF16)
        b_col = bcum[:, h:h + 1]
        dmat = b_col - bcum_t[h:h + 1, :] + li_t[h:h + 1, :]
        dmat = jnp.where(causal, dmat, -jnp.inf)
        inter = b_col + m_prev[:, h:h + 1]
        m_t = jnp.maximum(inter, jnp.max(dmat, axis=-1, keepdims=True))
        w_intra = jnp.exp(dmat - m_t)
        w_inter = jnp.exp(inter - m_t)
        qb = q.astype(BF16)
        sc = lax.dot_general(qb, k.astype(BF16), (((1,), (1,)), ((), ())),
                             preferred_element_type=F32) * w_intra
        c_aug = c_sc[h]
        tot = (w_inter * jnp.dot(qb, c_aug.astype(BF16), preferred_element_type=F32)
               + jnp.dot(sc.astype(BF16), v_aug, preferred_element_type=F32))
        num = tot[:, :dv]
        den = tot[:, dv:dv + 1]
        hh = num / jnp.maximum(jnp.abs(den), jnp.exp(-m_t))
        kw = (k * wk[:, h:h + 1]).astype(BF16)
        c_sc[h] = keep[:, h:h + 1] * c_aug + lax.dot_general(
            kw, v_aug, (((0,), (0,)), ((), ())), preferred_element_type=F32)

        ho = hh * jax.nn.sigmoid(o_ref[:, h * dv:(h + 1) * dv])
        mu = jnp.mean(ho, axis=-1, keepdims=True)
        hc = ho - mu
        var = jnp.mean(hc * hc, axis=-1, keepdims=True)
        hn = hc * lax.rsqrt(var + MLSTM_GN_EPS) * gw_ref[:, h * dv:(h + 1) * dv]
        out_ref[:, h * dv:(h + 1) * dv] = (hn * _silu(g_ref[:, h * dv:(h + 1) * dv])).astype(out_ref.dtype)


def mlstm_branch(proj, proj_if, offs, bsz, s, conv_w, conv_b, b_i, b_f, gn_w, w_c):
    off_qk, off_v, off_o, off_g, off_if, if_w = offs
    n_heads = b_i.shape[0]
    qk2 = conv_w.shape[1]
    dqk = qk2 // 2 // n_heads
    dv = w_c // n_heads
    ll = _tile(s, 256)
    nt = s // ll
    assert off_qk % qk2 == 0 and off_v % w_c == 0 and off_o % w_c == 0 and off_g % w_c == 0
    assert off_if % if_w == 0 and if_w == LANES and 2 * n_heads <= LANES
    bias = jnp.pad(jnp.concatenate([b_i, b_f]), (0, LANES - 2 * n_heads)).reshape(1, LANES)
    tril = jnp.asarray(np.tril(np.ones((ll, ll))), F32)
    blk = lambda off, width: pl.BlockSpec((ll, width), lambda b, t: (b * nt + t, off // width))
    full = lambda shape: pl.BlockSpec(shape, lambda b, t: (0,) * len(shape))
    return pl.pallas_call(
        functools.partial(_mlstm_kernel, n_heads=n_heads, dqk=dqk, dv=dv),
        grid=(bsz, nt),
        in_specs=[blk(off_qk, qk2), blk(off_v, w_c), blk(off_o, w_c), blk(off_g, w_c), blk(off_if, if_w),
                  full(conv_w.shape), full((1, qk2)), full((1, LANES)), full((1, w_c)), full((ll, ll))],
        out_specs=pl.BlockSpec((ll, w_c), lambda b, t: (b * nt + t, 0)),
        out_shape=jax.ShapeDtypeStruct((bsz * s, w_c), BF16),
        scratch_shapes=[pltpu.VMEM((SUBLANES, qk2), F32),
                        pltpu.VMEM((n_heads, dqk, dv + LANES), F32),
                        pltpu.VMEM((1, LANES), F32)],
        compiler_params=_cparams("parallel", "arbitrary"),
        name="mlstm",
    )(proj, proj, proj, proj, proj_if, conv_w, conv_b.reshape(1, qk2), bias, gn_w.reshape(1, w_c), tril)


def _xattn_kernel(q_ref, gate_ref, kv_ref, o_ref, *, n_heads):
    w_x = q_ref.shape[1]
    for h in range(n_heads):
        sl = slice(h * DH_X, (h + 1) * DH_X)
        q = q_ref[:, sl].astype(BF16)
        k = kv_ref[:, sl].astype(BF16)
        v = kv_ref[:, w_x + h * DH_X:w_x + (h + 1) * DH_X].astype(BF16)
        logits = lax.dot_general(q, k, (((1,), (1,)), ((), ())), preferred_element_type=F32) * (DH_X ** -0.5)
        e = jnp.exp(logits - jnp.max(logits, axis=-1, keepdims=True))
        probs = e / jnp.sum(e, axis=-1, keepdims=True)
        o = jnp.dot(probs.astype(BF16), v, preferred_element_type=F32)
        o_ref[:, sl] = (o * _silu(gate_ref[:, sl])).astype(o_ref.dtype)


def xattn_branch(proj, off_q, off_g, bsz, s, kv, w_x):
    m_len = kv.shape[0] // bsz
    tt = _tile(s, 512)
    nt = s // tt
    assert off_q % w_x == 0 and off_g % w_x == 0 and w_x % DH_X == 0
    return pl.pallas_call(
        functools.partial(_xattn_kernel, n_heads=w_x // DH_X),
        grid=(bsz, nt),
        in_specs=[pl.BlockSpec((tt, w_x), lambda b, t: (b * nt + t, off_q // w_x)),
                  pl.BlockSpec((tt, w_x), lambda b, t: (b * nt + t, off_g // w_x)),
                  pl.BlockSpec((m_len, 2 * w_x), lambda b, t: (b, 0))],
        out_specs=pl.BlockSpec((tt, w_x), lambda b, t: (b * nt + t, 0)),
        out_shape=jax.ShapeDtypeStruct((bsz * s, w_x), BF16),
        compiler_params=_cparams("parallel", "parallel"),
        name="mem_xattn",
    )(proj, proj, kv)


def _round_up(n, m):
    return (n + m - 1) // m * m


def _proj_layout(sizes, w_b, tn, tn_small):
    src = np.concatenate([[0], np.cumsum(sizes)])
    seg = {name: (int(src[i]), int(sizes[i])) for i, name in enumerate(
        ("a_x", "a_g", "b_s", "b_g", "c_qk", "c_v", "c_o", "c_g", "c_if", "x_q", "x_g", "gates"))}
    b0 = seg["b_s"][0]
    seg["b_r"], seg["b_k"], seg["b_v"] = (b0, w_b), (b0 + w_b, w_b), (b0 + 2 * w_b, w_b)
    seg["b_lo"] = (b0 + 3 * w_b, seg["b_s"][1] - 3 * w_b)
    assert seg["x_g"][0] == seg["x_q"][0] + seg["x_q"][1]
    seg["x_qg"] = (seg["x_q"][0], seg["x_q"][1] + seg["x_g"][1])
    offs, main_src, small_src = {}, [], []
    for name in ("a_x", "a_g", "b_g", "c_qk", "c_v", "c_o", "c_g", "b_r", "b_k", "b_v", "gates", "x_qg"):
        start, width = seg[name]
        assert start % SUBLANES == 0 and width % tn == 0, (name, start, width)
        offs[name] = len(main_src) * tn
        main_src.extend(start + b * tn for b in range(width // tn))
    offs["x_q"], offs["x_g"] = offs["x_qg"], offs["x_qg"] + seg["x_q"][1]
    for name in ("b_lo", "c_if"):
        start, width = seg[name]
        assert start % SUBLANES == 0 and width <= tn_small and start + tn_small <= src[-1]
        offs[name] = len(small_src) * tn_small
        small_src.append(start)
    return offs, main_src, small_src


def _in_proj_kernel(src_ref, h_ref, w_ref, o_ref):
    del src_ref
    o_ref[...] = lax.dot_general(h_ref[...], w_ref[...].astype(BF16), (((1,), (1,)), ((), ())),
                                 preferred_element_type=F32)


def in_proj(h, w_in_stacked, layer, block_src, tm, tn, name):
    m, k = h.shape
    n_layers, _, c_in = w_in_stacked.shape
    assert c_in % SUBLANES == 0
    w_t = jnp.swapaxes(w_in_stacked, 1, 2).reshape(n_layers * c_in, k)
    rows8 = jnp.asarray([(layer * c_in + c) // SUBLANES for c in block_src], jnp.int32)
    tm = _tile(m, tm)
    n_blocks = len(block_src)
    grid_spec = pltpu.PrefetchScalarGridSpec(
        num_scalar_prefetch=1,
        grid=(n_blocks, m // tm),
        in_specs=[pl.BlockSpec((tm, k), lambda j, i, src: (i, 0)),
                  pl.BlockSpec((pl.Element(tn), pl.Element(k)), lambda j, i, src: (src[j] * SUBLANES, 0))],
        out_specs=pl.BlockSpec((tm, tn), lambda j, i, src: (i, j)))
    return pl.pallas_call(
        _in_proj_kernel,
        grid_spec=grid_spec,
        out_shape=jax.ShapeDtypeStruct((m, n_blocks * tn), F32),
        compiler_params=_cparams("parallel", "parallel"),
        name=name,
    )(rows8, h, w_t)


def _layer(x2d, memn_src, bsz, s, p, w_in_stacked, layer):
    d = x2d.shape[1]
    w_a = p["lru_lambda"].shape[0]
    n_heads_b, n_head_b = p["rwkv_r_k"].shape
    w_b = n_heads_b * n_head_b
    w_c = p["w_branch_c"].shape[0]
    w_x = p["w_branch_x"].shape[0]
    qk2 = p["mlstm_conv_w"].shape[1]
    n_heads_c = p["mlstm_b_i"].shape[0]
    b_shift_w = p["rwkv_mu"].shape[0]
    sizes = (w_a, w_a, b_shift_w, w_b, qk2, w_c, w_c, w_c, 2 * n_heads_c, w_x, w_x, 4 * d)
    lo_w = _round_up(b_shift_w - 3 * w_b, 256)
    tn = min(1024, w_a, w_b, w_c, 2 * w_x, qk2)
    offs, main_src, small_src = _proj_layout(sizes, w_b, tn, lo_w)

    h = rmsnorm(x2d, p["norm_g"], BF16)
    proj = in_proj(h, w_in_stacked, layer, main_src, 512, tn, "in_proj")
    proj_small = in_proj(h, w_in_stacked, layer, small_src, 1024, lo_w, "in_proj_small")

    y_a = lru_branch(proj, offs["a_x"], offs["a_g"], bsz, s, p["lru_conv_w"], p["lru_conv_b"],
                     p["lru_wa"], p["lru_ba"], p["lru_wx"], p["lru_bx"], p["lru_lambda"])

    r, w, k, v, kk, kka, bonus = rwkv_prep(
        proj, proj_small, (offs["b_r"], offs["b_k"], offs["b_v"], offs["b_lo"], lo_w), bsz, s, p["rwkv_mu"],
        p["rwkv_w0"], p["rwkv_w_up"], p["rwkv_a0"], p["rwkv_a_up"], p["rwkv_k_k"], p["rwkv_k_a"],
        p["rwkv_r_k"], n_head_b)
    to_e = functools.partial(to_chain_layout, bsz=bsz, s=s, n_heads=n_heads_b, n_head=n_head_b)
    y_e = rwkv_scan(to_e(kk), to_e(w), to_e(kka), to_e(k), to_e(r), to_e(v, value_rows=True))
    y_b = rwkv_out(from_chain_layout(y_e, bsz, n_heads_b, n_head_b), bonus, proj, offs["b_g"],
                   p["rwkv_gn_w"], p["rwkv_gn_b"], n_head_b)

    y_c = mlstm_branch(proj, proj_small, (offs["c_qk"], offs["c_v"], offs["c_o"], offs["c_g"], offs["c_if"], LANES),
                       bsz, s, p["mlstm_conv_w"], p["mlstm_conv_b"], p["mlstm_b_i"], p["mlstm_b_f"],
                       p["mlstm_gn_w"], w_c)

    memn = rmsnorm(memn_src, p["mem_norm_g"], BF16)
    kv = matmul(memn, p["xattn_w_kv"], F32, 512, 512, name="mem_kv")
    y_x = xattn_branch(proj, offs["x_q"], offs["x_g"], bsz, s, kv, w_x)

    merged = merge((y_a, y_b, y_c, y_x),
                   tuple(p[n].astype(BF16) for n in ("w_branch_a", "w_branch_b", "w_branch_c", "w_branch_x")),
                   proj, offs["gates"], d)
    return matmul(merged, p["w_out"], F32, 1024, 512, residual=x2d, name="out_proj")


_LAYER_PARAMS = (
    "norm_g", "mem_norm_g", "w_in", "lru_conv_w", "lru_conv_b", "lru_wa", "lru_ba", "lru_wx", "lru_bx",
    "lru_lambda", "rwkv_mu", "rwkv_w0", "rwkv_w_up", "rwkv_a0", "rwkv_a_up", "rwkv_k_k", "rwkv_k_a",
    "rwkv_r_k", "rwkv_gn_w", "rwkv_gn_b", "mlstm_conv_w", "mlstm_conv_b", "mlstm_b_i", "mlstm_b_f",
    "mlstm_gn_w", "xattn_w_kv", "w_branch_a", "w_branch_b", "w_branch_c", "w_branch_x", "w_out")


def kernel(x, mem, norm_g, mem_norm_g, w_in, lru_conv_w, lru_conv_b, lru_wa, lru_ba, lru_wx, lru_bx, lru_lambda, rwkv_mu, rwkv_w0, rwkv_w_up, rwkv_a0, rwkv_a_up, rwkv_k_k, rwkv_k_a, rwkv_r_k, rwkv_gn_w, rwkv_gn_b, mlstm_conv_w, mlstm_conv_b, mlstm_b_i, mlstm_b_f, mlstm_gn_w, xattn_w_kv, w_branch_a, w_branch_b, w_branch_c, w_branch_x, w_out, final_norm_g):
    stacked = dict(zip(_LAYER_PARAMS, (
        norm_g, mem_norm_g, w_in, lru_conv_w, lru_conv_b, lru_wa, lru_ba, lru_wx, lru_bx, lru_lambda,
        rwkv_mu, rwkv_w0, rwkv_w_up, rwkv_a0, rwkv_a_up, rwkv_k_k, rwkv_k_a, rwkv_r_k, rwkv_gn_w,
        rwkv_gn_b, mlstm_conv_w, mlstm_conv_b, mlstm_b_i, mlstm_b_f, mlstm_gn_w, xattn_w_kv,
        w_branch_a, w_branch_b, w_branch_c, w_branch_x, w_out)))
    bsz, s, d = x.shape
    x2d = x.reshape(bsz * s, d)
    mem2d = mem.reshape(-1, d)
    for layer in range(norm_g.shape[0]):
        params = {name: val[layer] for name, val in stacked.items() if name != "w_in"}
        x2d = _layer(x2d, mem2d, bsz, s, params, w_in, layer)
    return rmsnorm(x2d, final_norm_g, x.dtype).reshape(bsz, s, d)
```

```python
import functools

import jax
import jax.numpy as jnp
import numpy as np
from jax import lax
from jax.experimental import pallas as pl
from jax.experimental.pallas import tpu as pltpu

F32 = jnp.float32
BF16 = jnp.bfloat16
HIGHEST = lax.Precision.HIGHEST

LANES = 128
SUBLANES = 8
VMEM_LIMIT = 56 * 1024 * 1024

RMS_EPS = 1e-6
LRU_C = 8.0
RWKV_GN_EPS = 64e-5
MLSTM_GN_EPS = 1e-6
DH_X = 128


def _cparams(*sem):
    return pltpu.CompilerParams(dimension_semantics=sem, vmem_limit_bytes=VMEM_LIMIT)


def _tile(n, want):
    t = min(n, want)
    assert n % t == 0, (n, want)
    return t


def _silu(x):
    return x * jax.nn.sigmoid(x)


def _rmsnorm_kernel(x_ref, g_ref, o_ref):
    x = x_ref[...]
    ms = jnp.mean(x * x, axis=-1, keepdims=True)
    o_ref[...] = (x * lax.rsqrt(ms + RMS_EPS) * g_ref[...]).astype(o_ref.dtype)


def rmsnorm(x2d, g, out_dtype):
    m, d = x2d.shape
    tm = _tile(m, 256)
    return pl.pallas_call(
        _rmsnorm_kernel,
        grid=(m // tm,),
        in_specs=[pl.BlockSpec((tm, d), lambda i: (i, 0)),
                  pl.BlockSpec((1, d), lambda i: (0, 0))],
        out_specs=pl.BlockSpec((tm, d), lambda i: (i, 0)),
        out_shape=jax.ShapeDtypeStruct((m, d), out_dtype),
        compiler_params=_cparams("parallel"),
        name="rmsnorm",
    )(x2d, g.reshape(1, d))


def _mm_kernel(a_ref, b_ref, o_ref):
    acc = jnp.dot(a_ref[...], b_ref[...].astype(BF16), preferred_element_type=F32)
    o_ref[...] = acc.astype(o_ref.dtype)


def _mm_res_kernel(a_ref, b_ref, r_ref, o_ref):
    acc = jnp.dot(a_ref[...], b_ref[...].astype(BF16), preferred_element_type=F32)
    o_ref[...] = (r_ref[...] + acc).astype(o_ref.dtype)


def matmul(a, b, layer, out_dtype, tm, tn, residual=None, name="matmul"):
    m, k = a.shape
    _, _, n = b.shape
    tm, tn = _tile(m, tm), _tile(n, tn)
    in_specs = [pl.BlockSpec((tm, k), lambda j, i: (i, 0)),
                pl.BlockSpec((None, k, tn), lambda j, i: (layer, 0, j))]
    args = [a, b]
    kern = _mm_kernel
    if residual is not None:
        in_specs.append(pl.BlockSpec((tm, tn), lambda j, i: (i, j)))
        args.append(residual)
        kern = _mm_res_kernel
    return pl.pallas_call(
        kern,
        grid=(n // tn, m // tm),
        in_specs=in_specs,
        out_specs=pl.BlockSpec((tm, tn), lambda j, i: (i, j)),
        out_shape=jax.ShapeDtypeStruct((m, n), out_dtype),
        compiler_params=_cparams("parallel", "parallel"),
        name=name,
    )(*args)


def _merge_kernel(ya_ref, yb_ref, yc_ref, yx_ref, wa_ref, wb_ref, wc_ref, wx_ref,
                  g0_ref, g1_ref, g2_ref, g3_ref, o_ref):
    def term(y_ref, w_ref, g_ref):
        return jax.nn.sigmoid(g_ref[...]) * jnp.dot(y_ref[...], w_ref[...].astype(BF16),
                                                    preferred_element_type=F32)

    acc = term(ya_ref, wa_ref, g0_ref)
    acc = acc + term(yb_ref, wb_ref, g1_ref)
    acc = acc + term(yc_ref, wc_ref, g2_ref)
    acc = acc + term(yx_ref, wx_ref, g3_ref)
    o_ref[...] = acc.astype(o_ref.dtype)


def merge(ys, ws, layer, proj, gate_off, d):
    m = proj.shape[0]
    tm, tn = _tile(m, 512), _tile(d, 256)
    assert gate_off % tn == 0
    y_specs = [pl.BlockSpec((tm, y.shape[1]), lambda j, i: (i, 0)) for y in ys]
    w_specs = [pl.BlockSpec((None, w.shape[1], tn), lambda j, i: (layer, 0, j)) for w in ws]
    g_specs = [pl.BlockSpec((tm, tn), functools.partial(
        lambda j, i, base: (i, base + j), base=(gate_off + k * d) // tn)) for k in range(4)]
    return pl.pallas_call(
        _merge_kernel,
        grid=(d // tn, m // tm),
        in_specs=y_specs + w_specs + g_specs,
        out_specs=pl.BlockSpec((tm, tn), lambda j, i: (i, j)),
        out_shape=jax.ShapeDtypeStruct((m, d), BF16),
        compiler_params=_cparams("parallel", "parallel"),
        name="merge",
    )(*ys, *ws, proj, proj, proj, proj)


def _shift_rows(x, shift, hist):
    rolled = pltpu.roll(x, shift, axis=0)
    head = pltpu.roll(hist, shift, axis=0)
    row = lax.broadcasted_iota(jnp.int32, (SUBLANES, x.shape[1]), 0)
    fixed = jnp.where(row < shift, head, rolled[:SUBLANES])
    return jnp.concatenate([fixed, rolled[SUBLANES:]], axis=0)


def _causal_conv(x, hist, w, b):
    kw = w.shape[0]
    out = x * w[kw - 1:kw] + b
    for j in range(kw - 1):
        out = out + _shift_rows(x, kw - 1 - j, hist) * w[j:j + 1]
    return out


def _lru_kernel(ax_ref, ag_ref, cw_ref, cb_ref, wa_ref, ba_ref, wx_ref, bx_ref, lam_ref, o_ref,
                hist_sc, hcar_sc, a_sc, b_sc, h_sc):
    t = pl.program_id(2)

    @pl.when(t == 0)
    def _():
        hist_sc[...] = jnp.zeros_like(hist_sc)
        hcar_sc[...] = jnp.zeros_like(hcar_sc)

    x = ax_ref[...]
    rows = x.shape[0]
    u = _causal_conv(x, hist_sc[...], cw_ref[...], cb_ref[...])
    hist_sc[...] = x[rows - SUBLANES:]
    ub = u.astype(BF16)
    r = jax.nn.sigmoid(jnp.dot(ub, wa_ref[...], preferred_element_type=F32) + ba_ref[...])
    i = jax.nn.sigmoid(jnp.dot(ub, wx_ref[...], preferred_element_type=F32) + bx_ref[...])
    log_a = -LRU_C * r * jax.nn.softplus(-lam_ref[...])
    a = jnp.exp(log_a)
    b = jnp.sqrt(-jnp.tanh(log_a) * (a * a + 1.0)) * (i * u)

    in_group = lax.broadcasted_iota(jnp.int32, x.shape, 0) % SUBLANES
    for s in (1, 2, 4):
        keep = in_group >= s
        b = jnp.where(keep, a * pltpu.roll(b, s, axis=0) + b, b)
        a = jnp.where(keep, a * pltpu.roll(a, s, axis=0), a)
    a_sc[...] = a
    b_sc[...] = b

    def group(g, h_prev):
        sl = pl.ds(pl.multiple_of(g * SUBLANES, SUBLANES), SUBLANES)
        h8 = a_sc[sl, :] * h_prev + b_sc[sl, :]
        h_sc[sl, :] = h8
        return h8[SUBLANES - 1:SUBLANES]

    hcar_sc[...] = lax.fori_loop(0, rows // SUBLANES, group, hcar_sc[...], unroll=4)
    o_ref[...] = (h_sc[...] * _silu(ag_ref[...])).astype(o_ref.dtype)


def lru_branch(proj, off_x, off_g, bsz, s, cw, cb, wa, ba, wx, bx, lam):
    nb, bs, _ = wa.shape
    w = nb * bs
    tt = _tile(s, 512)
    nt = s // tt
    assert off_x % bs == 0 and off_g % bs == 0
    vec = lambda v: v.reshape(1, w)
    vspec = pl.BlockSpec((1, bs), lambda b, n, t: (0, n))
    return pl.pallas_call(
        _lru_kernel,
        grid=(bsz, nb, nt),
        in_specs=[
            pl.BlockSpec((tt, bs), lambda b, n, t: (b * nt + t, off_x // bs + n)),
            pl.BlockSpec((tt, bs), lambda b, n, t: (b * nt + t, off_g // bs + n)),
            pl.BlockSpec((cw.shape[0], bs), lambda b, n, t: (0, n)),
            vspec,
            pl.BlockSpec((None, bs, bs), lambda b, n, t: (n, 0, 0)),
            vspec,
            pl.BlockSpec((None, bs, bs), lambda b, n, t: (n, 0, 0)),
            vspec,
            vspec,
        ],
        out_specs=pl.BlockSpec((tt, bs), lambda b, n, t: (b * nt + t, n)),
        out_shape=jax.ShapeDtypeStruct((bsz * s, w), BF16),
        scratch_shapes=[pltpu.VMEM((SUBLANES, bs), F32), pltpu.VMEM((1, bs), F32),
                        pltpu.VMEM((tt, bs), F32), pltpu.VMEM((tt, bs), F32), pltpu.VMEM((tt, bs), F32)],
        compiler_params=_cparams("parallel", "parallel", "arbitrary"),
        name="rg_lru",
    )(proj, proj, cw, vec(cb), wa.astype(BF16), vec(ba), wx.astype(BF16), vec(bx), vec(lam))


def _seg_sum(x, g):
    parts = [jnp.dot(x[:, c * LANES:(c + 1) * LANES], g, precision=HIGHEST, preferred_element_type=F32)
             for c in range(x.shape[1] // LANES)]
    return parts[0] if len(parts) == 1 else jnp.concatenate(parts, axis=1)


def _rwkv_prep_kernel(pr_ref, pk_ref, pv_ref, plo_ref, mur_ref, muk_ref, muv_ref, mulo_ref,
                      w0_ref, wup_ref, a0_ref, aup_ref, kk_ref, ka_ref, rk_ref, g_ref,
                      r_o, w_o, k_o, v_o, kk_o, kka_o, bonus_o,
                      lr_sc, lk_sc, lv_sc, llo_sc):
    t = pl.program_id(1)

    @pl.when(t == 0)
    def _():
        for sc in (lr_sc, lk_sc, lv_sc, llo_sc):
            sc[...] = jnp.zeros_like(sc)

    def shifted(p_ref, mu_ref, last_sc):
        p = p_ref[...]
        rows = p.shape[0]
        prev = _shift_rows(p, 1, last_sc[...])
        last_sc[...] = p[rows - SUBLANES:]
        return p + (prev - p) * mu_ref[...]

    r = shifted(pr_ref, mur_ref, lr_sc)
    k = shifted(pk_ref, muk_ref, lk_sc)
    v = shifted(pv_ref, muv_ref, lv_sc)
    lo = shifted(plo_ref, mulo_ref, llo_sc)

    w_pre = w0_ref[...] + jnp.dot(jnp.tanh(lo).astype(BF16), wup_ref[...], preferred_element_type=F32)
    w_log = -jax.nn.softplus(-w_pre) - 0.5
    decay = jnp.exp(-jnp.exp(w_log))
    a = jax.nn.sigmoid(a0_ref[...] + jnp.dot(lo.astype(BF16), aup_ref[...], preferred_element_type=F32))

    g = g_ref[...]
    kk = k * kk_ref[...]
    kk = kk / jnp.maximum(jnp.sqrt(_seg_sum(kk * kk, g)), 1e-12)
    k2 = k * (1.0 + (a - 1.0) * ka_ref[...])
    r_o[...] = r
    w_o[...] = decay
    k_o[...] = k2
    v_o[...] = v
    kk_o[...] = kk
    kka_o[...] = kk * a
    bonus_o[...] = _seg_sum(r * k2 * rk_ref[...], g) * v


def rwkv_prep(proj, proj_lo, offs, bsz, s, mu, w0, w_up, a0, a_up, k_k, k_a, r_k, n_head):
    off_r, off_k, off_v, off_lo, lo_w = offs
    r_w, w = w_up.shape
    r_a = a_up.shape[0]
    tt = _tile(s, 256)
    nt = s // tt
    assert off_r % w == 0 and off_k % w == 0 and off_v % w == 0 and off_lo % lo_w == 0
    assert LANES % n_head == 0 and w % LANES == 0
    vec = lambda x: x.reshape(1, -1)
    mu_lo = jnp.pad(mu[3 * w:], (0, lo_w - r_w - r_a))
    wup_pad = jnp.zeros((lo_w, w), F32).at[:r_w].set(w_up).astype(BF16)
    aup_pad = jnp.zeros((lo_w, w), F32).at[r_w:r_w + r_a].set(a_up).astype(BF16)
    gmat = jnp.asarray(np.kron(np.eye(LANES // n_head), np.ones((n_head, n_head))), F32)
    row_w = pl.BlockSpec((1, w), lambda b, t: (0, 0))
    blk = lambda off, width: pl.BlockSpec((tt, width), lambda b, t: (b * nt + t, off // width))
    out_spec = pl.BlockSpec((tt, w), lambda b, t: (b * nt + t, 0))
    out_shape = jax.ShapeDtypeStruct((bsz * s, w), F32)
    return pl.pallas_call(
        _rwkv_prep_kernel,
        grid=(bsz, nt),
        in_specs=[blk(off_r, w), blk(off_k, w), blk(off_v, w), blk(off_lo, lo_w),
                  row_w, row_w, row_w, pl.BlockSpec((1, lo_w), lambda b, t: (0, 0)),
                  row_w, pl.BlockSpec((lo_w, w), lambda b, t: (0, 0)),
                  row_w, pl.BlockSpec((lo_w, w), lambda b, t: (0, 0)),
                  row_w, row_w, row_w,
                  pl.BlockSpec((LANES, LANES), lambda b, t: (0, 0))],
        out_specs=[out_spec] * 7,
        out_shape=[out_shape] * 7,
        scratch_shapes=[pltpu.VMEM((SUBLANES, w), F32)] * 3 + [pltpu.VMEM((SUBLANES, lo_w), F32)],
        compiler_params=_cparams("parallel", "arbitrary"),
        name="rwkv_prep",
    )(proj, proj, proj, proj_lo, vec(mu[:w]), vec(mu[w:2 * w]), vec(mu[2 * w:3 * w]), vec(mu_lo),
      vec(w0), wup_pad, vec(a0), aup_pad, vec(k_k), vec(k_a), vec(r_k), gmat)


def _rwkv_scan_kernel(kk_ref, w_ref, kka_ref, k_ref, r_ref, v_ref, y_ref, s_sc):
    c = pl.program_id(0)
    steps = kk_ref.shape[1]
    n_k, n_rows, _ = s_sc.shape
    blocks = [pl.ds(b * SUBLANES, SUBLANES) for b in range(n_rows // SUBLANES)]

    @pl.when(c == 0)
    def _():
        s_sc[...] = jnp.zeros_like(s_sc)

    def tree_sum(parts):
        while len(parts) > 1:
            parts = [parts[i] + parts[i + 1] for i in range(0, len(parts), 2)]
        return parts[0]

    sa0 = []
    for blk in blocks:
        acc = [None, None]
        for j in range(n_k):
            term = s_sc[j, blk, :] * kk_ref[j, 0:1, :]
            acc[j % 2] = term if acc[j % 2] is None else acc[j % 2] + term
        sa0.append(acc[0] + acc[1])

    def step(t, sa, with_next):
        v = [v_ref[t, blk, :] for blk in blocks]
        yacc = [[None, None] for _ in blocks]
        zacc = [[None, None] for _ in blocks]
        row = pl.ds(t, 1)
        for j in range(n_k):
            w_j, kka_j, k_j, r_j = w_ref[j, row, :], kka_ref[j, row, :], k_ref[j, row, :], r_ref[j, row, :]
            kk_j = kk_ref[j, pl.ds(t + 1, 1), :] if with_next else None
            for b, blk in enumerate(blocks):
                st = s_sc[j, blk, :] * w_j - sa[b] * kka_j + v[b] * k_j
                s_sc[j, blk, :] = st
                ya = st * r_j
                yacc[b][j % 2] = ya if yacc[b][j % 2] is None else yacc[b][j % 2] + ya
                if with_next:
                    za = st * kk_j
                    zacc[b][j % 2] = za if zacc[b][j % 2] is None else zacc[b][j % 2] + za
        for b, blk in enumerate(blocks):
            y_ref[t, blk, :] = tree_sum(yacc[b])
        return tuple(tree_sum(z) for z in zacc) if with_next else None

    sa = lax.fori_loop(0, steps - 1, lambda t, sa: step(t, sa, True), tuple(sa0))
    step(steps - 1, sa, False)


def rwkv_scan(kk_e, w_e, kka_e, k_e, r_e, v_e):
    n_k, s, lanes = kk_e.shape
    n_rows = v_e.shape[1]
    assert n_rows % SUBLANES == 0
    tc = _tile(s, 64)
    vec_spec = pl.BlockSpec((n_k, tc, lanes), lambda c: (0, c, 0))
    row_spec = pl.BlockSpec((tc, n_rows, lanes), lambda c: (c, 0, 0))
    return pl.pallas_call(
        _rwkv_scan_kernel,
        grid=(s // tc,),
        in_specs=[vec_spec] * 5 + [row_spec],
        out_specs=row_spec,
        out_shape=jax.ShapeDtypeStruct((s, n_rows, lanes), F32),
        scratch_shapes=[pltpu.VMEM((n_k, n_rows, lanes), F32)],
        compiler_params=_cparams("arbitrary"),
        name="rwkv_scan",
    )(kk_e, w_e, kka_e, k_e, r_e, v_e)


def _rwkv_out_kernel(y_ref, bonus_ref, gate_ref, gw_ref, gb_ref, g_ref, o_ref, *, n_head):
    y = y_ref[...]
    g = g_ref[...] * (1.0 / n_head)
    mu = _seg_sum(y, g)
    yc = y - mu
    var = _seg_sum(yc * yc, g)
    yn = yc * lax.rsqrt(var + RWKV_GN_EPS) * gw_ref[...] + gb_ref[...]
    o_ref[...] = ((yn + bonus_ref[...]) * _silu(gate_ref[...])).astype(o_ref.dtype)


def rwkv_out(y, bonus, proj, off_g, gn_w, gn_b, n_head):
    m, w = y.shape
    tt = _tile(m, 256)
    assert off_g % w == 0
    gmat = jnp.asarray(np.kron(np.eye(LANES // n_head), np.ones((n_head, n_head))), F32)
    blk = pl.BlockSpec((tt, w), lambda i: (i, 0))
    row = pl.BlockSpec((1, w), lambda i: (0, 0))
    return pl.pallas_call(
        functools.partial(_rwkv_out_kernel, n_head=n_head),
        grid=(m // tt,),
        in_specs=[blk, blk, pl.BlockSpec((tt, w), lambda i: (i, off_g // w)), row, row,
                  pl.BlockSpec((LANES, LANES), lambda i: (0, 0))],
        out_specs=blk,
        out_shape=jax.ShapeDtypeStruct((m, w), BF16),
        compiler_params=_cparams("parallel"),
        name="rwkv_out",
    )(y, bonus, proj, gn_w.reshape(1, w), gn_b.reshape(1, w), gmat)


def _to_chain_kernel(x_ref, o_ref, *, n_heads, n_head, dup, value_rows):
    bsz, t, _ = x_ref.shape
    parts = []
    for b in range(bsz):
        x3 = x_ref[b].T.reshape(n_heads, n_head, t)
        parts.append(jnp.transpose(x3, (1, 0, 2)))
    if value_rows:
        nr = n_head // dup
        e = jnp.concatenate([p[d * nr:(d + 1) * nr] for d in range(dup) for p in parts], axis=1)
        o_ref[...] = jnp.transpose(jnp.swapaxes(e, 1, 2), (1, 0, 2))
    else:
        e = jnp.concatenate(parts * dup, axis=1)
        o_ref[...] = jnp.swapaxes(e, 1, 2)


def to_chain_layout(x, bsz, s, n_heads, n_head, value_rows=False):
    w = n_heads * n_head
    assert LANES % (bsz * n_heads) == 0
    dup = LANES // (bsz * n_heads)
    assert n_head % dup == 0
    tt = _tile(s, LANES)
    if value_rows:
        out_shape, out_block, out_idx = (s, n_head // dup, LANES), (tt, n_head // dup, LANES), lambda i: (i, 0, 0)
    else:
        out_shape, out_block, out_idx = (n_head, s, LANES), (n_head, tt, LANES), lambda i: (0, i, 0)
    return pl.pallas_call(
        functools.partial(_to_chain_kernel, n_heads=n_heads, n_head=n_head, dup=dup, value_rows=value_rows),
        grid=(s // tt,),
        in_specs=[pl.BlockSpec((bsz, tt, w), lambda i: (0, i, 0))],
        out_specs=pl.BlockSpec(out_block, out_idx),
        out_shape=jax.ShapeDtypeStruct(out_shape, F32),
        compiler_params=_cparams("parallel"),
        name="to_chain_rows" if value_rows else "to_chain_keys",
    )(x.reshape(bsz, s, w))


def _from_chain_kernel(y_ref, o_ref, *, n_heads, n_head, dup):
    bsz = o_ref.shape[0]
    t = y_ref.shape[0]
    e = jnp.swapaxes(jnp.transpose(y_ref[...], (1, 0, 2)), 1, 2)
    bh = bsz * n_heads
    for b in range(bsz):
        rows = jnp.concatenate(
            [e[:, d * bh + b * n_heads:d * bh + (b + 1) * n_heads, :] for d in range(dup)], axis=0)
        o_ref[b] = jnp.transpose(rows, (1, 0, 2)).reshape(n_heads * n_head, t).T


def from_chain_layout(y_e, bsz, n_heads, n_head):
    s, n_rows, _ = y_e.shape
    w = n_heads * n_head
    dup = LANES // (bsz * n_heads)
    tt = _tile(s, LANES)
    return pl.pallas_call(
        functools.partial(_from_chain_kernel, n_heads=n_heads, n_head=n_head, dup=dup),
        grid=(s // tt,),
        in_specs=[pl.BlockSpec((tt, n_rows, LANES), lambda i: (i, 0, 0))],
        out_specs=pl.BlockSpec((bsz, tt, w), lambda i: (0, i, 0)),
        out_shape=jax.ShapeDtypeStruct((bsz, s, w), F32),
        compiler_params=_cparams("parallel"),
        name="from_chain_rows",
    )(y_e).reshape(bsz * s, w)


def _mlstm_kernel(qk_ref, v_ref, o_ref, g_ref, if_ref, cw_ref, cb_ref, bias_ref, gw_ref, tril_ref,
                  out_ref, hist_sc, c_sc, m_sc, *, n_heads, dqk, dv):
    t = pl.program_id(1)

    @pl.when(t == 0)
    def _():
        hist_sc[...] = jnp.zeros_like(hist_sc)
        c_sc[...] = jnp.zeros_like(c_sc)
        m_sc[...] = jnp.full_like(m_sc, -jnp.inf)

    qk_in = qk_ref[...]
    rows = qk_in.shape[0]
    qk = _silu(_causal_conv(qk_in, hist_sc[...], cw_ref[...], cb_ref[...]))
    hist_sc[...] = qk_in[rows - SUBLANES:]

    pre = if_ref[...] + bias_ref[...]
    li = pre
    lf = jax.nn.log_sigmoid(pltpu.roll(pre, LANES - n_heads, axis=1))
    tril = tril_ref[...]
    bcum = jnp.dot(tril, lf, precision=HIGHEST, preferred_element_type=F32)
    g_tot = bcum[rows - 1:rows]
    m_prev = m_sc[...]
    le = g_tot - bcum + li
    m_new = jnp.maximum(g_tot + m_prev, jnp.max(le, axis=0, keepdims=True))
    keep = jnp.exp(g_tot + m_prev - m_new)
    wk = jnp.exp(le - m_new)
    m_sc[...] = m_new
    bcum_t = bcum.T
    li_t = li.T
    causal = tril > 0.5
    ones_blk = (lax.broadcasted_iota(jnp.int32, (rows, LANES), 1) == 0).astype(F32)

    for h in range(n_heads):
        q = qk[:, h * dqk:(h + 1) * dqk]
        k = qk[:, (n_heads + h) * dqk:(n_heads + h + 1) * dqk] * (dqk ** -0.5)
        v = v_ref[:, h * dv:(h + 1) * dv]
        v_aug = jnp.concatenate([v, ones_blk], axis=1).astype(BF16)
        b_col = bcum[:, h:h + 1]
        dmat = b_col - bcum_t[h:h + 1, :] + li_t[h:h + 1, :]
        dmat = jnp.where(causal, dmat, -jnp.inf)
        inter = b_col + m_prev[:, h:h + 1]
        m_t = jnp.maximum(inter, jnp.max(dmat, axis=-1, keepdims=True))
        w_intra = jnp.exp(dmat - m_t)
        w_inter = jnp.exp(inter - m_t)
        qb = q.astype(BF16)
        sc = lax.dot_general(qb, k.astype(BF16), (((1,), (1,)), ((), ())),
                             preferred_element_type=F32) * w_intra
        c_aug = c_sc[h]
        tot = (w_inter * jnp.dot(qb, c_aug.astype(BF16), preferred_element_type=F32)
               + jnp.dot(sc.astype(BF16), v_aug, preferred_element_type=F32))
        num = tot[:, :dv]
        den = tot[:, dv:dv + 1]
        hh = num / jnp.maximum(jnp.abs(den), jnp.exp(-m_t))
        kw = (k * wk[:, h:h + 1]).astype(BF16)
        c_sc[h] = keep[:, h:h + 1] * c_aug + lax.dot_general(
            kw, v_aug, (((0,), (0,)), ((), ())), preferred_element_type=F32)

        ho = hh * jax.nn.sigmoid(o_ref[:, h * dv:(h + 1) * dv])
        mu = jnp.mean(ho, axis=-1, keepdims=True)
        hc = ho - mu
        var = jnp.mean(hc * hc, axis=-1, keepdims=True)
        hn = hc * lax.rsqrt(var + MLSTM_GN_EPS) * gw_ref[:, h * dv:(h + 1) * dv]
        out_ref[:, h * dv:(h + 1) * dv] = (hn * _silu(g_ref[:, h * dv:(h + 1) * dv])).astype(out_ref.dtype)


def mlstm_branch(proj, proj_if, offs, bsz, s, conv_w, conv_b, b_i, b_f, gn_w, w_c):
    off_qk, off_v, off_o, off_g, off_if, if_w = offs
    n_heads = b_i.shape[0]
    qk2 = conv_w.shape[1]
    dqk = qk2 // 2 // n_heads
    dv = w_c // n_heads
    ll = _tile(s, 256)
    nt = s // ll
    assert off_qk % qk2 == 0 and off_v % w_c == 0 and off_o % w_c == 0 and off_g % w_c == 0
    assert off_if % if_w == 0 and if_w == LANES and 2 * n_heads <= LANES
    bias = jnp.pad(jnp.concatenate([b_i, b_f]), (0, LANES - 2 * n_heads)).reshape(1, LANES)
    tril = jnp.asarray(np.tril(np.ones((ll, ll))), F32)
    blk = lambda off, width: pl.BlockSpec((ll, width), lambda b, t: (b * nt + t, off // width))
    full = lambda shape: pl.BlockSpec(shape, lambda b, t: (0,) * len(shape))
    return pl.pallas_call(
        functools.partial(_mlstm_kernel, n_heads=n_heads, dqk=dqk, dv=dv),
        grid=(bsz, nt),
        in_specs=[blk(off_qk, qk2), blk(off_v, w_c), blk(off_o, w_c), blk(off_g, w_c), blk(off_if, if_w),
                  full(conv_w.shape), full((1, qk2)), full((1, LANES)), full((1, w_c)), full((ll, ll))],
        out_specs=pl.BlockSpec((ll, w_c), lambda b, t: (b * nt + t, 0)),
        out_shape=jax.ShapeDtypeStruct((bsz * s, w_c), BF16),
        scratch_shapes=[pltpu.VMEM((SUBLANES, qk2), F32),
                        pltpu.VMEM((n_heads, dqk, dv + LANES), F32),
                        pltpu.VMEM((1, LANES), F32)],
        compiler_params=_cparams("parallel", "arbitrary"),
        name="mlstm",
    )(proj, proj, proj, proj, proj_if, conv_w, conv_b.reshape(1, qk2), bias, gn_w.reshape(1, w_c), tril)


def _xattn_kernel(q_ref, gate_ref, kv_ref, o_ref, *, n_heads):
    w_x = q_ref.shape[1]
    for h in range(n_heads):
        sl = slice(h * DH_X, (h + 1) * DH_X)
        q = q_ref[:, sl].astype(BF16)
        k = kv_ref[:, sl].astype(BF16)
        v = kv_ref[:, w_x + h * DH_X:w_x + (h + 1) * DH_X].astype(BF16)
        logits = lax.dot_general(q, k, (((1,), (1,)), ((), ())), preferred_element_type=F32) * (DH_X ** -0.5)
        e = jnp.exp(logits - jnp.max(logits, axis=-1, keepdims=True))
        probs = e / jnp.sum(e, axis=-1, keepdims=True)
        o = jnp.dot(probs.astype(BF16), v, preferred_element_type=F32)
        o_ref[:, sl] = (o * _silu(gate_ref[:, sl])).astype(o_ref.dtype)


def xattn_branch(proj, off_q, off_g, bsz, s, kv, w_x):
    m_len = kv.shape[0] // bsz
    tt = _tile(s, 512)
    nt = s // tt
    assert off_q % w_x == 0 and off_g % w_x == 0 and w_x % DH_X == 0
    return pl.pallas_call(
        functools.partial(_xattn_kernel, n_heads=w_x // DH_X),
        grid=(bsz, nt),
        in_specs=[pl.BlockSpec((tt, w_x), lambda b, t: (b * nt + t, off_q // w_x)),
                  pl.BlockSpec((tt, w_x), lambda b, t: (b * nt + t, off_g // w_x)),
                  pl.BlockSpec((m_len, 2 * w_x), lambda b, t: (b, 0))],
        out_specs=pl.BlockSpec((tt, w_x), lambda b, t: (b * nt + t, 0)),
        out_shape=jax.ShapeDtypeStruct((bsz * s, w_x), BF16),
        compiler_params=_cparams("parallel", "parallel"),
        name="mem_xattn",
    )(proj, proj, kv)


def _round_up(n, m):
    return (n + m - 1) // m * m


def _proj_layout(sizes, w_b, tn, tn_small):
    src = np.concatenate([[0], np.cumsum(sizes)])
    seg = {name: (int(src[i]), int(sizes[i])) for i, name in enumerate(
        ("a_x", "a_g", "b_s", "b_g", "c_qk", "c_v", "c_o", "c_g", "c_if", "x_q", "x_g", "gates"))}
    b0 = seg["b_s"][0]
    seg["b_r"], seg["b_k"], seg["b_v"] = (b0, w_b), (b0 + w_b, w_b), (b0 + 2 * w_b, w_b)
    seg["b_lo"] = (b0 + 3 * w_b, seg["b_s"][1] - 3 * w_b)
    assert seg["x_g"][0] == seg["x_q"][0] + seg["x_q"][1]
    seg["x_qg"] = (seg["x_q"][0], seg["x_q"][1] + seg["x_g"][1])
    offs, main_src, small_src = {}, [], []
    for name in ("a_x", "a_g", "b_g", "c_qk", "c_v", "c_o", "c_g", "b_r", "b_k", "b_v", "gates", "x_qg"):
        start, width = seg[name]
        assert start % SUBLANES == 0 and width % tn == 0, (name, start, width)
        offs[name] = len(main_src) * tn
        main_src.extend(start + b * tn for b in range(width // tn))
    offs["x_q"], offs["x_g"] = offs["x_qg"], offs["x_qg"] + seg["x_q"][1]
    for name in ("b_lo", "c_if"):
        start, width = seg[name]
        assert start % SUBLANES == 0 and width <= tn_small and start + tn_small <= src[-1]
        offs[name] = len(small_src) * tn_small
        small_src.append(start)
    return offs, main_src, small_src


def _in_proj_kernel(src_ref, h_ref, w_ref, o_ref):
    del src_ref
    o_ref[...] = lax.dot_general(h_ref[...], w_ref[...].astype(BF16), (((1,), (1,)), ((), ())),
                                 preferred_element_type=F32)


def in_proj(h, w_in_stacked, layer, block_src, tm, tn, name):
    m, k = h.shape
    n_layers, _, c_in = w_in_stacked.shape
    assert c_in % SUBLANES == 0
    w_t = jnp.swapaxes(w_in_stacked, 1, 2).reshape(n_layers * c_in, k)
    rows8 = jnp.asarray([(layer * c_in + c) // SUBLANES for c in block_src], jnp.int32)
    tm = _tile(m, tm)
    n_blocks = len(block_src)
    grid_spec = pltpu.PrefetchScalarGridSpec(
        num_scalar_prefetch=1,
        grid=(n_blocks, m // tm),
        in_specs=[pl.BlockSpec((tm, k), lambda j, i, src: (i, 0)),
                  pl.BlockSpec((pl.Element(tn), pl.Element(k)), lambda j, i, src: (src[j] * SUBLANES, 0))],
        out_specs=pl.BlockSpec((tm, tn), lambda j, i, src: (i, j)))
    return pl.pallas_call(
        _in_proj_kernel,
        grid_spec=grid_spec,
        out_shape=jax.ShapeDtypeStruct((m, n_blocks * tn), F32),
        compiler_params=_cparams("parallel", "parallel"),
        name=name,
    )(rows8, h, w_t)


def _layer(x2d, memn_src, bsz, s, p, stacked, layer):
    d = x2d.shape[1]
    w_a = p["lru_lambda"].shape[0]
    n_heads_b, n_head_b = p["rwkv_r_k"].shape
    w_b = n_heads_b * n_head_b
    w_c = stacked["w_branch_c"].shape[1]
    w_x = stacked["w_branch_x"].shape[1]
    qk2 = p["mlstm_conv_w"].shape[1]
    n_heads_c = p["mlstm_b_i"].shape[0]
    b_shift_w = p["rwkv_mu"].shape[0]
    sizes = (w_a, w_a, b_shift_w, w_b, qk2, w_c, w_c, w_c, 2 * n_heads_c, w_x, w_x, 4 * d)
    lo_w = _round_up(b_shift_w - 3 * w_b, 256)
    tn = min(1024, w_a, w_b, w_c, 2 * w_x, qk2)
    offs, main_src, small_src = _proj_layout(sizes, w_b, tn, lo_w)

    h = rmsnorm(x2d, p["norm_g"], BF16)
    proj = in_proj(h, stacked["w_in"], layer, main_src, 512, tn, "in_proj")
    proj_small = in_proj(h, stacked["w_in"], layer, small_src, 1024, lo_w, "in_proj_small")

    y_a = lru_branch(proj, offs["a_x"], offs["a_g"], bsz, s, p["lru_conv_w"], p["lru_conv_b"],
                     p["lru_wa"], p["lru_ba"], p["lru_wx"], p["lru_bx"], p["lru_lambda"])

    r, w, k, v, kk, kka, bonus = rwkv_prep(
        proj, proj_small, (offs["b_r"], offs["b_k"], offs["b_v"], offs["b_lo"], lo_w), bsz, s, p["rwkv_mu"],
        p["rwkv_w0"], p["rwkv_w_up"], p["rwkv_a0"], p["rwkv_a_up"], p["rwkv_k_k"], p["rwkv_k_a"],
        p["rwkv_r_k"], n_head_b)
    to_e = functools.partial(to_chain_layout, bsz=bsz, s=s, n_heads=n_heads_b, n_head=n_head_b)
    y_e = rwkv_scan(to_e(kk), to_e(w), to_e(kka), to_e(k), to_e(r), to_e(v, value_rows=True))
    y_b = rwkv_out(from_chain_layout(y_e, bsz, n_heads_b, n_head_b), bonus, proj, offs["b_g"],
                   p["rwkv_gn_w"], p["rwkv_gn_b"], n_head_b)

    y_c = mlstm_branch(proj, proj_small, (offs["c_qk"], offs["c_v"], offs["c_o"], offs["c_g"], offs["c_if"], LANES),
                       bsz, s, p["mlstm_conv_w"], p["mlstm_conv_b"], p["mlstm_b_i"], p["mlstm_b_f"],
                       p["mlstm_gn_w"], w_c)

    memn = rmsnorm(memn_src, p["mem_norm_g"], BF16)
    kv = matmul(memn, stacked["xattn_w_kv"], layer, F32, 512, 512, name="mem_kv")
    y_x = xattn_branch(proj, offs["x_q"], offs["x_g"], bsz, s, kv, w_x)

    merged = merge((y_a, y_b, y_c, y_x),
                   tuple(stacked[n] for n in ("w_branch_a", "w_branch_b", "w_branch_c", "w_branch_x")),
                   layer, proj, offs["gates"], d)
    return matmul(merged, stacked["w_out"], layer, F32, 1024, 512, residual=x2d, name="out_proj")


_LAYER_PARAMS = (
    "norm_g", "mem_norm_g", "w_in", "lru_conv_w", "lru_conv_b", "lru_wa", "lru_ba", "lru_wx", "lru_bx",
    "lru_lambda", "rwkv_mu", "rwkv_w0", "rwkv_w_up", "rwkv_a0", "rwkv_a_up", "rwkv_k_k", "rwkv_k_a",
    "rwkv_r_k", "rwkv_gn_w", "rwkv_gn_b", "mlstm_conv_w", "mlstm_conv_b", "mlstm_b_i", "mlstm_b_f",
    "mlstm_gn_w", "xattn_w_kv", "w_branch_a", "w_branch_b", "w_branch_c", "w_branch_x", "w_out")


_STACKED_WEIGHTS = ("w_in", "xattn_w_kv", "w_branch_a", "w_branch_b", "w_branch_c", "w_branch_x", "w_out")


def kernel(x, mem, norm_g, mem_norm_g, w_in, lru_conv_w, lru_conv_b, lru_wa, lru_ba, lru_wx, lru_bx, lru_lambda, rwkv_mu, rwkv_w0, rwkv_w_up, rwkv_a0, rwkv_a_up, rwkv_k_k, rwkv_k_a, rwkv_r_k, rwkv_gn_w, rwkv_gn_b, mlstm_conv_w, mlstm_conv_b, mlstm_b_i, mlstm_b_f, mlstm_gn_w, xattn_w_kv, w_branch_a, w_branch_b, w_branch_c, w_branch_x, w_out, final_norm_g):
    stacked = dict(zip(_LAYER_PARAMS, (
        norm_g, mem_norm_g, w_in, lru_conv_w, lru_conv_b, lru_wa, lru_ba, lru_wx, lru_bx, lru_lambda,
        rwkv_mu, rwkv_w0, rwkv_w_up, rwkv_a0, rwkv_a_up, rwkv_k_k, rwkv_k_a, rwkv_r_k, rwkv_gn_w,
        rwkv_gn_b, mlstm_conv_w, mlstm_conv_b, mlstm_b_i, mlstm_b_f, mlstm_gn_w, xattn_w_kv,
        w_branch_a, w_branch_b, w_branch_c, w_branch_x, w_out)))
    bsz, s, d = x.shape
    x2d = x.reshape(bsz * s, d)
    mem2d = mem.reshape(-1, d)
    for layer in range(norm_g.shape[0]):
        params = {name: val[layer] for name, val in stacked.items() if name not in _STACKED_WEIGHTS}
        x2d = _layer(x2d, mem2d, bsz, s, params, stacked, layer)
    return rmsnorm(x2d, final_norm_g, x.dtype).reshape(bsz, s, d)
```

```python
import functools

import jax
import jax.numpy as jnp
import numpy as np
from jax import lax
from jax.experimental import pallas as pl
from jax.experimental.pallas import tpu as pltpu

F32 = jnp.float32
BF16 = jnp.bfloat16
HIGHEST = lax.Precision.HIGHEST

LANES = 128
SUBLANES = 8
VMEM_LIMIT = 56 * 1024 * 1024

RMS_EPS = 1e-6
LRU_C = 8.0
RWKV_GN_EPS = 64e-5
MLSTM_GN_EPS = 1e-6
DH_X = 128


def _cparams(*sem):
    return pltpu.CompilerParams(dimension_semantics=sem, vmem_limit_bytes=VMEM_LIMIT)


def _tile(n, want):
    t = min(n, want)
    assert n % t == 0, (n, want)
    return t


def _silu(x):
    return x * jax.nn.sigmoid(x)


def _rmsnorm_kernel(x_ref, g_ref, o_ref):
    x = x_ref[...]
    ms = jnp.mean(x * x, axis=-1, keepdims=True)
    o_ref[...] = (x * lax.rsqrt(ms + RMS_EPS) * g_ref[...]).astype(o_ref.dtype)


def rmsnorm(x2d, g, out_dtype):
    m, d = x2d.shape
    tm = _tile(m, 256)
    return pl.pallas_call(
        _rmsnorm_kernel,
        grid=(m // tm,),
        in_specs=[pl.BlockSpec((tm, d), lambda i: (i, 0)),
                  pl.BlockSpec((1, d), lambda i: (0, 0))],
        out_specs=pl.BlockSpec((tm, d), lambda i: (i, 0)),
        out_shape=jax.ShapeDtypeStruct((m, d), out_dtype),
        compiler_params=_cparams("parallel"),
        name="rmsnorm",
    )(x2d, g.reshape(1, d))


def _mm_kernel(a_ref, b_ref, o_ref):
    acc = jnp.dot(a_ref[...], b_ref[...].astype(BF16), preferred_element_type=F32)
    o_ref[...] = acc.astype(o_ref.dtype)


def _mm_res_kernel(a_ref, b_ref, r_ref, o_ref):
    acc = jnp.dot(a_ref[...], b_ref[...].astype(BF16), preferred_element_type=F32)
    o_ref[...] = (r_ref[...] + acc).astype(o_ref.dtype)


def matmul(a, b, layer, out_dtype, tm, tn, residual=None, name="matmul"):
    m, k = a.shape
    _, _, n = b.shape
    tm, tn = _tile(m, tm), _tile(n, tn)
    in_specs = [pl.BlockSpec((tm, k), lambda j, i: (i, 0)),
                pl.BlockSpec((None, k, tn), lambda j, i: (layer, 0, j))]
    args = [a, b]
    kern = _mm_kernel
    if residual is not None:
        in_specs.append(pl.BlockSpec((tm, tn), lambda j, i: (i, j)))
        args.append(residual)
        kern = _mm_res_kernel
    return pl.pallas_call(
        kern,
        grid=(n // tn, m // tm),
        in_specs=in_specs,
        out_specs=pl.BlockSpec((tm, tn), lambda j, i: (i, j)),
        out_shape=jax.ShapeDtypeStruct((m, n), out_dtype),
        compiler_params=_cparams("parallel", "parallel"),
        name=name,
    )(*args)


def _merge_kernel(ya_ref, yb_ref, yc_ref, yx_ref, wa_ref, wb_ref, wc_ref, wx_ref,
                  g0_ref, g1_ref, g2_ref, g3_ref, o_ref):
    def term(y_ref, w_ref, g_ref):
        return jax.nn.sigmoid(g_ref[...]) * jnp.dot(y_ref[...], w_ref[...], preferred_element_type=F32)

    acc = term(ya_ref, wa_ref, g0_ref)
    acc = acc + term(yb_ref, wb_ref, g1_ref)
    acc = acc + term(yc_ref, wc_ref, g2_ref)
    acc = acc + term(yx_ref, wx_ref, g3_ref)
    o_ref[...] = acc.astype(o_ref.dtype)


def merge(ys, ws, layer, proj, gate_off, d):
    m = proj.shape[0]
    tm, tn = _tile(m, 256), _tile(d, 1024)
    assert gate_off % tn == 0
    y_specs = [pl.BlockSpec((tm, y.shape[1]), lambda j, i: (i, 0)) for y in ys]
    w_specs = [pl.BlockSpec((None, w.shape[1], tn), lambda j, i: (layer, 0, j)) for w in ws]
    g_specs = [pl.BlockSpec((tm, tn), functools.partial(
        lambda j, i, base: (i, base + j), base=(gate_off + k * d) // tn)) for k in range(4)]
    return pl.pallas_call(
        _merge_kernel,
        grid=(d // tn, m // tm),
        in_specs=y_specs + w_specs + g_specs,
        out_specs=pl.BlockSpec((tm, tn), lambda j, i: (i, j)),
        out_shape=jax.ShapeDtypeStruct((m, d), BF16),
        compiler_params=_cparams("parallel", "parallel"),
        name="merge",
    )(*ys, *ws, proj, proj, proj, proj)


def _shift_rows(x, shift, hist):
    rolled = pltpu.roll(x, shift, axis=0)
    head = pltpu.roll(hist, shift, axis=0)
    row = lax.broadcasted_iota(jnp.int32, (SUBLANES, x.shape[1]), 0)
    fixed = jnp.where(row < shift, head, rolled[:SUBLANES])
    return jnp.concatenate([fixed, rolled[SUBLANES:]], axis=0)


def _causal_conv(x, hist, w, b):
    kw = w.shape[0]
    out = x * w[kw - 1:kw] + b
    for j in range(kw - 1):
        out = out + _shift_rows(x, kw - 1 - j, hist) * w[j:j + 1]
    return out


def _lru_kernel(ax_ref, ag_ref, cw_ref, cb_ref, wa_ref, ba_ref, wx_ref, bx_ref, lam_ref, o_ref,
                hist_sc, hcar_sc, a_sc, b_sc, h_sc):
    t = pl.program_id(2)

    @pl.when(t == 0)
    def _():
        hist_sc[...] = jnp.zeros_like(hist_sc)
        hcar_sc[...] = jnp.zeros_like(hcar_sc)

    x = ax_ref[...]
    rows = x.shape[0]
    u = _causal_conv(x, hist_sc[...], cw_ref[...], cb_ref[...])
    hist_sc[...] = x[rows - SUBLANES:]
    ub = u.astype(BF16)
    r = jax.nn.sigmoid(jnp.dot(ub, wa_ref[...], preferred_element_type=F32) + ba_ref[...])
    i = jax.nn.sigmoid(jnp.dot(ub, wx_ref[...], preferred_element_type=F32) + bx_ref[...])
    log_a = -LRU_C * r * jax.nn.softplus(-lam_ref[...])
    a = jnp.exp(log_a)
    b = jnp.sqrt(-jnp.tanh(log_a) * (a * a + 1.0)) * (i * u)

    in_group = lax.broadcasted_iota(jnp.int32, x.shape, 0) % SUBLANES
    for s in (1, 2, 4):
        keep = in_group >= s
        b = jnp.where(keep, a * pltpu.roll(b, s, axis=0) + b, b)
        a = jnp.where(keep, a * pltpu.roll(a, s, axis=0), a)
    a_sc[...] = a
    b_sc[...] = b

    def group(g, h_prev):
        sl = pl.ds(pl.multiple_of(g * SUBLANES, SUBLANES), SUBLANES)
        h8 = a_sc[sl, :] * h_prev + b_sc[sl, :]
        h_sc[sl, :] = h8
        return h8[SUBLANES - 1:SUBLANES]

    hcar_sc[...] = lax.fori_loop(0, rows // SUBLANES, group, hcar_sc[...], unroll=4)
    o_ref[...] = (h_sc[...] * _silu(ag_ref[...])).astype(o_ref.dtype)


def lru_branch(proj, off_x, off_g, bsz, s, cw, cb, wa, ba, wx, bx, lam):
    nb, bs, _ = wa.shape
    w = nb * bs
    tt = _tile(s, 512)
    nt = s // tt
    assert off_x % bs == 0 and off_g % bs == 0
    vec = lambda v: v.reshape(1, w)
    vspec = pl.BlockSpec((1, bs), lambda b, n, t: (0, n))
    return pl.pallas_call(
        _lru_kernel,
        grid=(bsz, nb, nt),
        in_specs=[
            pl.BlockSpec((tt, bs), lambda b, n, t: (b * nt + t, off_x // bs + n)),
            pl.BlockSpec((tt, bs), lambda b, n, t: (b * nt + t, off_g // bs + n)),
            pl.BlockSpec((cw.shape[0], bs), lambda b, n, t: (0, n)),
            vspec,
            pl.BlockSpec((None, bs, bs), lambda b, n, t: (n, 0, 0)),
            vspec,
            pl.BlockSpec((None, bs, bs), lambda b, n, t: (n, 0, 0)),
            vspec,
            vspec,
        ],
        out_specs=pl.BlockSpec((tt, bs), lambda b, n, t: (b * nt + t, n)),
        out_shape=jax.ShapeDtypeStruct((bsz * s, w), BF16),
        scratch_shapes=[pltpu.VMEM((SUBLANES, bs), F32), pltpu.VMEM((1, bs), F32),
                        pltpu.VMEM((tt, bs), F32), pltpu.VMEM((tt, bs), F32), pltpu.VMEM((tt, bs), F32)],
        compiler_params=_cparams("parallel", "parallel", "arbitrary"),
        name="rg_lru",
    )(proj, proj, cw, vec(cb), wa.astype(BF16), vec(ba), wx.astype(BF16), vec(bx), vec(lam))


def _seg_sum(x, g):
    parts = [jnp.dot(x[:, c * LANES:(c + 1) * LANES], g, precision=HIGHEST, preferred_element_type=F32)
             for c in range(x.shape[1] // LANES)]
    return parts[0] if len(parts) == 1 else jnp.concatenate(parts, axis=1)


def _rwkv_prep_kernel(pr_ref, pk_ref, pv_ref, plo_ref, mur_ref, muk_ref, muv_ref, mulo_ref,
                      w0_ref, wup_ref, a0_ref, aup_ref, kk_ref, ka_ref, rk_ref, g_ref,
                      r_o, w_o, k_o, v_o, kk_o, kka_o, bonus_o,
                      lr_sc, lk_sc, lv_sc, llo_sc):
    t = pl.program_id(1)

    @pl.when(t == 0)
    def _():
        for sc in (lr_sc, lk_sc, lv_sc, llo_sc):
            sc[...] = jnp.zeros_like(sc)

    def shifted(p_ref, mu_ref, last_sc):
        p = p_ref[...]
        rows = p.shape[0]
        prev = _shift_rows(p, 1, last_sc[...])
        last_sc[...] = p[rows - SUBLANES:]
        return p + (prev - p) * mu_ref[...]

    r = shifted(pr_ref, mur_ref, lr_sc)
    k = shifted(pk_ref, muk_ref, lk_sc)
    v = shifted(pv_ref, muv_ref, lv_sc)
    lo = shifted(plo_ref, mulo_ref, llo_sc)

    w_pre = w0_ref[...] + jnp.dot(jnp.tanh(lo).astype(BF16), wup_ref[...], preferred_element_type=F32)
    w_log = -jax.nn.softplus(-w_pre) - 0.5
    decay = jnp.exp(-jnp.exp(w_log))
    a = jax.nn.sigmoid(a0_ref[...] + jnp.dot(lo.astype(BF16), aup_ref[...], preferred_element_type=F32))

    g = g_ref[...]
    kk = k * kk_ref[...]
    kk = kk / jnp.maximum(jnp.sqrt(_seg_sum(kk * kk, g)), 1e-12)
    k2 = k * (1.0 + (a - 1.0) * ka_ref[...])
    r_o[...] = r
    w_o[...] = decay
    k_o[...] = k2
    v_o[...] = v
    kk_o[...] = kk
    kka_o[...] = kk * a
    bonus_o[...] = _seg_sum(r * k2 * rk_ref[...], g) * v


def rwkv_prep(proj, proj_lo, offs, bsz, s, mu, w0, w_up, a0, a_up, k_k, k_a, r_k, n_head):
    off_r, off_k, off_v, off_lo, lo_w = offs
    r_w, w = w_up.shape
    r_a = a_up.shape[0]
    tt = _tile(s, 256)
    nt = s // tt
    assert off_r % w == 0 and off_k % w == 0 and off_v % w == 0 and off_lo % lo_w == 0
    assert LANES % n_head == 0 and w % LANES == 0
    vec = lambda x: x.reshape(1, -1)
    mu_lo = jnp.pad(mu[3 * w:], (0, lo_w - r_w - r_a))
    wup_pad = jnp.zeros((lo_w, w), F32).at[:r_w].set(w_up).astype(BF16)
    aup_pad = jnp.zeros((lo_w, w), F32).at[r_w:r_w + r_a].set(a_up).astype(BF16)
    gmat = jnp.asarray(np.kron(np.eye(LANES // n_head), np.ones((n_head, n_head))), F32)
    row_w = pl.BlockSpec((1, w), lambda b, t: (0, 0))
    blk = lambda off, width: pl.BlockSpec((tt, width), lambda b, t: (b * nt + t, off // width))
    out_spec = pl.BlockSpec((tt, w), lambda b, t: (b * nt + t, 0))
    out_shape = jax.ShapeDtypeStruct((bsz * s, w), F32)
    return pl.pallas_call(
        _rwkv_prep_kernel,
        grid=(bsz, nt),
        in_specs=[blk(off_r, w), blk(off_k, w), blk(off_v, w), blk(off_lo, lo_w),
                  row_w, row_w, row_w, pl.BlockSpec((1, lo_w), lambda b, t: (0, 0)),
                  row_w, pl.BlockSpec((lo_w, w), lambda b, t: (0, 0)),
                  row_w, pl.BlockSpec((lo_w, w), lambda b, t: (0, 0)),
                  row_w, row_w, row_w,
                  pl.BlockSpec((LANES, LANES), lambda b, t: (0, 0))],
        out_specs=[out_spec] * 7,
        out_shape=[out_shape] * 7,
        scratch_shapes=[pltpu.VMEM((SUBLANES, w), F32)] * 3 + [pltpu.VMEM((SUBLANES, lo_w), F32)],
        compiler_params=_cparams("parallel", "arbitrary"),
        name="rwkv_prep",
    )(proj, proj, proj, proj_lo, vec(mu[:w]), vec(mu[w:2 * w]), vec(mu[2 * w:3 * w]), vec(mu_lo),
      vec(w0), wup_pad, vec(a0), aup_pad, vec(k_k), vec(k_a), vec(r_k), gmat)


def _rwkv_scan_kernel(kk_ref, w_ref, kka_ref, k_ref, r_ref, v_ref, y_ref, s_sc):
    c = pl.program_id(0)
    steps = kk_ref.shape[1]
    n_k, n_rows, _ = s_sc.shape
    blocks = [pl.ds(b * SUBLANES, SUBLANES) for b in range(n_rows // SUBLANES)]

    @pl.when(c == 0)
    def _():
        s_sc[...] = jnp.zeros_like(s_sc)

    def tree_sum(parts):
        while len(parts) > 1:
            parts = [parts[i] + parts[i + 1] for i in range(0, len(parts), 2)]
        return parts[0]

    sa0 = []
    for blk in blocks:
        acc = [None, None]
        for j in range(n_k):
            term = s_sc[j, blk, :] * kk_ref[j, 0:1, :]
            acc[j % 2] = term if acc[j % 2] is None else acc[j % 2] + term
        sa0.append(acc[0] + acc[1])

    def step(t, sa, with_next):
        v = [v_ref[t, blk, :] for blk in blocks]
        yacc = [[None, None] for _ in blocks]
        zacc = [[None, None] for _ in blocks]
        row = pl.ds(t, 1)
        for j in range(n_k):
            w_j, kka_j, k_j, r_j = w_ref[j, row, :], kka_ref[j, row, :], k_ref[j, row, :], r_ref[j, row, :]
            kk_j = kk_ref[j, pl.ds(t + 1, 1), :] if with_next else None
            for b, blk in enumerate(blocks):
                st = s_sc[j, blk, :] * w_j - sa[b] * kka_j + v[b] * k_j
                s_sc[j, blk, :] = st
                ya = st * r_j
                yacc[b][j % 2] = ya if yacc[b][j % 2] is None else yacc[b][j % 2] + ya
                if with_next:
                    za = st * kk_j
                    zacc[b][j % 2] = za if zacc[b][j % 2] is None else zacc[b][j % 2] + za
        for b, blk in enumerate(blocks):
            y_ref[t, blk, :] = tree_sum(yacc[b])
        return tuple(tree_sum(z) for z in zacc) if with_next else None

    sa = lax.fori_loop(0, steps - 1, lambda t, sa: step(t, sa, True), tuple(sa0))
    step(steps - 1, sa, False)


def rwkv_scan(kk_e, w_e, kka_e, k_e, r_e, v_e):
    n_k, s, lanes = kk_e.shape
    n_rows = v_e.shape[1]
    assert n_rows % SUBLANES == 0
    tc = _tile(s, 64)
    vec_spec = pl.BlockSpec((n_k, tc, lanes), lambda c: (0, c, 0))
    row_spec = pl.BlockSpec((tc, n_rows, lanes), lambda c: (c, 0, 0))
    return pl.pallas_call(
        _rwkv_scan_kernel,
        grid=(s // tc,),
        in_specs=[vec_spec] * 5 + [row_spec],
        out_specs=row_spec,
        out_shape=jax.ShapeDtypeStruct((s, n_rows, lanes), F32),
        scratch_shapes=[pltpu.VMEM((n_k, n_rows, lanes), F32)],
        compiler_params=_cparams("arbitrary"),
        name="rwkv_scan",
    )(kk_e, w_e, kka_e, k_e, r_e, v_e)


def _rwkv_out_kernel(y_ref, bonus_ref, gate_ref, gw_ref, gb_ref, g_ref, o_ref, *, n_head):
    y = y_ref[...]
    g = g_ref[...] * (1.0 / n_head)
    mu = _seg_sum(y, g)
    yc = y - mu
    var = _seg_sum(yc * yc, g)
    yn = yc * lax.rsqrt(var + RWKV_GN_EPS) * gw_ref[...] + gb_ref[...]
    o_ref[...] = ((yn + bonus_ref[...]) * _silu(gate_ref[...])).astype(o_ref.dtype)


def rwkv_out(y, bonus, proj, off_g, gn_w, gn_b, n_head):
    m, w = y.shape
    tt = _tile(m, 256)
    assert off_g % w == 0
    gmat = jnp.asarray(np.kron(np.eye(LANES // n_head), np.ones((n_head, n_head))), F32)
    blk = pl.BlockSpec((tt, w), lambda i: (i, 0))
    row = pl.BlockSpec((1, w), lambda i: (0, 0))
    return pl.pallas_call(
        functools.partial(_rwkv_out_kernel, n_head=n_head),
        grid=(m // tt,),
        in_specs=[blk, blk, pl.BlockSpec((tt, w), lambda i: (i, off_g // w)), row, row,
                  pl.BlockSpec((LANES, LANES), lambda i: (0, 0))],
        out_specs=blk,
        out_shape=jax.ShapeDtypeStruct((m, w), BF16),
        compiler_params=_cparams("parallel"),
        name="rwkv_out",
    )(y, bonus, proj, gn_w.reshape(1, w), gn_b.reshape(1, w), gmat)


def _to_chain_kernel(x_ref, o_ref, *, n_heads, n_head, dup, value_rows):
    bsz, t, _ = x_ref.shape
    parts = []
    for b in range(bsz):
        x3 = x_ref[b].T.reshape(n_heads, n_head, t)
        parts.append(jnp.transpose(x3, (1, 0, 2)))
    if value_rows:
        nr = n_head // dup
        e = jnp.concatenate([p[d * nr:(d + 1) * nr] for d in range(dup) for p in parts], axis=1)
        o_ref[...] = jnp.transpose(jnp.swapaxes(e, 1, 2), (1, 0, 2))
    else:
        e = jnp.concatenate(parts * dup, axis=1)
        o_ref[...] = jnp.swapaxes(e, 1, 2)


def to_chain_layout(x, bsz, s, n_heads, n_head, value_rows=False):
    w = n_heads * n_head
    assert LANES % (bsz * n_heads) == 0
    dup = LANES // (bsz * n_heads)
    assert n_head % dup == 0
    tt = _tile(s, LANES)
    if value_rows:
        out_shape, out_block, out_idx = (s, n_head // dup, LANES), (tt, n_head // dup, LANES), lambda i: (i, 0, 0)
    else:
        out_shape, out_block, out_idx = (n_head, s, LANES), (n_head, tt, LANES), lambda i: (0, i, 0)
    return pl.pallas_call(
        functools.partial(_to_chain_kernel, n_heads=n_heads, n_head=n_head, dup=dup, value_rows=value_rows),
        grid=(s // tt,),
        in_specs=[pl.BlockSpec((bsz, tt, w), lambda i: (0, i, 0))],
        out_specs=pl.BlockSpec(out_block, out_idx),
        out_shape=jax.ShapeDtypeStruct(out_shape, F32),
        compiler_params=_cparams("parallel"),
        name="to_chain_rows" if value_rows else "to_chain_keys",
    )(x.reshape(bsz, s, w))


def _from_chain_kernel(y_ref, o_ref, *, n_heads, n_head, dup):
    bsz = o_ref.shape[0]
    t = y_ref.shape[0]
    e = jnp.swapaxes(jnp.transpose(y_ref[...], (1, 0, 2)), 1, 2)
    bh = bsz * n_heads
    for b in range(bsz):
        rows = jnp.concatenate(
            [e[:, d * bh + b * n_heads:d * bh + (b + 1) * n_heads, :] for d in range(dup)], axis=0)
        o_ref[b] = jnp.transpose(rows, (1, 0, 2)).reshape(n_heads * n_head, t).T


def from_chain_layout(y_e, bsz, n_heads, n_head):
    s, n_rows, _ = y_e.shape
    w = n_heads * n_head
    dup = LANES // (bsz * n_heads)
    tt = _tile(s, LANES)
    return pl.pallas_call(
        functools.partial(_from_chain_kernel, n_heads=n_heads, n_head=n_head, dup=dup),
        grid=(s // tt,),
        in_specs=[pl.BlockSpec((tt, n_rows, LANES), lambda i: (i, 0, 0))],
        out_specs=pl.BlockSpec((bsz, tt, w), lambda i: (0, i, 0)),
        out_shape=jax.ShapeDtypeStruct((bsz, s, w), F32),
        compiler_params=_cparams("parallel"),
        name="from_chain_rows",
    )(y_e).reshape(bsz * s, w)


def _mlstm_kernel(qk_ref, v_ref, o_ref, g_ref, if_ref, cw_ref, cb_ref, bias_ref, gw_ref, tril_ref,
                  out_ref, hist_sc, c_sc, m_sc, *, n_heads, dqk, dv):
    t = pl.program_id(1)

    @pl.when(t == 0)
    def _():
        hist_sc[...] = jnp.zeros_like(hist_sc)
        c_sc[...] = jnp.zeros_like(c_sc)
        m_sc[...] = jnp.full_like(m_sc, -jnp.inf)

    qk_in = qk_ref[...]
    rows = qk_in.shape[0]
    qk = _silu(_causal_conv(qk_in, hist_sc[...], cw_ref[...], cb_ref[...]))
    hist_sc[...] = qk_in[rows - SUBLANES:]

    pre = if_ref[...] + bias_ref[...]
    li = pre
    lf = jax.nn.log_sigmoid(pltpu.roll(pre, LANES - n_heads, axis=1))
    tril = tril_ref[...]
    bcum = jnp.dot(tril, lf, precision=HIGHEST, preferred_element_type=F32)
    g_tot = bcum[rows - 1:rows]
    m_prev = m_sc[...]
    le = g_tot - bcum + li
    m_new = jnp.maximum(g_tot + m_prev, jnp.max(le, axis=0, keepdims=True))
    keep = jnp.exp(g_tot + m_prev - m_new)
    wk = jnp.exp(le - m_new)
    m_sc[...] = m_new
    bcum_t = bcum.T
    li_t = li.T
    causal = tril > 0.5
    ones_blk = (lax.broadcasted_iota(jnp.int32, (rows, LANES), 1) == 0).astype(F32)

    for h in range(n_heads):
        q = qk[:, h * dqk:(h + 1) * dqk]
        k = qk[:, (n_heads + h) * dqk:(n_heads + h + 1) * dqk] * (dqk ** -0.5)
        v = v_ref[:, h * dv:(h + 1) * dv]
        v_aug = jnp.concatenate([v, ones_blk], axis=1).astype(BF16)
        b_col = bcum[:, h:h + 1]
        dmat = b_col - bcum_t[h:h + 1, :] + li_t[h:h + 1, :]
        dmat = jnp.where(causal, dmat, -jnp.inf)
        inter = b_col + m_prev[:, h:h + 1]
        m_t = jnp.maximum(inter, jnp.max(dmat, axis=-1, keepdims=True))
        w_intra = jnp.exp(dmat - m_t)
        w_inter = jnp.exp(inter - m_t)
        qb = q.astype(BF16)
        sc = lax.dot_general(qb, k.astype(BF16), (((1,), (1,)), ((), ())),
                             preferred_element_type=F32) * w_intra
        c_aug = c_sc[h]
        tot = (w_inter * jnp.dot(qb, c_aug.astype(BF16), preferred_element_type=F32)
               + jnp.dot(sc.astype(BF16), v_aug, preferred_element_type=F32))
        num = tot[:, :dv]
        den = tot[:, dv:dv + 1]
        hh = num / jnp.maximum(jnp.abs(den), jnp.exp(-m_t))
        kw = (k * wk[:, h:h + 1]).astype(BF16)
        c_sc[h] = keep[:, h:h + 1] * c_aug + lax.dot_general(
            kw, v_aug, (((0,), (0,)), ((), ())), preferred_element_type=F32)

        ho = hh * jax.nn.sigmoid(o_ref[:, h * dv:(h + 1) * dv])
        mu = jnp.mean(ho, axis=-1, keepdims=True)
        hc = ho - mu
        var = jnp.mean(hc * hc, axis=-1, keepdims=True)
        hn = hc * lax.rsqrt(var + MLSTM_GN_EPS) * gw_ref[:, h * dv:(h + 1) * dv]
        out_ref[:, h * dv:(h + 1) * dv] = (hn * _silu(g_ref[:, h * dv:(h + 1) * dv])).astype(out_ref.dtype)


def mlstm_branch(proj, proj_if, offs, bsz, s, conv_w, conv_b, b_i, b_f, gn_w, w_c):
    off_qk, off_v, off_o, off_g, off_if, if_w = offs
    n_heads = b_i.shape[0]
    qk2 = conv_w.shape[1]
    dqk = qk2 // 2 // n_heads
    dv = w_c // n_heads
    ll = _tile(s, 256)
    nt = s // ll
    assert off_qk % qk2 == 0 and off_v % w_c == 0 and off_o % w_c == 0 and off_g % w_c == 0
    assert off_if % if_w == 0 and if_w == LANES and 2 * n_heads <= LANES
    bias = jnp.pad(jnp.concatenate([b_i, b_f]), (0, LANES - 2 * n_heads)).reshape(1, LANES)
    tril = jnp.asarray(np.tril(np.ones((ll, ll))), F32)
    blk = lambda off, width: pl.BlockSpec((ll, width), lambda b, t: (b * nt + t, off // width))
    full = lambda shape: pl.BlockSpec(shape, lambda b, t: (0,) * len(shape))
    return pl.pallas_call(
        functools.partial(_mlstm_kernel, n_heads=n_heads, dqk=dqk, dv=dv),
        grid=(bsz, nt),
        in_specs=[blk(off_qk, qk2), blk(off_v, w_c), blk(off_o, w_c), blk(off_g, w_c), blk(off_if, if_w),
                  full(conv_w.shape), full((1, qk2)), full((1, LANES)), full((1, w_c)), full((ll, ll))],
        out_specs=pl.BlockSpec((ll, w_c), lambda b, t: (b * nt + t, 0)),
        out_shape=jax.ShapeDtypeStruct((bsz * s, w_c), BF16),
        scratch_shapes=[pltpu.VMEM((SUBLANES, qk2), F32),
                        pltpu.VMEM((n_heads, dqk, dv + LANES), F32),
                        pltpu.VMEM((1, LANES), F32)],
        compiler_params=_cparams("parallel", "arbitrary"),
        name="mlstm",
    )(proj, proj, proj, proj, proj_if, conv_w, conv_b.reshape(1, qk2), bias, gn_w.reshape(1, w_c), tril)


def _xattn_kernel(q_ref, gate_ref, kv_ref, o_ref, *, n_heads):
    w_x = q_ref.shape[1]
    for h in range(n_heads):
        sl = slice(h * DH_X, (h + 1) * DH_X)
        q = q_ref[:, sl].astype(BF16)
        k = kv_ref[:, sl].astype(BF16)
        v = kv_ref[:, w_x + h * DH_X:w_x + (h + 1) * DH_X].astype(BF16)
        logits = lax.dot_general(q, k, (((1,), (1,)), ((), ())), preferred_element_type=F32) * (DH_X ** -0.5)
        e = jnp.exp(logits - jnp.max(logits, axis=-1, keepdims=True))
        probs = e / jnp.sum(e, axis=-1, keepdims=True)
        o = jnp.dot(probs.astype(BF16), v, preferred_element_type=F32)
        o_ref[:, sl] = (o * _silu(gate_ref[:, sl])).astype(o_ref.dtype)


def xattn_branch(proj, off_q, off_g, bsz, s, kv, w_x):
    m_len = kv.shape[0] // bsz
    tt = _tile(s, 512)
    nt = s // tt
    assert off_q % w_x == 0 and off_g % w_x == 0 and w_x % DH_X == 0
    return pl.pallas_call(
        functools.partial(_xattn_kernel, n_heads=w_x // DH_X),
        grid=(bsz, nt),
        in_specs=[pl.BlockSpec((tt, w_x), lambda b, t: (b * nt + t, off_q // w_x)),
                  pl.BlockSpec((tt, w_x), lambda b, t: (b * nt + t, off_g // w_x)),
                  pl.BlockSpec((m_len, 2 * w_x), lambda b, t: (b, 0))],
        out_specs=pl.BlockSpec((tt, w_x), lambda b, t: (b * nt + t, 0)),
        out_shape=jax.ShapeDtypeStruct((bsz * s, w_x), BF16),
        compiler_params=_cparams("parallel", "parallel"),
        name="mem_xattn",
    )(proj, proj, kv)


def _round_up(n, m):
    return (n + m - 1) // m * m


def _proj_layout(sizes, w_b, tn, tn_small):
    src = np.concatenate([[0], np.cumsum(sizes)])
    seg = {name: (int(src[i]), int(sizes[i])) for i, name in enumerate(
        ("a_x", "a_g", "b_s", "b_g", "c_qk", "c_v", "c_o", "c_g", "c_if", "x_q", "x_g", "gates"))}
    b0 = seg["b_s"][0]
    seg["b_r"], seg["b_k"], seg["b_v"] = (b0, w_b), (b0 + w_b, w_b), (b0 + 2 * w_b, w_b)
    seg["b_lo"] = (b0 + 3 * w_b, seg["b_s"][1] - 3 * w_b)
    assert seg["x_g"][0] == seg["x_q"][0] + seg["x_q"][1]
    seg["x_qg"] = (seg["x_q"][0], seg["x_q"][1] + seg["x_g"][1])
    offs, main_src, small_src = {}, [], []
    for name in ("a_x", "a_g", "b_g", "c_qk", "c_v", "c_o", "c_g", "b_r", "b_k", "b_v", "gates", "x_qg"):
        start, width = seg[name]
        assert start % SUBLANES == 0 and width % tn == 0, (name, start, width)
        offs[name] = len(main_src) * tn
        main_src.extend(start + b * tn for b in range(width // tn))
    offs["x_q"], offs["x_g"] = offs["x_qg"], offs["x_qg"] + seg["x_q"][1]
    for name in ("b_lo", "c_if"):
        start, width = seg[name]
        assert start % SUBLANES == 0 and width <= tn_small and start + tn_small <= src[-1]
        offs[name] = len(small_src) * tn_small
        small_src.append(start)
    return offs, main_src, small_src


def _in_proj_kernel(src_ref, h_ref, w_ref, o_ref):
    del src_ref
    o_ref[...] = lax.dot_general(h_ref[...], w_ref[...].astype(BF16), (((1,), (1,)), ((), ())),
                                 preferred_element_type=F32)


def in_proj(h, w_in_stacked, layer, block_src, tm, tn, name):
    m, k = h.shape
    n_layers, _, c_in = w_in_stacked.shape
    assert c_in % SUBLANES == 0
    w_t = jnp.swapaxes(w_in_stacked, 1, 2).reshape(n_layers * c_in, k)
    rows8 = jnp.asarray([(layer * c_in + c) // SUBLANES for c in block_src], jnp.int32)
    tm = _tile(m, tm)
    n_blocks = len(block_src)
    grid_spec = pltpu.PrefetchScalarGridSpec(
        num_scalar_prefetch=1,
        grid=(n_blocks, m // tm),
        in_specs=[pl.BlockSpec((tm, k), lambda j, i, src: (i, 0)),
                  pl.BlockSpec((pl.Element(tn), pl.Element(k)), lambda j, i, src: (src[j] * SUBLANES, 0))],
        out_specs=pl.BlockSpec((tm, tn), lambda j, i, src: (i, j)))
    return pl.pallas_call(
        _in_proj_kernel,
        grid_spec=grid_spec,
        out_shape=jax.ShapeDtypeStruct((m, n_blocks * tn), F32),
        compiler_params=_cparams("parallel", "parallel"),
        name=name,
    )(rows8, h, w_t)


def _layer(x2d, memn_src, bsz, s, p, stacked, layer):
    d = x2d.shape[1]
    w_a = p["lru_lambda"].shape[0]
    n_heads_b, n_head_b = p["rwkv_r_k"].shape
    w_b = n_heads_b * n_head_b
    w_c = stacked["w_branch_c"].shape[1]
    w_x = stacked["w_branch_x"].shape[1]
    qk2 = p["mlstm_conv_w"].shape[1]
    n_heads_c = p["mlstm_b_i"].shape[0]
    b_shift_w = p["rwkv_mu"].shape[0]
    sizes = (w_a, w_a, b_shift_w, w_b, qk2, w_c, w_c, w_c, 2 * n_heads_c, w_x, w_x, 4 * d)
    lo_w = _round_up(b_shift_w - 3 * w_b, 256)
    tn = min(1024, w_a, w_b, w_c, 2 * w_x, qk2)
    offs, main_src, small_src = _proj_layout(sizes, w_b, tn, lo_w)

    h = rmsnorm(x2d, p["norm_g"], BF16)
    proj = in_proj(h, stacked["w_in"], layer, main_src, 512, tn, "in_proj")
    proj_small = in_proj(h, stacked["w_in"], layer, small_src, 1024, lo_w, "in_proj_small")

    y_a = lru_branch(proj, offs["a_x"], offs["a_g"], bsz, s, p["lru_conv_w"], p["lru_conv_b"],
                     p["lru_wa"], p["lru_ba"], p["lru_wx"], p["lru_bx"], p["lru_lambda"])

    r, w, k, v, kk, kka, bonus = rwkv_prep(
        proj, proj_small, (offs["b_r"], offs["b_k"], offs["b_v"], offs["b_lo"], lo_w), bsz, s, p["rwkv_mu"],
        p["rwkv_w0"], p["rwkv_w_up"], p["rwkv_a0"], p["rwkv_a_up"], p["rwkv_k_k"], p["rwkv_k_a"],
        p["rwkv_r_k"], n_head_b)
    to_e = functools.partial(to_chain_layout, bsz=bsz, s=s, n_heads=n_heads_b, n_head=n_head_b)
    y_e = rwkv_scan(to_e(kk), to_e(w), to_e(kka), to_e(k), to_e(r), to_e(v, value_rows=True))
    y_b = rwkv_out(from_chain_layout(y_e, bsz, n_heads_b, n_head_b), bonus, proj, offs["b_g"],
                   p["rwkv_gn_w"], p["rwkv_gn_b"], n_head_b)

    y_c = mlstm_branch(proj, proj_small, (offs["c_qk"], offs["c_v"], offs["c_o"], offs["c_g"], offs["c_if"], LANES),
                       bsz, s, p["mlstm_conv_w"], p["mlstm_conv_b"], p["mlstm_b_i"], p["mlstm_b_f"],
                       p["mlstm_gn_w"], w_c)

    memn = rmsnorm(memn_src, p["mem_norm_g"], BF16)
    kv = matmul(memn, stacked["xattn_w_kv"], layer, F32, 512, 512, name="mem_kv")
    y_x = xattn_branch(proj, offs["x_q"], offs["x_g"], bsz, s, kv, w_x)

    merged = merge((y_a, y_b, y_c, y_x),
                   tuple(stacked[n].astype(BF16) for n in ("w_branch_a", "w_branch_b", "w_branch_c", "w_branch_x")),
                   layer, proj, offs["gates"], d)
    return matmul(merged, stacked["w_out"], layer, F32, 512, 1024, residual=x2d, name="out_proj")


_LAYER_PARAMS = (
    "norm_g", "mem_norm_g", "w_in", "lru_conv_w", "lru_conv_b", "lru_wa", "lru_ba", "lru_wx", "lru_bx",
    "lru_lambda", "rwkv_mu", "rwkv_w0", "rwkv_w_up", "rwkv_a0", "rwkv_a_up", "rwkv_k_k", "rwkv_k_a",
    "rwkv_r_k", "rwkv_gn_w", "rwkv_gn_b", "mlstm_conv_w", "mlstm_conv_b", "mlstm_b_i", "mlstm_b_f",
    "mlstm_gn_w", "xattn_w_kv", "w_branch_a", "w_branch_b", "w_branch_c", "w_branch_x", "w_out")


_STACKED_WEIGHTS = ("w_in", "xattn_w_kv", "w_branch_a", "w_branch_b", "w_branch_c", "w_branch_x", "w_out")


def kernel(x, mem, norm_g, mem_norm_g, w_in, lru_conv_w, lru_conv_b, lru_wa, lru_ba, lru_wx, lru_bx, lru_lambda, rwkv_mu, rwkv_w0, rwkv_w_up, rwkv_a0, rwkv_a_up, rwkv_k_k, rwkv_k_a, rwkv_r_k, rwkv_gn_w, rwkv_gn_b, mlstm_conv_w, mlstm_conv_b, mlstm_b_i, mlstm_b_f, mlstm_gn_w, xattn_w_kv, w_branch_a, w_branch_b, w_branch_c, w_branch_x, w_out, final_norm_g):
    stacked = dict(zip(_LAYER_PARAMS, (
        norm_g, mem_norm_g, w_in, lru_conv_w, lru_conv_b, lru_wa, lru_ba, lru_wx, lru_bx, lru_lambda,
        rwkv_mu, rwkv_w0, rwkv_w_up, rwkv_a0, rwkv_a_up, rwkv_k_k, rwkv_k_a, rwkv_r_k, rwkv_gn_w,
        rwkv_gn_b, mlstm_conv_w, mlstm_conv_b, mlstm_b_i, mlstm_b_f, mlstm_gn_w, xattn_w_kv,
        w_branch_a, w_branch_b, w_branch_c, w_branch_x, w_out)))
    bsz, s, d = x.shape
    x2d = x.reshape(bsz * s, d)
    mem2d = mem.reshape(-1, d)
    for layer in range(norm_g.shape[0]):
        params = {name: val[layer] for name, val in stacked.items() if name not in _STACKED_WEIGHTS}
        x2d = _layer(x2d, mem2d, bsz, s, params, stacked, layer)
    return rmsnorm(x2d, final_norm_g, x.dtype).reshape(bsz, s, d)
```

```python
import functools

import jax
import jax.numpy as jnp
import numpy as np
from jax import lax
from jax.experimental import pallas as pl
from jax.experimental.pallas import tpu as pltpu

F32 = jnp.float32
BF16 = jnp.bfloat16
HIGHEST = lax.Precision.HIGHEST

LANES = 128
SUBLANES = 8
VMEM_LIMIT = 56 * 1024 * 1024

RMS_EPS = 1e-6
LRU_C = 8.0
RWKV_GN_EPS = 64e-5
MLSTM_GN_EPS = 1e-6
DH_X = 128


def _cparams(*sem):
    return pltpu.CompilerParams(dimension_semantics=sem, vmem_limit_bytes=VMEM_LIMIT)


def _tile(n, want):
    t = min(n, want)
    assert n % t == 0, (n, want)
    return t


def _silu(x):
    return x * jax.nn.sigmoid(x)


def _rmsnorm_kernel(x_ref, g_ref, o_ref):
    x = x_ref[...]
    ms = jnp.mean(x * x, axis=-1, keepdims=True)
    o_ref[...] = (x * lax.rsqrt(ms + RMS_EPS) * g_ref[...]).astype(o_ref.dtype)


def rmsnorm(x2d, g, out_dtype):
    m, d = x2d.shape
    tm = _tile(m, 256)
    return pl.pallas_call(
        _rmsnorm_kernel,
        grid=(m // tm,),
        in_specs=[pl.BlockSpec((tm, d), lambda i: (i, 0)),
                  pl.BlockSpec((1, d), lambda i: (0, 0))],
        out_specs=pl.BlockSpec((tm, d), lambda i: (i, 0)),
        out_shape=jax.ShapeDtypeStruct((m, d), out_dtype),
        compiler_params=_cparams("parallel"),
        name="rmsnorm",
    )(x2d, g.reshape(1, d))


def _mm_kernel(a_ref, b_ref, o_ref):
    acc = jnp.dot(a_ref[...], b_ref[...].astype(BF16), preferred_element_type=F32)
    o_ref[...] = acc.astype(o_ref.dtype)


def _mm_res_kernel(a_ref, b_ref, r_ref, o_ref):
    acc = jnp.dot(a_ref[...], b_ref[...].astype(BF16), preferred_element_type=F32)
    o_ref[...] = (r_ref[...] + acc).astype(o_ref.dtype)


def matmul(a, b, layer, out_dtype, tm, tn, residual=None, name="matmul"):
    m, k = a.shape
    _, _, n = b.shape
    tm, tn = _tile(m, tm), _tile(n, tn)
    in_specs = [pl.BlockSpec((tm, k), lambda j, i: (i, 0)),
                pl.BlockSpec((None, k, tn), lambda j, i: (layer, 0, j))]
    args = [a, b]
    kern = _mm_kernel
    if residual is not None:
        in_specs.append(pl.BlockSpec((tm, tn), lambda j, i: (i, j)))
        args.append(residual)
        kern = _mm_res_kernel
    return pl.pallas_call(
        kern,
        grid=(n // tn, m // tm),
        in_specs=in_specs,
        out_specs=pl.BlockSpec((tm, tn), lambda j, i: (i, j)),
        out_shape=jax.ShapeDtypeStruct((m, n), out_dtype),
        compiler_params=_cparams("parallel", "parallel"),
        name=name,
    )(*args)


def _merge_kernel(ya_ref, yb_ref, yc_ref, yx_ref, wa_ref, wb_ref, wc_ref, wx_ref,
                  g0_ref, g1_ref, g2_ref, g3_ref, o_ref):
    def term(y_ref, w_ref, g_ref):
        return jax.nn.sigmoid(g_ref[...]) * jnp.dot(y_ref[...], w_ref[...], preferred_element_type=F32)

    acc = term(ya_ref, wa_ref, g0_ref)
    acc = acc + term(yb_ref, wb_ref, g1_ref)
    acc = acc + term(yc_ref, wc_ref, g2_ref)
    acc = acc + term(yx_ref, wx_ref, g3_ref)
    o_ref[...] = acc.astype(o_ref.dtype)


def merge(ys, ws, layer, proj, gate_off, d):
    m = proj.shape[0]
    tm, tn = _tile(m, 256), _tile(d, 1024)
    assert gate_off % tn == 0
    y_specs = [pl.BlockSpec((tm, y.shape[1]), lambda j, i: (i, 0)) for y in ys]
    w_specs = [pl.BlockSpec((None, w.shape[1], tn), lambda j, i: (layer, 0, j)) for w in ws]
    g_specs = [pl.BlockSpec((tm, tn), functools.partial(
        lambda j, i, base: (i, base + j), base=(gate_off + k * d) // tn)) for k in range(4)]
    return pl.pallas_call(
        _merge_kernel,
        grid=(d // tn, m // tm),
        in_specs=y_specs + w_specs + g_specs,
        out_specs=pl.BlockSpec((tm, tn), lambda j, i: (i, j)),
        out_shape=jax.ShapeDtypeStruct((m, d), BF16),
        compiler_params=_cparams("parallel", "parallel"),
        name="merge",
    )(*ys, *ws, proj, proj, proj, proj)


def _shift_rows(x, shift, hist):
    rolled = pltpu.roll(x, shift, axis=0)
    head = pltpu.roll(hist, shift, axis=0)
    row = lax.broadcasted_iota(jnp.int32, (SUBLANES, x.shape[1]), 0)
    fixed = jnp.where(row < shift, head, rolled[:SUBLANES])
    return jnp.concatenate([fixed, rolled[SUBLANES:]], axis=0)


def _causal_conv(x, hist, w, b):
    kw = w.shape[0]
    out = x * w[kw - 1:kw] + b
    for j in range(kw - 1):
        out = out + _shift_rows(x, kw - 1 - j, hist) * w[j:j + 1]
    return out


def _lru_kernel(ax_ref, ag_ref, cw_ref, cb_ref, wa_ref, ba_ref, wx_ref, bx_ref, lam_ref, o_ref,
                hist_sc, hcar_sc, a_sc, b_sc, h_sc):
    t = pl.program_id(2)

    @pl.when(t == 0)
    def _():
        hist_sc[...] = jnp.zeros_like(hist_sc)
        hcar_sc[...] = jnp.zeros_like(hcar_sc)

    x = ax_ref[...]
    rows = x.shape[0]
    u = _causal_conv(x, hist_sc[...], cw_ref[...], cb_ref[...])
    hist_sc[...] = x[rows - SUBLANES:]
    ub = u.astype(BF16)
    r = jax.nn.sigmoid(jnp.dot(ub, wa_ref[...], preferred_element_type=F32) + ba_ref[...])
    i = jax.nn.sigmoid(jnp.dot(ub, wx_ref[...], preferred_element_type=F32) + bx_ref[...])
    log_a = -LRU_C * r * jax.nn.softplus(-lam_ref[...])
    a = jnp.exp(log_a)
    b = jnp.sqrt(-jnp.tanh(log_a) * (a * a + 1.0)) * (i * u)

    in_group = lax.broadcasted_iota(jnp.int32, x.shape, 0) % SUBLANES
    for s in (1, 2, 4):
        keep = in_group >= s
        b = jnp.where(keep, a * pltpu.roll(b, s, axis=0) + b, b)
        a = jnp.where(keep, a * pltpu.roll(a, s, axis=0), a)
    a_sc[...] = a
    b_sc[...] = b

    def group(g, h_prev):
        sl = pl.ds(pl.multiple_of(g * SUBLANES, SUBLANES), SUBLANES)
        h8 = a_sc[sl, :] * h_prev + b_sc[sl, :]
        h_sc[sl, :] = h8
        return h8[SUBLANES - 1:SUBLANES]

    hcar_sc[...] = lax.fori_loop(0, rows // SUBLANES, group, hcar_sc[...], unroll=4)
    o_ref[...] = (h_sc[...] * _silu(ag_ref[...])).astype(o_ref.dtype)


def lru_branch(proj, off_x, off_g, bsz, s, cw, cb, wa, ba, wx, bx, lam):
    nb, bs, _ = wa.shape
    w = nb * bs
    tt = _tile(s, 512)
    nt = s // tt
    assert off_x % bs == 0 and off_g % bs == 0
    vec = lambda v: v.reshape(1, w)
    vspec = pl.BlockSpec((1, bs), lambda b, n, t: (0, n))
    return pl.pallas_call(
        _lru_kernel,
        grid=(bsz, nb, nt),
        in_specs=[
            pl.BlockSpec((tt, bs), lambda b, n, t: (b * nt + t, off_x // bs + n)),
            pl.BlockSpec((tt, bs), lambda b, n, t: (b * nt + t, off_g // bs + n)),
            pl.BlockSpec((cw.shape[0], bs), lambda b, n, t: (0, n)),
            vspec,
            pl.BlockSpec((None, bs, bs), lambda b, n, t: (n, 0, 0)),
            vspec,
            pl.BlockSpec((None, bs, bs), lambda b, n, t: (n, 0, 0)),
            vspec,
            vspec,
        ],
        out_specs=pl.BlockSpec((tt, bs), lambda b, n, t: (b * nt + t, n)),
        out_shape=jax.ShapeDtypeStruct((bsz * s, w), BF16),
        scratch_shapes=[pltpu.VMEM((SUBLANES, bs), F32), pltpu.VMEM((1, bs), F32),
                        pltpu.VMEM((tt, bs), F32), pltpu.VMEM((tt, bs), F32), pltpu.VMEM((tt, bs), F32)],
        compiler_params=_cparams("parallel", "parallel", "arbitrary"),
        name="rg_lru",
    )(proj, proj, cw, vec(cb), wa.astype(BF16), vec(ba), wx.astype(BF16), vec(bx), vec(lam))


def _seg_sum(x, g):
    parts = [jnp.dot(x[:, c * LANES:(c + 1) * LANES], g, precision=HIGHEST, preferred_element_type=F32)
             for c in range(x.shape[1] // LANES)]
    return parts[0] if len(parts) == 1 else jnp.concatenate(parts, axis=1)


def _token_shift(p_ref, mu_ref, last_sc, b):
    p = p_ref[...]
    rows = p.shape[0]
    prev = _shift_rows(p, 1, last_sc[b])
    last_sc[b] = p[rows - SUBLANES:]
    return p + (prev - p) * mu_ref[...]


def _heads_major(x, n_heads, n_head):
    return jnp.transpose(x.T.reshape(n_heads, n_head, x.shape[0]), (1, 0, 2))


def _chain_keys(per_batch, dup):
    return jnp.swapaxes(jnp.concatenate(per_batch * dup, axis=1), 1, 2)


def _chain_rows(per_batch, dup):
    nr = per_batch[0].shape[0] // dup
    e = jnp.concatenate([p[d * nr:(d + 1) * nr] for d in range(dup) for p in per_batch], axis=1)
    return jnp.transpose(jnp.swapaxes(e, 1, 2), (1, 0, 2))


def _rwkv_keys_kernel(*refs, bsz, n_heads, n_head, dup):
    pk, plo = refs[:bsz], refs[bsz:2 * bsz]
    muk_ref, mulo_ref, w0_ref, wup_ref, a0_ref, aup_ref, kkw_ref, g_ref = refs[2 * bsz:2 * bsz + 8]
    kk_o, w_o, kka_o, lk_sc, llo_sc = refs[2 * bsz + 8:]

    @pl.when(pl.program_id(0) == 0)
    def _():
        lk_sc[...] = jnp.zeros_like(lk_sc)
        llo_sc[...] = jnp.zeros_like(llo_sc)

    kk_b, w_b, kka_b = [], [], []
    for b in range(bsz):
        k = _token_shift(pk[b], muk_ref, lk_sc, b)
        lo = _token_shift(plo[b], mulo_ref, llo_sc, b)
        w_pre = w0_ref[...] + jnp.dot(jnp.tanh(lo).astype(BF16), wup_ref[...], preferred_element_type=F32)
        decay = jnp.exp(-jnp.exp(-jax.nn.softplus(-w_pre) - 0.5))
        a = jax.nn.sigmoid(a0_ref[...] + jnp.dot(lo.astype(BF16), aup_ref[...], preferred_element_type=F32))
        kk = k * kkw_ref[...]
        kk = kk / jnp.maximum(jnp.sqrt(_seg_sum(kk * kk, g_ref[...])), 1e-12)
        kk_b.append(_heads_major(kk, n_heads, n_head))
        w_b.append(_heads_major(decay, n_heads, n_head))
        kka_b.append(_heads_major(kk * a, n_heads, n_head))
    kk_o[...] = _chain_keys(kk_b, dup)
    w_o[...] = _chain_keys(w_b, dup)
    kka_o[...] = _chain_keys(kka_b, dup)


def _rwkv_vals_kernel(*refs, bsz, n_heads, n_head, dup):
    pr, pk, pv, plo = (refs[i * bsz:(i + 1) * bsz] for i in range(4))
    mur_ref, muk_ref, muv_ref, mulo_ref, a0_ref, aup_ref, ka_ref, rk_ref, g_ref = refs[4 * bsz:4 * bsz + 9]
    k_o, r_o, v_o, bonus_o, lr_sc, lk_sc, lv_sc, llo_sc = refs[4 * bsz + 9:]

    @pl.when(pl.program_id(0) == 0)
    def _():
        for sc in (lr_sc, lk_sc, lv_sc, llo_sc):
            sc[...] = jnp.zeros_like(sc)

    k_b, r_b, v_b = [], [], []
    for b in range(bsz):
        r = _token_shift(pr[b], mur_ref, lr_sc, b)
        k = _token_shift(pk[b], muk_ref, lk_sc, b)
        v = _token_shift(pv[b], muv_ref, lv_sc, b)
        lo = _token_shift(plo[b], mulo_ref, llo_sc, b)
        a = jax.nn.sigmoid(a0_ref[...] + jnp.dot(lo.astype(BF16), aup_ref[...], preferred_element_type=F32))
        k2 = k * (1.0 + (a - 1.0) * ka_ref[...])
        bonus_o[b] = _seg_sum(r * k2 * rk_ref[...], g_ref[...]) * v
        k_b.append(_heads_major(k2, n_heads, n_head))
        r_b.append(_heads_major(r, n_heads, n_head))
        v_b.append(_heads_major(v, n_heads, n_head))
    k_o[...] = _chain_keys(k_b, dup)
    r_o[...] = _chain_keys(r_b, dup)
    v_o[...] = _chain_rows(v_b, dup)


def rwkv_prep(proj, proj_lo, offs, bsz, s, mu, w0, w_up, a0, a_up, k_k, k_a, r_k, n_heads, n_head):
    off_r, off_k, off_v, off_lo, lo_w = offs
    r_w, w = w_up.shape
    r_a = a_up.shape[0]
    tt = _tile(s, LANES)
    nt = s // tt
    assert off_r % w == 0 and off_k % w == 0 and off_v % w == 0 and off_lo % lo_w == 0
    assert LANES % n_head == 0 and w % LANES == 0 and LANES % (bsz * n_heads) == 0
    dup = LANES // (bsz * n_heads)
    assert n_head % dup == 0
    vec = lambda x: x.reshape(1, -1)
    mu_r, mu_k, mu_v = vec(mu[:w]), vec(mu[w:2 * w]), vec(mu[2 * w:3 * w])
    mu_lo = vec(jnp.pad(mu[3 * w:], (0, lo_w - r_w - r_a)))
    wup_pad = jnp.zeros((lo_w, w), F32).at[:r_w].set(w_up).astype(BF16)
    aup_pad = jnp.zeros((lo_w, w), F32).at[r_w:r_w + r_a].set(a_up).astype(BF16)
    gmat = jnp.asarray(np.kron(np.eye(LANES // n_head), np.ones((n_head, n_head))), F32)

    row_w = pl.BlockSpec((1, w), lambda t: (0, 0))
    row_lo = pl.BlockSpec((1, lo_w), lambda t: (0, 0))
    lora = pl.BlockSpec((lo_w, w), lambda t: (0, 0))
    gspec = pl.BlockSpec((LANES, LANES), lambda t: (0, 0))

    def blk(off, width):
        return [pl.BlockSpec((tt, width), functools.partial(lambda t, b: (b * nt + t, off // width), b=b))
                for b in range(bsz)]

    keys_spec = pl.BlockSpec((n_head, tt, LANES), lambda t: (0, t, 0))
    keys_shape = jax.ShapeDtypeStruct((n_head, s, LANES), F32)
    hist = lambda width: pltpu.VMEM((bsz, SUBLANES, width), F32)
    statics = dict(bsz=bsz, n_heads=n_heads, n_head=n_head, dup=dup)

    kk_e, w_e, kka_e = pl.pallas_call(
        functools.partial(_rwkv_keys_kernel, **statics),
        grid=(nt,),
        in_specs=blk(off_k, w) + blk(off_lo, lo_w) + [row_w, row_lo, row_w, lora, row_w, lora, row_w, gspec],
        out_specs=[keys_spec] * 3,
        out_shape=[keys_shape] * 3,
        scratch_shapes=[hist(w), hist(lo_w)],
        compiler_params=_cparams("arbitrary"),
        name="rwkv_prep_keys",
    )(*[proj] * bsz, *[proj_lo] * bsz, mu_k, mu_lo, vec(w0), wup_pad, vec(a0), aup_pad, vec(k_k), gmat)

    k_e, r_e, v_e, bonus = pl.pallas_call(
        functools.partial(_rwkv_vals_kernel, **statics),
        grid=(nt,),
        in_specs=(blk(off_r, w) + blk(off_k, w) + blk(off_v, w) + blk(off_lo, lo_w)
                  + [row_w, row_w, row_w, row_lo, row_w, lora, row_w, row_w, gspec]),
        out_specs=[keys_spec, keys_spec,
                   pl.BlockSpec((tt, n_head // dup, LANES), lambda t: (t, 0, 0)),
                   pl.BlockSpec((bsz, tt, w), lambda t: (0, t, 0))],
        out_shape=[keys_shape, keys_shape,
                   jax.ShapeDtypeStruct((s, n_head // dup, LANES), F32),
                   jax.ShapeDtypeStruct((bsz, s, w), F32)],
        scratch_shapes=[hist(w), hist(w), hist(w), hist(lo_w)],
        compiler_params=_cparams("arbitrary"),
        name="rwkv_prep_vals",
    )(*[proj] * (3 * bsz), *[proj_lo] * bsz, mu_r, mu_k, mu_v, mu_lo, vec(a0), aup_pad, vec(k_a), vec(r_k), gmat)
    return kk_e, w_e, kka_e, k_e, r_e, v_e, bonus.reshape(bsz * s, w)


def _rwkv_scan_kernel(kk_ref, w_ref, kka_ref, k_ref, r_ref, v_ref, y_ref, s_sc):
    c = pl.program_id(0)
    steps = kk_ref.shape[1]
    n_k, n_rows, _ = s_sc.shape
    blocks = [pl.ds(b * SUBLANES, SUBLANES) for b in range(n_rows // SUBLANES)]

    @pl.when(c == 0)
    def _():
        s_sc[...] = jnp.zeros_like(s_sc)

    def tree_sum(parts):
        while len(parts) > 1:
            parts = [parts[i] + parts[i + 1] for i in range(0, len(parts), 2)]
        return parts[0]

    sa0 = []
    for blk in blocks:
        acc = [None, None]
        for j in range(n_k):
            term = s_sc[j, blk, :] * kk_ref[j, 0:1, :]
            acc[j % 2] = term if acc[j % 2] is None else acc[j % 2] + term
        sa0.append(acc[0] + acc[1])

    def step(t, sa, with_next):
        v = [v_ref[t, blk, :] for blk in blocks]
        yacc = [[None, None] for _ in blocks]
        zacc = [[None, None] for _ in blocks]
        row = pl.ds(t, 1)
        for j in range(n_k):
            w_j, kka_j, k_j, r_j = w_ref[j, row, :], kka_ref[j, row, :], k_ref[j, row, :], r_ref[j, row, :]
            kk_j = kk_ref[j, pl.ds(t + 1, 1), :] if with_next else None
            for b, blk in enumerate(blocks):
                st = s_sc[j, blk, :] * w_j - sa[b] * kka_j + v[b] * k_j
                s_sc[j, blk, :] = st
                ya = st * r_j
                yacc[b][j % 2] = ya if yacc[b][j % 2] is None else yacc[b][j % 2] + ya
                if with_next:
                    za = st * kk_j
                    zacc[b][j % 2] = za if zacc[b][j % 2] is None else zacc[b][j % 2] + za
        for b, blk in enumerate(blocks):
            y_ref[t, blk, :] = tree_sum(yacc[b])
        return tuple(tree_sum(z) for z in zacc) if with_next else None

    sa = lax.fori_loop(0, steps - 1, lambda t, sa: step(t, sa, True), tuple(sa0))
    step(steps - 1, sa, False)


def rwkv_scan(kk_e, w_e, kka_e, k_e, r_e, v_e):
    n_k, s, lanes = kk_e.shape
    n_rows = v_e.shape[1]
    assert n_rows % SUBLANES == 0
    tc = _tile(s, 64)
    vec_spec = pl.BlockSpec((n_k, tc, lanes), lambda c: (0, c, 0))
    row_spec = pl.BlockSpec((tc, n_rows, lanes), lambda c: (c, 0, 0))
    return pl.pallas_call(
        _rwkv_scan_kernel,
        grid=(s // tc,),
        in_specs=[vec_spec] * 5 + [row_spec],
        out_specs=row_spec,
        out_shape=jax.ShapeDtypeStruct((s, n_rows, lanes), F32),
        scratch_shapes=[pltpu.VMEM((n_k, n_rows, lanes), F32)],
        compiler_params=_cparams("arbitrary"),
        name="rwkv_scan",
    )(kk_e, w_e, kka_e, k_e, r_e, v_e)


def _rwkv_out_kernel(y_ref, bonus_ref, gate_ref, gw_ref, gb_ref, g_ref, o_ref, *, n_head):
    y = y_ref[...]
    g = g_ref[...] * (1.0 / n_head)
    mu = _seg_sum(y, g)
    yc = y - mu
    var = _seg_sum(yc * yc, g)
    yn = yc * lax.rsqrt(var + RWKV_GN_EPS) * gw_ref[...] + gb_ref[...]
    o_ref[...] = ((yn + bonus_ref[...]) * _silu(gate_ref[...])).astype(o_ref.dtype)


def rwkv_out(y, bonus, proj, off_g, gn_w, gn_b, n_head):
    m, w = y.shape
    tt = _tile(m, 256)
    assert off_g % w == 0
    gmat = jnp.asarray(np.kron(np.eye(LANES // n_head), np.ones((n_head, n_head))), F32)
    blk = pl.BlockSpec((tt, w), lambda i: (i, 0))
    row = pl.BlockSpec((1, w), lambda i: (0, 0))
    return pl.pallas_call(
        functools.partial(_rwkv_out_kernel, n_head=n_head),
        grid=(m // tt,),
        in_specs=[blk, blk, pl.BlockSpec((tt, w), lambda i: (i, off_g // w)), row, row,
                  pl.BlockSpec((LANES, LANES), lambda i: (0, 0))],
        out_specs=blk,
        out_shape=jax.ShapeDtypeStruct((m, w), BF16),
        compiler_params=_cparams("parallel"),
        name="rwkv_out",
    )(y, bonus, proj, gn_w.reshape(1, w), gn_b.reshape(1, w), gmat)


def _from_chain_kernel(y_ref, o_ref, *, n_heads, n_head, dup):
    bsz = o_ref.shape[0]
    t = y_ref.shape[0]
    e = jnp.swapaxes(jnp.transpose(y_ref[...], (1, 0, 2)), 1, 2)
    bh = bsz * n_heads
    for b in range(bsz):
        rows = jnp.concatenate(
            [e[:, d * bh + b * n_heads:d * bh + (b + 1) * n_heads, :] for d in range(dup)], axis=0)
        o_ref[b] = jnp.transpose(rows, (1, 0, 2)).reshape(n_heads * n_head, t).T


def from_chain_layout(y_e, bsz, n_heads, n_head):
    s, n_rows, _ = y_e.shape
    w = n_heads * n_head
    dup = LANES // (bsz * n_heads)
    tt = _tile(s, LANES)
    return pl.pallas_call(
        functools.partial(_from_chain_kernel, n_heads=n_heads, n_head=n_head, dup=dup),
        grid=(s // tt,),
        in_specs=[pl.BlockSpec((tt, n_rows, LANES), lambda i: (i, 0, 0))],
        out_specs=pl.BlockSpec((bsz, tt, w), lambda i: (0, i, 0)),
        out_shape=jax.ShapeDtypeStruct((bsz, s, w), F32),
        compiler_params=_cparams("parallel"),
        name="from_chain_rows",
    )(y_e).reshape(bsz * s, w)


def _mlstm_kernel(qk_ref, v_ref, o_ref, g_ref, if_ref, cw_ref, cb_ref, bias_ref, gw_ref, tril_ref,
                  out_ref, hist_sc, c_sc, m_sc, *, n_heads, dqk, dv):
    t = pl.program_id(1)

    @pl.when(t == 0)
    def _():
        hist_sc[...] = jnp.zeros_like(hist_sc)
        c_sc[...] = jnp.zeros_like(c_sc)
        m_sc[...] = jnp.full_like(m_sc, -jnp.inf)

    qk_in = qk_ref[...]
    rows = qk_in.shape[0]
    qk = _silu(_causal_conv(qk_in, hist_sc[...], cw_ref[...], cb_ref[...]))
    hist_sc[...] = qk_in[rows - SUBLANES:]

    pre = if_ref[...] + bias_ref[...]
    li = pre
    lf = jax.nn.log_sigmoid(pltpu.roll(pre, LANES - n_heads, axis=1))
    tril = tril_ref[...]
    bcum = jnp.dot(tril, lf, precision=HIGHEST, preferred_element_type=F32)
    g_tot = bcum[rows - 1:rows]
    m_prev = m_sc[...]
    le = g_tot - bcum + li
    m_new = jnp.maximum(g_tot + m_prev, jnp.max(le, axis=0, keepdims=True))
    keep = jnp.exp(g_tot + m_prev - m_new)
    wk = jnp.exp(le - m_new)
    m_sc[...] = m_new
    bcum_t = bcum.T
    li_t = li.T
    causal = tril > 0.5
    ones_blk = (lax.broadcasted_iota(jnp.int32, (rows, LANES), 1) == 0).astype(F32)

    for h in range(n_heads):
        q = qk[:, h * dqk:(h + 1) * dqk]
        k = qk[:, (n_heads + h) * dqk:(n_heads + h + 1) * dqk] * (dqk ** -0.5)
        v = v_ref[:, h * dv:(h + 1) * dv]
        v_aug = jnp.concatenate([v, ones_blk], axis=1).astype(BF16)
        b_col = bcum[:, h:h + 1]
        dmat = b_col - bcum_t[h:h + 1, :] + li_t[h:h + 1, :]
        dmat = jnp.where(causal, dmat, -jnp.inf)
        inter = b_col + m_prev[:, h:h + 1]
        m_t = jnp.maximum(inter, jnp.max(dmat, axis=-1, keepdims=True))
        w_intra = jnp.exp(dmat - m_t)
        w_inter = jnp.exp(inter - m_t)
        qb = q.astype(BF16)
        sc = lax.dot_general(qb, k.astype(BF16), (((1,), (1,)), ((), ())),
                             preferred_element_type=F32) * w_intra
        c_aug = c_sc[h]
        tot = (w_inter * jnp.dot(qb, c_aug.astype(BF16), preferred_element_type=F32)
               + jnp.dot(sc.astype(BF16), v_aug, preferred_element_type=F32))
        num = tot[:, :dv]
        den = tot[:, dv:dv + 1]
        hh = num / jnp.maximum(jnp.abs(den), jnp.exp(-m_t))
        kw = (k * wk[:, h:h + 1]).astype(BF16)
        c_sc[h] = keep[:, h:h + 1] * c_aug + lax.dot_general(
            kw, v_aug, (((0,), (0,)), ((), ())), preferred_element_type=F32)

        ho = hh * jax.nn.sigmoid(o_ref[:, h * dv:(h + 1) * dv])
        mu = jnp.mean(ho, axis=-1, keepdims=True)
        hc = ho - mu
        var = jnp.mean(hc * hc, axis=-1, keepdims=True)
        hn = hc * lax.rsqrt(var + MLSTM_GN_EPS) * gw_ref[:, h * dv:(h + 1) * dv]
        out_ref[:, h * dv:(h + 1) * dv] = (hn * _silu(g_ref[:, h * dv:(h + 1) * dv])).astype(out_ref.dtype)


def mlstm_branch(proj, proj_if, offs, bsz, s, conv_w, conv_b, b_i, b_f, gn_w, w_c):
    off_qk, off_v, off_o, off_g, off_if, if_w = offs
    n_heads = b_i.shape[0]
    qk2 = conv_w.shape[1]
    dqk = qk2 // 2 // n_heads
    dv = w_c // n_heads
    ll = _tile(s, 256)
    nt = s // ll
    assert off_qk % qk2 == 0 and off_v % w_c == 0 and off_o % w_c == 0 and off_g % w_c == 0
    assert off_if % if_w == 0 and if_w == LANES and 2 * n_heads <= LANES
    bias = jnp.pad(jnp.concatenate([b_i, b_f]), (0, LANES - 2 * n_heads)).reshape(1, LANES)
    tril = jnp.asarray(np.tril(np.ones((ll, ll))), F32)
    blk = lambda off, width: pl.BlockSpec((ll, width), lambda b, t: (b * nt + t, off // width))
    full = lambda shape: pl.BlockSpec(shape, lambda b, t: (0,) * len(shape))
    return pl.pallas_call(
        functools.partial(_mlstm_kernel, n_heads=n_heads, dqk=dqk, dv=dv),
        grid=(bsz, nt),
        in_specs=[blk(off_qk, qk2), blk(off_v, w_c), blk(off_o, w_c), blk(off_g, w_c), blk(off_if, if_w),
                  full(conv_w.shape), full((1, qk2)), full((1, LANES)), full((1, w_c)), full((ll, ll))],
        out_specs=pl.BlockSpec((ll, w_c), lambda b, t: (b * nt + t, 0)),
        out_shape=jax.ShapeDtypeStruct((bsz * s, w_c), BF16),
        scratch_shapes=[pltpu.VMEM((SUBLANES, qk2), F32),
                        pltpu.VMEM((n_heads, dqk, dv + LANES), F32),
                        pltpu.VMEM((1, LANES), F32)],
        compiler_params=_cparams("parallel", "arbitrary"),
        name="mlstm",
    )(proj, proj, proj, proj, proj_if, conv_w, conv_b.reshape(1, qk2), bias, gn_w.reshape(1, w_c), tril)


def _xattn_kernel(q_ref, gate_ref, kv_ref, o_ref, *, n_heads):
    w_x = q_ref.shape[1]
    for h in range(n_heads):
        sl = slice(h * DH_X, (h + 1) * DH_X)
        q = q_ref[:, sl].astype(BF16)
        k = kv_ref[:, sl].astype(BF16)
        v = kv_ref[:, w_x + h * DH_X:w_x + (h + 1) * DH_X].astype(BF16)
        logits = lax.dot_general(q, k, (((1,), (1,)), ((), ())), preferred_element_type=F32) * (DH_X ** -0.5)
        e = jnp.exp(logits - jnp.max(logits, axis=-1, keepdims=True))
        probs = e / jnp.sum(e, axis=-1, keepdims=True)
        o = jnp.dot(probs.astype(BF16), v, preferred_element_type=F32)
        o_ref[:, sl] = (o * _silu(gate_ref[:, sl])).astype(o_ref.dtype)


def xattn_branch(proj, off_q, off_g, bsz, s, kv, w_x):
    m_len = kv.shape[0] // bsz
    tt = _tile(s, 512)
    nt = s // tt
    assert off_q % w_x == 0 and off_g % w_x == 0 and w_x % DH_X == 0
    return pl.pallas_call(
        functools.partial(_xattn_kernel, n_heads=w_x // DH_X),
        grid=(bsz, nt),
        in_specs=[pl.BlockSpec((tt, w_x), lambda b, t: (b * nt + t, off_q // w_x)),
                  pl.BlockSpec((tt, w_x), lambda b, t: (b * nt + t, off_g // w_x)),
                  pl.BlockSpec((m_len, 2 * w_x), lambda b, t: (b, 0))],
        out_specs=pl.BlockSpec((tt, w_x), lambda b, t: (b * nt + t, 0)),
        out_shape=jax.ShapeDtypeStruct((bsz * s, w_x), BF16),
        compiler_params=_cparams("parallel", "parallel"),
        name="mem_xattn",
    )(proj, proj, kv)


def _round_up(n, m):
    return (n + m - 1) // m * m


def _proj_layout(sizes, w_b, tn, tn_small):
    src = np.concatenate([[0], np.cumsum(sizes)])
    seg = {name: (int(src[i]), int(sizes[i])) for i, name in enumerate(
        ("a_x", "a_g", "b_s", "b_g", "c_qk", "c_v", "c_o", "c_g", "c_if", "x_q", "x_g", "gates"))}
    b0 = seg["b_s"][0]
    seg["b_r"], seg["b_k"], seg["b_v"] = (b0, w_b), (b0 + w_b, w_b), (b0 + 2 * w_b, w_b)
    seg["b_lo"] = (b0 + 3 * w_b, seg["b_s"][1] - 3 * w_b)
    assert seg["x_g"][0] == seg["x_q"][0] + seg["x_q"][1]
    seg["x_qg"] = (seg["x_q"][0], seg["x_q"][1] + seg["x_g"][1])
    offs, main_src, small_src = {}, [], []
    for name in ("a_x", "a_g", "b_g", "c_qk", "c_v", "c_o", "c_g", "b_r", "b_k", "b_v", "gates", "x_qg"):
        start, width = seg[name]
        assert start % SUBLANES == 0 and width % tn == 0, (name, start, width)
        offs[name] = len(main_src) * tn
        main_src.extend(start + b * tn for b in range(width // tn))
    offs["x_q"], offs["x_g"] = offs["x_qg"], offs["x_qg"] + seg["x_q"][1]
    for name in ("b_lo", "c_if"):
        start, width = seg[name]
        assert start % SUBLANES == 0 and width <= tn_small and start + tn_small <= src[-1]
        offs[name] = len(small_src) * tn_small
        small_src.append(start)
    return offs, main_src, small_src


def _in_proj_kernel(src_ref, h_ref, w_ref, o_ref):
    del src_ref
    o_ref[...] = lax.dot_general(h_ref[...], w_ref[...].astype(BF16), (((1,), (1,)), ((), ())),
                                 preferred_element_type=F32)


def in_proj(h, w_in_stacked, layer, block_src, tm, tn, name):
    m, k = h.shape
    n_layers, _, c_in = w_in_stacked.shape
    assert c_in % SUBLANES == 0
    w_t = jnp.swapaxes(w_in_stacked, 1, 2).reshape(n_layers * c_in, k)
    rows8 = jnp.asarray([(layer * c_in + c) // SUBLANES for c in block_src], jnp.int32)
    tm = _tile(m, tm)
    n_blocks = len(block_src)
    grid_spec = pltpu.PrefetchScalarGridSpec(
        num_scalar_prefetch=1,
        grid=(n_blocks, m // tm),
        in_specs=[pl.BlockSpec((tm, k), lambda j, i, src: (i, 0)),
                  pl.BlockSpec((pl.Element(tn), pl.Element(k)), lambda j, i, src: (src[j] * SUBLANES, 0))],
        out_specs=pl.BlockSpec((tm, tn), lambda j, i, src: (i, j)))
    return pl.pallas_call(
        _in_proj_kernel,
        grid_spec=grid_spec,
        out_shape=jax.ShapeDtypeStruct((m, n_blocks * tn), F32),
        compiler_params=_cparams("parallel", "parallel"),
        name=name,
    )(rows8, h, w_t)


def _layer(x2d, memn_src, bsz, s, p, stacked, layer):
    d = x2d.shape[1]
    w_a = p["lru_lambda"].shape[0]
    n_heads_b, n_head_b = p["rwkv_r_k"].shape
    w_b = n_heads_b * n_head_b
    w_c = stacked["w_branch_c"].shape[1]
    w_x = stacked["w_branch_x"].shape[1]
    qk2 = p["mlstm_conv_w"].shape[1]
    n_heads_c = p["mlstm_b_i"].shape[0]
    b_shift_w = p["rwkv_mu"].shape[0]
    sizes = (w_a, w_a, b_shift_w, w_b, qk2, w_c, w_c, w_c, 2 * n_heads_c, w_x, w_x, 4 * d)
    lo_w = _round_up(b_shift_w - 3 * w_b, 256)
    tn = min(1024, w_a, w_b, w_c, 2 * w_x, qk2)
    offs, main_src, small_src = _proj_layout(sizes, w_b, tn, lo_w)

    h = rmsnorm(x2d, p["norm_g"], BF16)
    proj = in_proj(h, stacked["w_in"], layer, main_src, 512, tn, "in_proj")
    proj_small = in_proj(h, stacked["w_in"], layer, small_src, 1024, lo_w, "in_proj_small")

    y_a = lru_branch(proj, offs["a_x"], offs["a_g"], bsz, s, p["lru_conv_w"], p["lru_conv_b"],
                     p["lru_wa"], p["lru_ba"], p["lru_wx"], p["lru_bx"], p["lru_lambda"])

    kk_e, w_e, kka_e, k_e, r_e, v_e, bonus = rwkv_prep(
        proj, proj_small, (offs["b_r"], offs["b_k"], offs["b_v"], offs["b_lo"], lo_w), bsz, s, p["rwkv_mu"],
        p["rwkv_w0"], p["rwkv_w_up"], p["rwkv_a0"], p["rwkv_a_up"], p["rwkv_k_k"], p["rwkv_k_a"],
        p["rwkv_r_k"], n_heads_b, n_head_b)
    y_e = rwkv_scan(kk_e, w_e, kka_e, k_e, r_e, v_e)
    y_b = rwkv_out(from_chain_layout(y_e, bsz, n_heads_b, n_head_b), bonus, proj, offs["b_g"],
                   p["rwkv_gn_w"], p["rwkv_gn_b"], n_head_b)

    y_c = mlstm_branch(proj, proj_small, (offs["c_qk"], offs["c_v"], offs["c_o"], offs["c_g"], offs["c_if"], LANES),
                       bsz, s, p["mlstm_conv_w"], p["mlstm_conv_b"], p["mlstm_b_i"], p["mlstm_b_f"],
                       p["mlstm_gn_w"], w_c)

    memn = rmsnorm(memn_src, p["mem_norm_g"], BF16)
    kv = matmul(memn, stacked["xattn_w_kv"], layer, F32, 512, 512, name="mem_kv")
    y_x = xattn_branch(proj, offs["x_q"], offs["x_g"], bsz, s, kv, w_x)

    merged = merge((y_a, y_b, y_c, y_x),
                   tuple(stacked[n].astype(BF16) for n in ("w_branch_a", "w_branch_b", "w_branch_c", "w_branch_x")),
                   layer, proj, offs["gates"], d)
    return matmul(merged, stacked["w_out"], layer, F32, 512, 1024, residual=x2d, name="out_proj")


_LAYER_PARAMS = (
    "norm_g", "mem_norm_g", "w_in", "lru_conv_w", "lru_conv_b", "lru_wa", "lru_ba", "lru_wx", "lru_bx",
    "lru_lambda", "rwkv_mu", "rwkv_w0", "rwkv_w_up", "rwkv_a0", "rwkv_a_up", "rwkv_k_k", "rwkv_k_a",
    "rwkv_r_k", "rwkv_gn_w", "rwkv_gn_b", "mlstm_conv_w", "mlstm_conv_b", "mlstm_b_i", "mlstm_b_f",
    "mlstm_gn_w", "xattn_w_kv", "w_branch_a", "w_branch_b", "w_branch_c", "w_branch_x", "w_out")


_STACKED_WEIGHTS = ("w_in", "xattn_w_kv", "w_branch_a", "w_branch_b", "w_branch_c", "w_branch_x", "w_out")


def kernel(x, mem, norm_g, mem_norm_g, w_in, lru_conv_w, lru_conv_b, lru_wa, lru_ba, lru_wx, lru_bx, lru_lambda, rwkv_mu, rwkv_w0, rwkv_w_up, rwkv_a0, rwkv_a_up, rwkv_k_k, rwkv_k_a, rwkv_r_k, rwkv_gn_w, rwkv_gn_b, mlstm_conv_w, mlstm_conv_b, mlstm_b_i, mlstm_b_f, mlstm_gn_w, xattn_w_kv, w_branch_a, w_branch_b, w_branch_c, w_branch_x, w_out, final_norm_g):
    stacked = dict(zip(_LAYER_PARAMS, (
        norm_g, mem_norm_g, w_in, lru_conv_w, lru_conv_b, lru_wa, lru_ba, lru_wx, lru_bx, lru_lambda,
        rwkv_mu, rwkv_w0, rwkv_w_up, rwkv_a0, rwkv_a_up, rwkv_k_k, rwkv_k_a, rwkv_r_k, rwkv_gn_w,
        rwkv_gn_b, mlstm_conv_w, mlstm_conv_b, mlstm_b_i, mlstm_b_f, mlstm_gn_w, xattn_w_kv,
        w_branch_a, w_branch_b, w_branch_c, w_branch_x, w_out)))
    bsz, s, d = x.shape
    x2d = x.reshape(bsz * s, d)
    mem2d = mem.reshape(-1, d)
    for layer in range(norm_g.shape[0]):
        params = {name: val[layer] for name, val in stacked.items() if name not in _STACKED_WEIGHTS}
        x2d = _layer(x2d, mem2d, bsz, s, params, stacked, layer)
    return rmsnorm(x2d, final_norm_g, x.dtype).reshape(bsz, s, d)
```

```python
import functools

import jax
import jax.numpy as jnp
import numpy as np
from jax import lax
from jax.experimental import pallas as pl
from jax.experimental.pallas import tpu as pltpu

F32 = jnp.float32
BF16 = jnp.bfloat16
HIGHEST = lax.Precision.HIGHEST

LANES = 128
SUBLANES = 8
VMEM_LIMIT = 56 * 1024 * 1024

RMS_EPS = 1e-6
LRU_C = 8.0
RWKV_GN_EPS = 64e-5
MLSTM_GN_EPS = 1e-6
DH_X = 128


def _cparams(*sem):
    return pltpu.CompilerParams(dimension_semantics=sem, vmem_limit_bytes=VMEM_LIMIT)


def _tile(n, want):
    t = min(n, want)
    assert n % t == 0, (n, want)
    return t


def _silu(x):
    return x * jax.nn.sigmoid(x)


def _rmsnorm_kernel(x_ref, g_ref, o_ref):
    x = x_ref[...]
    ms = jnp.mean(x * x, axis=-1, keepdims=True)
    o_ref[...] = (x * lax.rsqrt(ms + RMS_EPS) * g_ref[...]).astype(o_ref.dtype)


def rmsnorm(x2d, g, out_dtype):
    m, d = x2d.shape
    tm = _tile(m, 256)
    return pl.pallas_call(
        _rmsnorm_kernel,
        grid=(m // tm,),
        in_specs=[pl.BlockSpec((tm, d), lambda i: (i, 0)),
                  pl.BlockSpec((1, d), lambda i: (0, 0))],
        out_specs=pl.BlockSpec((tm, d), lambda i: (i, 0)),
        out_shape=jax.ShapeDtypeStruct((m, d), out_dtype),
        compiler_params=_cparams("parallel"),
        name="rmsnorm",
    )(x2d, g.reshape(1, d))


def _mm_kernel(a_ref, b_ref, o_ref):
    acc = jnp.dot(a_ref[...], b_ref[...].astype(BF16), preferred_element_type=F32)
    o_ref[...] = acc.astype(o_ref.dtype)


def _mm_res_kernel(a_ref, b_ref, r_ref, o_ref):
    acc = jnp.dot(a_ref[...], b_ref[...].astype(BF16), preferred_element_type=F32)
    o_ref[...] = (r_ref[...] + acc).astype(o_ref.dtype)


def matmul(a, b, layer, out_dtype, tm, tn, residual=None, name="matmul"):
    m, k = a.shape
    _, _, n = b.shape
    tm, tn = _tile(m, tm), _tile(n, tn)
    in_specs = [pl.BlockSpec((tm, k), lambda j, i: (i, 0)),
                pl.BlockSpec((None, k, tn), lambda j, i: (layer, 0, j))]
    args = [a, b]
    kern = _mm_kernel
    if residual is not None:
        in_specs.append(pl.BlockSpec((tm, tn), lambda j, i: (i, j)))
        args.append(residual)
        kern = _mm_res_kernel
    return pl.pallas_call(
        kern,
        grid=(n // tn, m // tm),
        in_specs=in_specs,
        out_specs=pl.BlockSpec((tm, tn), lambda j, i: (i, j)),
        out_shape=jax.ShapeDtypeStruct((m, n), out_dtype),
        compiler_params=_cparams("parallel", "parallel"),
        name=name,
    )(*args)


def _merge_kernel(ya_ref, yb_ref, yc_ref, yx_ref, wa_ref, wb_ref, wc_ref, wx_ref,
                  g0_ref, g1_ref, g2_ref, g3_ref, o_ref):
    def term(y_ref, w_ref, g_ref):
        return jax.nn.sigmoid(g_ref[...]) * jnp.dot(y_ref[...], w_ref[...], preferred_element_type=F32)

    acc = term(ya_ref, wa_ref, g0_ref)
    acc = acc + term(yb_ref, wb_ref, g1_ref)
    acc = acc + term(yc_ref, wc_ref, g2_ref)
    acc = acc + term(yx_ref, wx_ref, g3_ref)
    o_ref[...] = acc.astype(o_ref.dtype)


def merge(ys, ws, layer, proj, gate_off, d):
    m = proj.shape[0]
    tm, tn = _tile(m, 256), _tile(d, 1024)
    assert gate_off % tn == 0
    y_specs = [pl.BlockSpec((tm, y.shape[1]), lambda j, i: (i, 0)) for y in ys]
    w_specs = [pl.BlockSpec((None, w.shape[1], tn), lambda j, i: (layer, 0, j)) for w in ws]
    g_specs = [pl.BlockSpec((tm, tn), functools.partial(
        lambda j, i, base: (i, base + j), base=(gate_off + k * d) // tn)) for k in range(4)]
    return pl.pallas_call(
        _merge_kernel,
        grid=(d // tn, m // tm),
        in_specs=y_specs + w_specs + g_specs,
        out_specs=pl.BlockSpec((tm, tn), lambda j, i: (i, j)),
        out_shape=jax.ShapeDtypeStruct((m, d), BF16),
        compiler_params=_cparams("parallel", "parallel"),
        name="merge",
    )(*ys, *ws, proj, proj, proj, proj)


def _shift_rows(x, shift, hist):
    rolled = pltpu.roll(x, shift, axis=0)
    head = pltpu.roll(hist, shift, axis=0)
    row = lax.broadcasted_iota(jnp.int32, (SUBLANES, x.shape[1]), 0)
    fixed = jnp.where(row < shift, head, rolled[:SUBLANES])
    return jnp.concatenate([fixed, rolled[SUBLANES:]], axis=0)


def _causal_conv(x, hist, w, b):
    kw = w.shape[0]
    out = x * w[kw - 1:kw] + b
    for j in range(kw - 1):
        out = out + _shift_rows(x, kw - 1 - j, hist) * w[j:j + 1]
    return out


def _lru_kernel(ax_ref, ag_ref, cw_ref, cb_ref, wa_ref, ba_ref, wx_ref, bx_ref, lam_ref, o_ref,
                hist_sc, hcar_sc, a_sc, b_sc, h_sc):
    t = pl.program_id(2)

    @pl.when(t == 0)
    def _():
        hist_sc[...] = jnp.zeros_like(hist_sc)
        hcar_sc[...] = jnp.zeros_like(hcar_sc)

    x = ax_ref[...]
    rows = x.shape[0]
    u = _causal_conv(x, hist_sc[...], cw_ref[...], cb_ref[...])
    hist_sc[...] = x[rows - SUBLANES:]
    ub = u.astype(BF16)
    r = jax.nn.sigmoid(jnp.dot(ub, wa_ref[...], preferred_element_type=F32) + ba_ref[...])
    i = jax.nn.sigmoid(jnp.dot(ub, wx_ref[...], preferred_element_type=F32) + bx_ref[...])
    log_a = -LRU_C * r * jax.nn.softplus(-lam_ref[...])
    a = jnp.exp(log_a)
    b = jnp.sqrt(-jnp.tanh(log_a) * (a * a + 1.0)) * (i * u)

    in_group = lax.broadcasted_iota(jnp.int32, x.shape, 0) % SUBLANES
    for s in (1, 2, 4):
        keep = in_group >= s
        b = jnp.where(keep, a * pltpu.roll(b, s, axis=0) + b, b)
        a = jnp.where(keep, a * pltpu.roll(a, s, axis=0), a)
    a_sc[...] = a
    b_sc[...] = b

    def group(g, h_prev):
        sl = pl.ds(pl.multiple_of(g * SUBLANES, SUBLANES), SUBLANES)
        h8 = a_sc[sl, :] * h_prev + b_sc[sl, :]
        h_sc[sl, :] = h8
        return h8[SUBLANES - 1:SUBLANES]

    hcar_sc[...] = lax.fori_loop(0, rows // SUBLANES, group, hcar_sc[...], unroll=4)
    o_ref[...] = (h_sc[...] * _silu(ag_ref[...])).astype(o_ref.dtype)


def lru_branch(proj, off_x, off_g, bsz, s, cw, cb, wa, ba, wx, bx, lam):
    nb, bs, _ = wa.shape
    w = nb * bs
    tt = _tile(s, 512)
    nt = s // tt
    assert off_x % bs == 0 and off_g % bs == 0
    vec = lambda v: v.reshape(1, w)
    vspec = pl.BlockSpec((1, bs), lambda b, n, t: (0, n))
    return pl.pallas_call(
        _lru_kernel,
        grid=(bsz, nb, nt),
        in_specs=[
            pl.BlockSpec((tt, bs), lambda b, n, t: (b * nt + t, off_x // bs + n)),
            pl.BlockSpec((tt, bs), lambda b, n, t: (b * nt + t, off_g // bs + n)),
            pl.BlockSpec((cw.shape[0], bs), lambda b, n, t: (0, n)),
            vspec,
            pl.BlockSpec((None, bs, bs), lambda b, n, t: (n, 0, 0)),
            vspec,
            pl.BlockSpec((None, bs, bs), lambda b, n, t: (n, 0, 0)),
            vspec,
            vspec,
        ],
        out_specs=pl.BlockSpec((tt, bs), lambda b, n, t: (b * nt + t, n)),
        out_shape=jax.ShapeDtypeStruct((bsz * s, w), BF16),
        scratch_shapes=[pltpu.VMEM((SUBLANES, bs), F32), pltpu.VMEM((1, bs), F32),
                        pltpu.VMEM((tt, bs), F32), pltpu.VMEM((tt, bs), F32), pltpu.VMEM((tt, bs), F32)],
        compiler_params=_cparams("parallel", "parallel", "arbitrary"),
        name="rg_lru",
    )(proj, proj, cw, vec(cb), wa.astype(BF16), vec(ba), wx.astype(BF16), vec(bx), vec(lam))


def _seg_sum(x, g):
    parts = [jnp.dot(x[:, c * LANES:(c + 1) * LANES], g, precision=HIGHEST, preferred_element_type=F32)
             for c in range(x.shape[1] // LANES)]
    return parts[0] if len(parts) == 1 else jnp.concatenate(parts, axis=1)


def _token_shift(p_ref, mu_ref, last_sc, b):
    p = p_ref[...]
    rows = p.shape[0]
    prev = _shift_rows(p, 1, last_sc[b])
    last_sc[b] = p[rows - SUBLANES:]
    return p + (prev - p) * mu_ref[...]


def _heads_major(x, n_heads, n_head):
    return jnp.transpose(x.T.reshape(n_heads, n_head, x.shape[0]), (1, 0, 2))


def _chain_keys(per_batch, dup):
    return jnp.swapaxes(jnp.concatenate(per_batch * dup, axis=1), 1, 2)


def _chain_rows(per_batch, dup):
    nr = per_batch[0].shape[0] // dup
    e = jnp.concatenate([p[d * nr:(d + 1) * nr] for d in range(dup) for p in per_batch], axis=1)
    return jnp.transpose(jnp.swapaxes(e, 1, 2), (1, 0, 2))


def _rwkv_keys_kernel(*refs, bsz, n_heads, n_head, dup):
    pk, plo = refs[:bsz], refs[bsz:2 * bsz]
    muk_ref, mulo_ref, w0_ref, wup_ref, a0_ref, aup_ref, kkw_ref, g_ref = refs[2 * bsz:2 * bsz + 8]
    kk_o, w_o, kka_o, lk_sc, llo_sc = refs[2 * bsz + 8:]

    @pl.when(pl.program_id(0) == 0)
    def _():
        lk_sc[...] = jnp.zeros_like(lk_sc)
        llo_sc[...] = jnp.zeros_like(llo_sc)

    kk_b, w_b, kka_b = [], [], []
    for b in range(bsz):
        k = _token_shift(pk[b], muk_ref, lk_sc, b)
        lo = _token_shift(plo[b], mulo_ref, llo_sc, b)
        w_pre = w0_ref[...] + jnp.dot(jnp.tanh(lo).astype(BF16), wup_ref[...], preferred_element_type=F32)
        decay = jnp.exp(-jnp.exp(-jax.nn.softplus(-w_pre) - 0.5))
        a = jax.nn.sigmoid(a0_ref[...] + jnp.dot(lo.astype(BF16), aup_ref[...], preferred_element_type=F32))
        kk = k * kkw_ref[...]
        kk = kk / jnp.maximum(jnp.sqrt(_seg_sum(kk * kk, g_ref[...])), 1e-12)
        kk_b.append(_heads_major(kk, n_heads, n_head))
        w_b.append(_heads_major(decay, n_heads, n_head))
        kka_b.append(_heads_major(kk * a, n_heads, n_head))
    kk_o[...] = _chain_keys(kk_b, dup)
    w_o[...] = _chain_keys(w_b, dup)
    kka_o[...] = _chain_keys(kka_b, dup)


def _rwkv_vals_kernel(*refs, bsz, n_heads, n_head, dup):
    pr, pk, pv, plo = (refs[i * bsz:(i + 1) * bsz] for i in range(4))
    mur_ref, muk_ref, muv_ref, mulo_ref, a0_ref, aup_ref, ka_ref, rk_ref, g_ref = refs[4 * bsz:4 * bsz + 9]
    k_o, r_o, v_o, bonus_o, lr_sc, lk_sc, lv_sc, llo_sc = refs[4 * bsz + 9:]

    @pl.when(pl.program_id(0) == 0)
    def _():
        for sc in (lr_sc, lk_sc, lv_sc, llo_sc):
            sc[...] = jnp.zeros_like(sc)

    k_b, r_b, v_b = [], [], []
    for b in range(bsz):
        r = _token_shift(pr[b], mur_ref, lr_sc, b)
        k = _token_shift(pk[b], muk_ref, lk_sc, b)
        v = _token_shift(pv[b], muv_ref, lv_sc, b)
        lo = _token_shift(plo[b], mulo_ref, llo_sc, b)
        a = jax.nn.sigmoid(a0_ref[...] + jnp.dot(lo.astype(BF16), aup_ref[...], preferred_element_type=F32))
        k2 = k * (1.0 + (a - 1.0) * ka_ref[...])
        bonus_o[b] = _seg_sum(r * k2 * rk_ref[...], g_ref[...]) * v
        k_b.append(_heads_major(k2, n_heads, n_head))
        r_b.append(_heads_major(r, n_heads, n_head))
        v_b.append(_heads_major(v, n_heads, n_head))
    k_o[...] = _chain_keys(k_b, dup)
    r_o[...] = _chain_keys(r_b, dup)
    v_o[...] = _chain_rows(v_b, dup)


def rwkv_prep(proj, proj_lo, offs, bsz, s, mu, w0, w_up, a0, a_up, k_k, k_a, r_k, n_heads, n_head):
    off_r, off_k, off_v, off_lo, lo_w = offs
    r_w, w = w_up.shape
    r_a = a_up.shape[0]
    tt = _tile(s, LANES)
    nt = s // tt
    assert off_r % w == 0 and off_k % w == 0 and off_v % w == 0 and off_lo % lo_w == 0
    assert LANES % n_head == 0 and w % LANES == 0 and LANES % (bsz * n_heads) == 0
    dup = LANES // (bsz * n_heads)
    assert n_head % dup == 0
    vec = lambda x: x.reshape(1, -1)
    mu_r, mu_k, mu_v = vec(mu[:w]), vec(mu[w:2 * w]), vec(mu[2 * w:3 * w])
    mu_lo = vec(jnp.pad(mu[3 * w:], (0, lo_w - r_w - r_a)))
    wup_pad = jnp.zeros((lo_w, w), F32).at[:r_w].set(w_up).astype(BF16)
    aup_pad = jnp.zeros((lo_w, w), F32).at[r_w:r_w + r_a].set(a_up).astype(BF16)
    gmat = jnp.asarray(np.kron(np.eye(LANES // n_head), np.ones((n_head, n_head))), F32)

    row_w = pl.BlockSpec((1, w), lambda t: (0, 0))
    row_lo = pl.BlockSpec((1, lo_w), lambda t: (0, 0))
    lora = pl.BlockSpec((lo_w, w), lambda t: (0, 0))
    gspec = pl.BlockSpec((LANES, LANES), lambda t: (0, 0))

    def blk(off, width):
        return [pl.BlockSpec((tt, width), functools.partial(lambda t, b: (b * nt + t, off // width), b=b))
                for b in range(bsz)]

    keys_spec = pl.BlockSpec((n_head, tt, LANES), lambda t: (0, t, 0))
    keys_shape = jax.ShapeDtypeStruct((n_head, s, LANES), F32)
    hist = lambda width: pltpu.VMEM((bsz, SUBLANES, width), F32)
    statics = dict(bsz=bsz, n_heads=n_heads, n_head=n_head, dup=dup)

    kk_e, w_e, kka_e = pl.pallas_call(
        functools.partial(_rwkv_keys_kernel, **statics),
        grid=(nt,),
        in_specs=blk(off_k, w) + blk(off_lo, lo_w) + [row_w, row_lo, row_w, lora, row_w, lora, row_w, gspec],
        out_specs=[keys_spec] * 3,
        out_shape=[keys_shape] * 3,
        scratch_shapes=[hist(w), hist(lo_w)],
        compiler_params=_cparams("arbitrary"),
        name="rwkv_prep_keys",
    )(*[proj] * bsz, *[proj_lo] * bsz, mu_k, mu_lo, vec(w0), wup_pad, vec(a0), aup_pad, vec(k_k), gmat)

    k_e, r_e, v_e, bonus = pl.pallas_call(
        functools.partial(_rwkv_vals_kernel, **statics),
        grid=(nt,),
        in_specs=(blk(off_r, w) + blk(off_k, w) + blk(off_v, w) + blk(off_lo, lo_w)
                  + [row_w, row_w, row_w, row_lo, row_w, lora, row_w, row_w, gspec]),
        out_specs=[keys_spec, keys_spec,
                   pl.BlockSpec((tt, n_head // dup, LANES), lambda t: (t, 0, 0)),
                   pl.BlockSpec((bsz, tt, w), lambda t: (0, t, 0))],
        out_shape=[keys_shape, keys_shape,
                   jax.ShapeDtypeStruct((s, n_head // dup, LANES), F32),
                   jax.ShapeDtypeStruct((bsz, s, w), F32)],
        scratch_shapes=[hist(w), hist(w), hist(w), hist(lo_w)],
        compiler_params=_cparams("arbitrary"),
        name="rwkv_prep_vals",
    )(*[proj] * (3 * bsz), *[proj_lo] * bsz, mu_r, mu_k, mu_v, mu_lo, vec(a0), aup_pad, vec(k_a), vec(r_k), gmat)
    return kk_e, w_e, kka_e, k_e, r_e, v_e, bonus


def _rwkv_scan_kernel(kk_ref, w_ref, kka_ref, k_ref, r_ref, v_ref, y_ref, s_sc):
    c = pl.program_id(0)
    steps = kk_ref.shape[1]
    n_k, n_rows, _ = s_sc.shape
    blocks = [pl.ds(b * SUBLANES, SUBLANES) for b in range(n_rows // SUBLANES)]

    @pl.when(c == 0)
    def _():
        s_sc[...] = jnp.zeros_like(s_sc)

    def tree_sum(parts):
        while len(parts) > 1:
            parts = [parts[i] + parts[i + 1] for i in range(0, len(parts), 2)]
        return parts[0]

    sa0 = []
    for blk in blocks:
        acc = [None, None]
        for j in range(n_k):
            term = s_sc[j, blk, :] * kk_ref[j, 0:1, :]
            acc[j % 2] = term if acc[j % 2] is None else acc[j % 2] + term
        sa0.append(acc[0] + acc[1])

    def step(t, sa, with_next):
        v = [v_ref[t, blk, :] for blk in blocks]
        yacc = [[None, None] for _ in blocks]
        zacc = [[None, None] for _ in blocks]
        row = pl.ds(t, 1)
        for j in range(n_k):
            w_j, kka_j, k_j, r_j = w_ref[j, row, :], kka_ref[j, row, :], k_ref[j, row, :], r_ref[j, row, :]
            kk_j = kk_ref[j, pl.ds(t + 1, 1), :] if with_next else None
            for b, blk in enumerate(blocks):
                st = s_sc[j, blk, :] * w_j - sa[b] * kka_j + v[b] * k_j
                s_sc[j, blk, :] = st
                ya = st * r_j
                yacc[b][j % 2] = ya if yacc[b][j % 2] is None else yacc[b][j % 2] + ya
                if with_next:
                    za = st * kk_j
                    zacc[b][j % 2] = za if zacc[b][j % 2] is None else zacc[b][j % 2] + za
        for b, blk in enumerate(blocks):
            y_ref[t, blk, :] = tree_sum(yacc[b])
        return tuple(tree_sum(z) for z in zacc) if with_next else None

    sa = lax.fori_loop(0, steps - 1, lambda t, sa: step(t, sa, True), tuple(sa0))
    step(steps - 1, sa, False)


def rwkv_scan(kk_e, w_e, kka_e, k_e, r_e, v_e):
    n_k, s, lanes = kk_e.shape
    n_rows = v_e.shape[1]
    assert n_rows % SUBLANES == 0
    tc = _tile(s, 64)
    vec_spec = pl.BlockSpec((n_k, tc, lanes), lambda c: (0, c, 0))
    row_spec = pl.BlockSpec((tc, n_rows, lanes), lambda c: (c, 0, 0))
    return pl.pallas_call(
        _rwkv_scan_kernel,
        grid=(s // tc,),
        in_specs=[vec_spec] * 5 + [row_spec],
        out_specs=row_spec,
        out_shape=jax.ShapeDtypeStruct((s, n_rows, lanes), F32),
        scratch_shapes=[pltpu.VMEM((n_k, n_rows, lanes), F32)],
        compiler_params=_cparams("arbitrary"),
        name="rwkv_scan",
    )(kk_e, w_e, kka_e, k_e, r_e, v_e)


def _rwkv_out_kernel(*refs, n_heads, n_head, dup):
    y_ref, bonus_ref = refs[:2]
    bsz = bonus_ref.shape[0]
    gate_refs = refs[2:2 + bsz]
    gw_ref, gb_ref, g_ref, o_ref = refs[2 + bsz:]
    t = y_ref.shape[0]
    e = jnp.swapaxes(jnp.transpose(y_ref[...], (1, 0, 2)), 1, 2)
    bh = bsz * n_heads
    g = g_ref[...] * (1.0 / n_head)
    for b in range(bsz):
        rows = jnp.concatenate(
            [e[:, d * bh + b * n_heads:d * bh + (b + 1) * n_heads, :] for d in range(dup)], axis=0)
        y = jnp.transpose(rows, (1, 0, 2)).reshape(n_heads * n_head, t).T
        yc = y - _seg_sum(y, g)
        var = _seg_sum(yc * yc, g)
        yn = yc * lax.rsqrt(var + RWKV_GN_EPS) * gw_ref[...] + gb_ref[...]
        o_ref[b] = ((yn + bonus_ref[b]) * _silu(gate_refs[b][...])).astype(o_ref.dtype)


def rwkv_out(y_e, bonus, proj, off_g, gn_w, gn_b, n_heads, n_head):
    bsz, s, w = bonus.shape
    n_rows = y_e.shape[1]
    dup = LANES // (bsz * n_heads)
    tt = _tile(s, LANES)
    nt = s // tt
    assert off_g % w == 0
    gmat = jnp.asarray(np.kron(np.eye(LANES // n_head), np.ones((n_head, n_head))), F32)
    row = pl.BlockSpec((1, w), lambda t: (0, 0))
    tok = pl.BlockSpec((bsz, tt, w), lambda t: (0, t, 0))
    gates = [pl.BlockSpec((tt, w), functools.partial(lambda t, b: (b * nt + t, off_g // w), b=b))
             for b in range(bsz)]
    return pl.pallas_call(
        functools.partial(_rwkv_out_kernel, n_heads=n_heads, n_head=n_head, dup=dup),
        grid=(nt,),
        in_specs=[pl.BlockSpec((tt, n_rows, LANES), lambda t: (t, 0, 0)), tok] + gates
                 + [row, row, pl.BlockSpec((LANES, LANES), lambda t: (0, 0))],
        out_specs=tok,
        out_shape=jax.ShapeDtypeStruct((bsz, s, w), BF16),
        compiler_params=_cparams("parallel"),
        name="rwkv_out",
    )(y_e, bonus, *[proj] * bsz, gn_w.reshape(1, w), gn_b.reshape(1, w), gmat).reshape(bsz * s, w)


def _mlstm_kernel(qk_ref, v_ref, o_ref, g_ref, if_ref, cw_ref, cb_ref, bias_ref, gw_ref, tril_ref,
                  out_ref, hist_sc, c_sc, m_sc, *, n_heads, dqk, dv):
    t = pl.program_id(1)

    @pl.when(t == 0)
    def _():
        hist_sc[...] = jnp.zeros_like(hist_sc)
        c_sc[...] = jnp.zeros_like(c_sc)
        m_sc[...] = jnp.full_like(m_sc, -jnp.inf)

    qk_in = qk_ref[...]
    rows = qk_in.shape[0]
    qk = _silu(_causal_conv(qk_in, hist_sc[...], cw_ref[...], cb_ref[...]))
    hist_sc[...] = qk_in[rows - SUBLANES:]

    pre = if_ref[...] + bias_ref[...]
    li = pre
    lf = jax.nn.log_sigmoid(pltpu.roll(pre, LANES - n_heads, axis=1))
    tril = tril_ref[...]
    bcum = jnp.dot(tril, lf, precision=HIGHEST, preferred_element_type=F32)
    g_tot = bcum[rows - 1:rows]
    m_prev = m_sc[...]
    le = g_tot - bcum + li
    m_new = jnp.maximum(g_tot + m_prev, jnp.max(le, axis=0, keepdims=True))
    keep = jnp.exp(g_tot + m_prev - m_new)
    wk = jnp.exp(le - m_new)
    m_sc[...] = m_new
    bcum_t = bcum.T
    li_t = li.T
    causal = tril > 0.5
    ones_blk = (lax.broadcasted_iota(jnp.int32, (rows, LANES), 1) == 0).astype(F32)

    for h in range(n_heads):
        q = qk[:, h * dqk:(h + 1) * dqk]
        k = qk[:, (n_heads + h) * dqk:(n_heads + h + 1) * dqk] * (dqk ** -0.5)
        v = v_ref[:, h * dv:(h + 1) * dv]
        v_aug = jnp.concatenate([v, ones_blk], axis=1).astype(BF16)
        b_col = bcum[:, h:h + 1]
        dmat = b_col - bcum_t[h:h + 1, :] + li_t[h:h + 1, :]
        dmat = jnp.where(causal, dmat, -jnp.inf)
        inter = b_col + m_prev[:, h:h + 1]
        m_t = jnp.maximum(inter, jnp.max(dmat, axis=-1, keepdims=True))
        w_intra = jnp.exp(dmat - m_t)
        w_inter = jnp.exp(inter - m_t)
        qb = q.astype(BF16)
        sc = lax.dot_general(qb, k.astype(BF16), (((1,), (1,)), ((), ())),
                             preferred_element_type=F32) * w_intra
        c_aug = c_sc[h]
        tot = (w_inter * jnp.dot(qb, c_aug.astype(BF16), preferred_element_type=F32)
               + jnp.dot(sc.astype(BF16), v_aug, preferred_element_type=F32))
        num = tot[:, :dv]
        den = tot[:, dv:dv + 1]
        hh = num / jnp.maximum(jnp.abs(den), jnp.exp(-m_t))
        kw = (k * wk[:, h:h + 1]).astype(BF16)
        c_sc[h] = keep[:, h:h + 1] * c_aug + lax.dot_general(
            kw, v_aug, (((0,), (0,)), ((), ())), preferred_element_type=F32)

        ho = hh * jax.nn.sigmoid(o_ref[:, h * dv:(h + 1) * dv])
        mu = jnp.mean(ho, axis=-1, keepdims=True)
        hc = ho - mu
        var = jnp.mean(hc * hc, axis=-1, keepdims=True)
        hn = hc * lax.rsqrt(var + MLSTM_GN_EPS) * gw_ref[:, h * dv:(h + 1) * dv]
        out_ref[:, h * dv:(h + 1) * dv] = (hn * _silu(g_ref[:, h * dv:(h + 1) * dv])).astype(out_ref.dtype)


def mlstm_branch(proj, proj_if, offs, bsz, s, conv_w, conv_b, b_i, b_f, gn_w, w_c):
    off_qk, off_v, off_o, off_g, off_if, if_w = offs
    n_heads = b_i.shape[0]
    qk2 = conv_w.shape[1]
    dqk = qk2 // 2 // n_heads
    dv = w_c // n_heads
    ll = _tile(s, 256)
    nt = s // ll
    assert off_qk % qk2 == 0 and off_v % w_c == 0 and off_o % w_c == 0 and off_g % w_c == 0
    assert off_if % if_w == 0 and if_w == LANES and 2 * n_heads <= LANES
    bias = jnp.pad(jnp.concatenate([b_i, b_f]), (0, LANES - 2 * n_heads)).reshape(1, LANES)
    tril = jnp.asarray(np.tril(np.ones((ll, ll))), F32)
    blk = lambda off, width: pl.BlockSpec((ll, width), lambda b, t: (b * nt + t, off // width))
    full = lambda shape: pl.BlockSpec(shape, lambda b, t: (0,) * len(shape))
    return pl.pallas_call(
        functools.partial(_mlstm_kernel, n_heads=n_heads, dqk=dqk, dv=dv),
        grid=(bsz, nt),
        in_specs=[blk(off_qk, qk2), blk(off_v, w_c), blk(off_o, w_c), blk(off_g, w_c), blk(off_if, if_w),
                  full(conv_w.shape), full((1, qk2)), full((1, LANES)), full((1, w_c)), full((ll, ll))],
        out_specs=pl.BlockSpec((ll, w_c), lambda b, t: (b * nt + t, 0)),
        out_shape=jax.ShapeDtypeStruct((bsz * s, w_c), BF16),
        scratch_shapes=[pltpu.VMEM((SUBLANES, qk2), F32),
                        pltpu.VMEM((n_heads, dqk, dv + LANES), F32),
                        pltpu.VMEM((1, LANES), F32)],
        compiler_params=_cparams("parallel", "arbitrary"),
        name="mlstm",
    )(proj, proj, proj, proj, proj_if, conv_w, conv_b.reshape(1, qk2), bias, gn_w.reshape(1, w_c), tril)


def _xattn_kernel(q_ref, gate_ref, kv_ref, o_ref, *, n_heads):
    w_x = q_ref.shape[1]
    for h in range(n_heads):
        sl = slice(h * DH_X, (h + 1) * DH_X)
        q = q_ref[:, sl].astype(BF16)
        k = kv_ref[:, sl].astype(BF16)
        v = kv_ref[:, w_x + h * DH_X:w_x + (h + 1) * DH_X].astype(BF16)
        logits = lax.dot_general(q, k, (((1,), (1,)), ((), ())), preferred_element_type=F32) * (DH_X ** -0.5)
        e = jnp.exp(logits - jnp.max(logits, axis=-1, keepdims=True))
        probs = e / jnp.sum(e, axis=-1, keepdims=True)
        o = jnp.dot(probs.astype(BF16), v, preferred_element_type=F32)
        o_ref[:, sl] = (o * _silu(gate_ref[:, sl])).astype(o_ref.dtype)


def xattn_branch(proj, off_q, off_g, bsz, s, kv, w_x):
    m_len = kv.shape[0] // bsz
    tt = _tile(s, 512)
    nt = s // tt
    assert off_q % w_x == 0 and off_g % w_x == 0 and w_x % DH_X == 0
    return pl.pallas_call(
        functools.partial(_xattn_kernel, n_heads=w_x // DH_X),
        grid=(bsz, nt),
        in_specs=[pl.BlockSpec((tt, w_x), lambda b, t: (b * nt + t, off_q // w_x)),
                  pl.BlockSpec((tt, w_x), lambda b, t: (b * nt + t, off_g // w_x)),
                  pl.BlockSpec((m_len, 2 * w_x), lambda b, t: (b, 0))],
        out_specs=pl.BlockSpec((tt, w_x), lambda b, t: (b * nt + t, 0)),
        out_shape=jax.ShapeDtypeStruct((bsz * s, w_x), BF16),
        compiler_params=_cparams("parallel", "parallel"),
        name="mem_xattn",
    )(proj, proj, kv)


def _round_up(n, m):
    return (n + m - 1) // m * m


def _proj_layout(sizes, w_b, tn, tn_small):
    src = np.concatenate([[0], np.cumsum(sizes)])
    seg = {name: (int(src[i]), int(sizes[i])) for i, name in enumerate(
        ("a_x", "a_g", "b_s", "b_g", "c_qk", "c_v", "c_o", "c_g", "c_if", "x_q", "x_g", "gates"))}
    b0 = seg["b_s"][0]
    seg["b_r"], seg["b_k"], seg["b_v"] = (b0, w_b), (b0 + w_b, w_b), (b0 + 2 * w_b, w_b)
    seg["b_lo"] = (b0 + 3 * w_b, seg["b_s"][1] - 3 * w_b)
    assert seg["x_g"][0] == seg["x_q"][0] + seg["x_q"][1]
    seg["x_qg"] = (seg["x_q"][0], seg["x_q"][1] + seg["x_g"][1])
    offs, main_src, small_src = {}, [], []
    for name in ("a_x", "a_g", "b_g", "c_qk", "c_v", "c_o", "c_g", "b_r", "b_k", "b_v", "gates", "x_qg"):
        start, width = seg[name]
        assert start % SUBLANES == 0 and width % tn == 0, (name, start, width)
        offs[name] = len(main_src) * tn
        main_src.extend(start + b * tn for b in range(width // tn))
    offs["x_q"], offs["x_g"] = offs["x_qg"], offs["x_qg"] + seg["x_q"][1]
    for name in ("b_lo", "c_if"):
        start, width = seg[name]
        assert start % SUBLANES == 0 and width <= tn_small and start + tn_small <= src[-1]
        offs[name] = len(small_src) * tn_small
        small_src.append(start)
    return offs, main_src, small_src


def _in_proj_kernel(src_ref, h_ref, w_ref, o_ref):
    del src_ref
    o_ref[...] = lax.dot_general(h_ref[...], w_ref[...].astype(BF16), (((1,), (1,)), ((), ())),
                                 preferred_element_type=F32)


def in_proj(h, w_in_stacked, layer, block_src, tm, tn, name):
    m, k = h.shape
    n_layers, _, c_in = w_in_stacked.shape
    assert c_in % SUBLANES == 0
    w_t = jnp.swapaxes(w_in_stacked, 1, 2).reshape(n_layers * c_in, k)
    rows8 = jnp.asarray([(layer * c_in + c) // SUBLANES for c in block_src], jnp.int32)
    tm = _tile(m, tm)
    n_blocks = len(block_src)
    grid_spec = pltpu.PrefetchScalarGridSpec(
        num_scalar_prefetch=1,
        grid=(n_blocks, m // tm),
        in_specs=[pl.BlockSpec((tm, k), lambda j, i, src: (i, 0)),
                  pl.BlockSpec((pl.Element(tn), pl.Element(k)), lambda j, i, src: (src[j] * SUBLANES, 0))],
        out_specs=pl.BlockSpec((tm, tn), lambda j, i, src: (i, j)))
    return pl.pallas_call(
        _in_proj_kernel,
        grid_spec=grid_spec,
        out_shape=jax.ShapeDtypeStruct((m, n_blocks * tn), F32),
        compiler_params=_cparams("parallel", "parallel"),
        name=name,
    )(rows8, h, w_t)


def _layer(x2d, memn_src, bsz, s, p, stacked, layer):
    d = x2d.shape[1]
    w_a = p["lru_lambda"].shape[0]
    n_heads_b, n_head_b = p["rwkv_r_k"].shape
    w_b = n_heads_b * n_head_b
    w_c = stacked["w_branch_c"].shape[1]
    w_x = stacked["w_branch_x"].shape[1]
    qk2 = p["mlstm_conv_w"].shape[1]
    n_heads_c = p["mlstm_b_i"].shape[0]
    b_shift_w = p["rwkv_mu"].shape[0]
    sizes = (w_a, w_a, b_shift_w, w_b, qk2, w_c, w_c, w_c, 2 * n_heads_c, w_x, w_x, 4 * d)
    lo_w = _round_up(b_shift_w - 3 * w_b, 256)
    tn = min(1024, w_a, w_b, w_c, 2 * w_x, qk2)
    offs, main_src, small_src = _proj_layout(sizes, w_b, tn, lo_w)

    h = rmsnorm(x2d, p["norm_g"], BF16)
    proj = in_proj(h, stacked["w_in"], layer, main_src, 512, tn, "in_proj")
    proj_small = in_proj(h, stacked["w_in"], layer, small_src, 1024, lo_w, "in_proj_small")

    y_a = lru_branch(proj, offs["a_x"], offs["a_g"], bsz, s, p["lru_conv_w"], p["lru_conv_b"],
                     p["lru_wa"], p["lru_ba"], p["lru_wx"], p["lru_bx"], p["lru_lambda"])

    kk_e, w_e, kka_e, k_e, r_e, v_e, bonus = rwkv_prep(
        proj, proj_small, (offs["b_r"], offs["b_k"], offs["b_v"], offs["b_lo"], lo_w), bsz, s, p["rwkv_mu"],
        p["rwkv_w0"], p["rwkv_w_up"], p["rwkv_a0"], p["rwkv_a_up"], p["rwkv_k_k"], p["rwkv_k_a"],
        p["rwkv_r_k"], n_heads_b, n_head_b)
    y_e = rwkv_scan(kk_e, w_e, kka_e, k_e, r_e, v_e)
    y_b = rwkv_out(y_e, bonus, proj, offs["b_g"], p["rwkv_gn_w"], p["rwkv_gn_b"], n_heads_b, n_head_b)

    y_c = mlstm_branch(proj, proj_small, (offs["c_qk"], offs["c_v"], offs["c_o"], offs["c_g"], offs["c_if"], LANES),
                       bsz, s, p["mlstm_conv_w"], p["mlstm_conv_b"], p["mlstm_b_i"], p["mlstm_b_f"],
                       p["mlstm_gn_w"], w_c)

    memn = rmsnorm(memn_src, p["mem_norm_g"], BF16)
    kv = matmul(memn, stacked["xattn_w_kv"], layer, F32, 512, 512, name="mem_kv")
    y_x = xattn_branch(proj, offs["x_q"], offs["x_g"], bsz, s, kv, w_x)

    merged = merge((y_a, y_b, y_c, y_x),
                   tuple(stacked[n].astype(BF16) for n in ("w_branch_a", "w_branch_b", "w_branch_c", "w_branch_x")),
                   layer, proj, offs["gates"], d)
    return matmul(merged, stacked["w_out"], layer, F32, 512, 1024, residual=x2d, name="out_proj")


_LAYER_PARAMS = (
    "norm_g", "mem_norm_g", "w_in", "lru_conv_w", "lru_conv_b", "lru_wa", "lru_ba", "lru_wx", "lru_bx",
    "lru_lambda", "rwkv_mu", "rwkv_w0", "rwkv_w_up", "rwkv_a0", "rwkv_a_up", "rwkv_k_k", "rwkv_k_a",
    "rwkv_r_k", "rwkv_gn_w", "rwkv_gn_b", "mlstm_conv_w", "mlstm_conv_b", "mlstm_b_i", "mlstm_b_f",
    "mlstm_gn_w", "xattn_w_kv", "w_branch_a", "w_branch_b", "w_branch_c", "w_branch_x", "w_out")


_STACKED_WEIGHTS = ("w_in", "xattn_w_kv", "w_branch_a", "w_branch_b", "w_branch_c", "w_branch_x", "w_out")


def kernel(x, mem, norm_g, mem_norm_g, w_in, lru_conv_w, lru_conv_b, lru_wa, lru_ba, lru_wx, lru_bx, lru_lambda, rwkv_mu, rwkv_w0, rwkv_w_up, rwkv_a0, rwkv_a_up, rwkv_k_k, rwkv_k_a, rwkv_r_k, rwkv_gn_w, rwkv_gn_b, mlstm_conv_w, mlstm_conv_b, mlstm_b_i, mlstm_b_f, mlstm_gn_w, xattn_w_kv, w_branch_a, w_branch_b, w_branch_c, w_branch_x, w_out, final_norm_g):
    stacked = dict(zip(_LAYER_PARAMS, (
        norm_g, mem_norm_g, w_in, lru_conv_w, lru_conv_b, lru_wa, lru_ba, lru_wx, lru_bx, lru_lambda,
        rwkv_mu, rwkv_w0, rwkv_w_up, rwkv_a0, rwkv_a_up, rwkv_k_k, rwkv_k_a, rwkv_r_k, rwkv_gn_w,
        rwkv_gn_b, mlstm_conv_w, mlstm_conv_b, mlstm_b_i, mlstm_b_f, mlstm_gn_w, xattn_w_kv,
        w_branch_a, w_branch_b, w_branch_c, w_branch_x, w_out)))
    bsz, s, d = x.shape
    x2d = x.reshape(bsz * s, d)
    mem2d = mem.reshape(-1, d)
    for layer in range(norm_g.shape[0]):
        params = {name: val[layer] for name, val in stacked.items() if name not in _STACKED_WEIGHTS}
        x2d = _layer(x2d, mem2d, bsz, s, params, stacked, layer)
    return rmsnorm(x2d, final_norm_g, x.dtype).reshape(bsz, s, d)
```

```python
import functools

import jax
import jax.numpy as jnp
import numpy as np
from jax import lax
from jax.experimental import pallas as pl
from jax.experimental.pallas import tpu as pltpu

F32 = jnp.float32
BF16 = jnp.bfloat16
HIGHEST = lax.Precision.HIGHEST

LANES = 128
SUBLANES = 8
VMEM_LIMIT = 56 * 1024 * 1024

RMS_EPS = 1e-6
LRU_C = 8.0
RWKV_GN_EPS = 64e-5
MLSTM_GN_EPS = 1e-6
DH_X = 128


def _cparams(*sem):
    return pltpu.CompilerParams(dimension_semantics=sem, vmem_limit_bytes=VMEM_LIMIT)


def _tile(n, want):
    t = min(n, want)
    assert n % t == 0, (n, want)
    return t


def _silu(x):
    return x * jax.nn.sigmoid(x)


def _rmsnorm_kernel(x_ref, g_ref, o_ref):
    x = x_ref[...]
    ms = jnp.mean(x * x, axis=-1, keepdims=True)
    o_ref[...] = (x * lax.rsqrt(ms + RMS_EPS) * g_ref[...]).astype(o_ref.dtype)


def rmsnorm(x2d, g, out_dtype):
    m, d = x2d.shape
    tm = _tile(m, 512)
    return pl.pallas_call(
        _rmsnorm_kernel,
        grid=(m // tm,),
        in_specs=[pl.BlockSpec((tm, d), lambda i: (i, 0)),
                  pl.BlockSpec((1, d), lambda i: (0, 0))],
        out_specs=pl.BlockSpec((tm, d), lambda i: (i, 0)),
        out_shape=jax.ShapeDtypeStruct((m, d), out_dtype),
        compiler_params=_cparams("parallel"),
        name="rmsnorm",
    )(x2d, g.reshape(1, d))


def _mm_kernel(a_ref, b_ref, o_ref):
    acc = jnp.dot(a_ref[...], b_ref[...].astype(BF16), preferred_element_type=F32)
    o_ref[...] = acc.astype(o_ref.dtype)


def _mm_res_kernel(a_ref, b_ref, r_ref, o_ref):
    acc = jnp.dot(a_ref[...], b_ref[...].astype(BF16), preferred_element_type=F32)
    o_ref[...] = (r_ref[...] + acc).astype(o_ref.dtype)


def matmul(a, b, layer, out_dtype, tm, tn, residual=None, name="matmul"):
    m, k = a.shape
    _, _, n = b.shape
    tm, tn = _tile(m, tm), _tile(n, tn)
    in_specs = [pl.BlockSpec((tm, k), lambda j, i: (i, 0)),
                pl.BlockSpec((None, k, tn), lambda j, i: (layer, 0, j))]
    args = [a, b]
    kern = _mm_kernel
    if residual is not None:
        in_specs.append(pl.BlockSpec((tm, tn), lambda j, i: (i, j)))
        args.append(residual)
        kern = _mm_res_kernel
    return pl.pallas_call(
        kern,
        grid=(n // tn, m // tm),
        in_specs=in_specs,
        out_specs=pl.BlockSpec((tm, tn), lambda j, i: (i, j)),
        out_shape=jax.ShapeDtypeStruct((m, n), out_dtype),
        compiler_params=_cparams("parallel", "parallel"),
        name=name,
    )(*args)


def _merge_kernel(ya_ref, yb_ref, yc_ref, yx_ref, wa_ref, wb_ref, wc_ref, wx_ref,
                  g0_ref, g1_ref, g2_ref, g3_ref, o_ref):
    def term(y_ref, w_ref, g_ref):
        return jax.nn.sigmoid(g_ref[...]) * jnp.dot(y_ref[...], w_ref[...], preferred_element_type=F32)

    acc = term(ya_ref, wa_ref, g0_ref)
    acc = acc + term(yb_ref, wb_ref, g1_ref)
    acc = acc + term(yc_ref, wc_ref, g2_ref)
    acc = acc + term(yx_ref, wx_ref, g3_ref)
    o_ref[...] = acc.astype(o_ref.dtype)


def merge(ys, ws, layer, proj, gate_off, d):
    m = proj.shape[0]
    tm, tn = _tile(m, 256), _tile(d, 1024)
    assert gate_off % tn == 0
    y_specs = [pl.BlockSpec((tm, y.shape[1]), lambda j, i: (i, 0)) for y in ys]
    w_specs = [pl.BlockSpec((None, w.shape[1], tn), lambda j, i: (layer, 0, j)) for w in ws]
    g_specs = [pl.BlockSpec((tm, tn), functools.partial(
        lambda j, i, base: (i, base + j), base=(gate_off + k * d) // tn)) for k in range(4)]
    return pl.pallas_call(
        _merge_kernel,
        grid=(d // tn, m // tm),
        in_specs=y_specs + w_specs + g_specs,
        out_specs=pl.BlockSpec((tm, tn), lambda j, i: (i, j)),
        out_shape=jax.ShapeDtypeStruct((m, d), BF16),
        compiler_params=_cparams("parallel", "parallel"),
        name="merge",
    )(*ys, *ws, proj, proj, proj, proj)


def _shift_rows(x, shift, hist):
    rolled = pltpu.roll(x, shift, axis=0)
    head = pltpu.roll(hist, shift, axis=0)
    row = lax.broadcasted_iota(jnp.int32, (SUBLANES, x.shape[1]), 0)
    fixed = jnp.where(row < shift, head, rolled[:SUBLANES])
    return jnp.concatenate([fixed, rolled[SUBLANES:]], axis=0)


def _causal_conv(x, hist, w, b):
    kw = w.shape[0]
    out = x * w[kw - 1:kw] + b
    for j in range(kw - 1):
        out = out + _shift_rows(x, kw - 1 - j, hist) * w[j:j + 1]
    return out


def _lru_kernel(ax_ref, ag_ref, cw_ref, cb_ref, wa_ref, ba_ref, wx_ref, bx_ref, lam_ref, o_ref,
                hist_sc, hcar_sc, a_sc, b_sc, h_sc):
    t = pl.program_id(2)

    @pl.when(t == 0)
    def _():
        hist_sc[...] = jnp.zeros_like(hist_sc)
        hcar_sc[...] = jnp.zeros_like(hcar_sc)

    x = ax_ref[...]
    rows = x.shape[0]
    u = _causal_conv(x, hist_sc[...], cw_ref[...], cb_ref[...])
    hist_sc[...] = x[rows - SUBLANES:]
    ub = u.astype(BF16)
    r = jax.nn.sigmoid(jnp.dot(ub, wa_ref[...], preferred_element_type=F32) + ba_ref[...])
    i = jax.nn.sigmoid(jnp.dot(ub, wx_ref[...], preferred_element_type=F32) + bx_ref[...])
    log_a = -LRU_C * r * jax.nn.softplus(-lam_ref[...])
    a = jnp.exp(log_a)
    b = jnp.sqrt(-jnp.tanh(log_a) * (a * a + 1.0)) * (i * u)

    in_group = lax.broadcasted_iota(jnp.int32, x.shape, 0) % SUBLANES
    for s in (1, 2, 4):
        keep = in_group >= s
        b = jnp.where(keep, a * pltpu.roll(b, s, axis=0) + b, b)
        a = jnp.where(keep, a * pltpu.roll(a, s, axis=0), a)
    a_sc[...] = a
    b_sc[...] = b

    def group(g, h_prev):
        sl = pl.ds(pl.multiple_of(g * SUBLANES, SUBLANES), SUBLANES)
        h8 = a_sc[sl, :] * h_prev + b_sc[sl, :]
        h_sc[sl, :] = h8
        return h8[SUBLANES - 1:SUBLANES]

    hcar_sc[...] = lax.fori_loop(0, rows // SUBLANES, group, hcar_sc[...], unroll=4)
    o_ref[...] = (h_sc[...] * _silu(ag_ref[...])).astype(o_ref.dtype)


def lru_branch(proj, off_x, off_g, bsz, s, cw, cb, wa, ba, wx, bx, lam):
    nb, bs, _ = wa.shape
    w = nb * bs
    tt = _tile(s, 1024)
    nt = s // tt
    assert off_x % bs == 0 and off_g % bs == 0
    vec = lambda v: v.reshape(1, w)
    vspec = pl.BlockSpec((1, bs), lambda b, n, t: (0, n))
    return pl.pallas_call(
        _lru_kernel,
        grid=(bsz, nb, nt),
        in_specs=[
            pl.BlockSpec((tt, bs), lambda b, n, t: (b * nt + t, off_x // bs + n)),
            pl.BlockSpec((tt, bs), lambda b, n, t: (b * nt + t, off_g // bs + n)),
            pl.BlockSpec((cw.shape[0], bs), lambda b, n, t: (0, n)),
            vspec,
            pl.BlockSpec((None, bs, bs), lambda b, n, t: (n, 0, 0)),
            vspec,
            pl.BlockSpec((None, bs, bs), lambda b, n, t: (n, 0, 0)),
            vspec,
            vspec,
        ],
        out_specs=pl.BlockSpec((tt, bs), lambda b, n, t: (b * nt + t, n)),
        out_shape=jax.ShapeDtypeStruct((bsz * s, w), BF16),
        scratch_shapes=[pltpu.VMEM((SUBLANES, bs), F32), pltpu.VMEM((1, bs), F32),
                        pltpu.VMEM((tt, bs), F32), pltpu.VMEM((tt, bs), F32), pltpu.VMEM((tt, bs), F32)],
        compiler_params=_cparams("parallel", "parallel", "arbitrary"),
        name="rg_lru",
    )(proj, proj, cw, vec(cb), wa.astype(BF16), vec(ba), wx.astype(BF16), vec(bx), vec(lam))


def _seg_sum(x, g):
    parts = [jnp.dot(x[:, c * LANES:(c + 1) * LANES], g, precision=HIGHEST, preferred_element_type=F32)
             for c in range(x.shape[1] // LANES)]
    return parts[0] if len(parts) == 1 else jnp.concatenate(parts, axis=1)


def _token_shift(p_ref, mu_ref, last_sc, b):
    p = p_ref[...]
    rows = p.shape[0]
    prev = _shift_rows(p, 1, last_sc[b])
    last_sc[b] = p[rows - SUBLANES:]
    return p + (prev - p) * mu_ref[...]


def _heads_major(x, n_heads, n_head):
    return jnp.transpose(x.T.reshape(n_heads, n_head, x.shape[0]), (1, 0, 2))


def _chain_keys(per_batch, dup):
    return jnp.swapaxes(jnp.concatenate(per_batch * dup, axis=1), 1, 2)


def _chain_rows(per_batch, dup):
    nr = per_batch[0].shape[0] // dup
    e = jnp.concatenate([p[d * nr:(d + 1) * nr] for d in range(dup) for p in per_batch], axis=1)
    return jnp.transpose(jnp.swapaxes(e, 1, 2), (1, 0, 2))


def _rwkv_keys_kernel(*refs, bsz, n_heads, n_head, dup):
    pk, plo = refs[:bsz], refs[bsz:2 * bsz]
    muk_ref, mulo_ref, w0_ref, wup_ref, a0_ref, aup_ref, kkw_ref, g_ref = refs[2 * bsz:2 * bsz + 8]
    kk_o, w_o, kka_o, lk_sc, llo_sc = refs[2 * bsz + 8:]

    @pl.when(pl.program_id(0) == 0)
    def _():
        lk_sc[...] = jnp.zeros_like(lk_sc)
        llo_sc[...] = jnp.zeros_like(llo_sc)

    kk_b, w_b, kka_b = [], [], []
    for b in range(bsz):
        k = _token_shift(pk[b], muk_ref, lk_sc, b)
        lo = _token_shift(plo[b], mulo_ref, llo_sc, b)
        w_pre = w0_ref[...] + jnp.dot(jnp.tanh(lo).astype(BF16), wup_ref[...], preferred_element_type=F32)
        decay = jnp.exp(-jnp.exp(-jax.nn.softplus(-w_pre) - 0.5))
        a = jax.nn.sigmoid(a0_ref[...] + jnp.dot(lo.astype(BF16), aup_ref[...], preferred_element_type=F32))
        kk = k * kkw_ref[...]
        kk = kk / jnp.maximum(jnp.sqrt(_seg_sum(kk * kk, g_ref[...])), 1e-12)
        kk_b.append(_heads_major(kk, n_heads, n_head))
        w_b.append(_heads_major(decay, n_heads, n_head))
        kka_b.append(_heads_major(kk * a, n_heads, n_head))
    kk_o[...] = _chain_keys(kk_b, dup)
    w_o[...] = _chain_keys(w_b, dup)
    kka_o[...] = _chain_keys(kka_b, dup)


def _rwkv_vals_kernel(*refs, bsz, n_heads, n_head, dup):
    pr, pk, pv, plo = (refs[i * bsz:(i + 1) * bsz] for i in range(4))
    mur_ref, muk_ref, muv_ref, mulo_ref, a0_ref, aup_ref, ka_ref, rk_ref, g_ref = refs[4 * bsz:4 * bsz + 9]
    k_o, r_o, v_o, bonus_o, lr_sc, lk_sc, lv_sc, llo_sc = refs[4 * bsz + 9:]

    @pl.when(pl.program_id(0) == 0)
    def _():
        for sc in (lr_sc, lk_sc, lv_sc, llo_sc):
            sc[...] = jnp.zeros_like(sc)

    k_b, r_b, v_b = [], [], []
    for b in range(bsz):
        r = _token_shift(pr[b], mur_ref, lr_sc, b)
        k = _token_shift(pk[b], muk_ref, lk_sc, b)
        v = _token_shift(pv[b], muv_ref, lv_sc, b)
        lo = _token_shift(plo[b], mulo_ref, llo_sc, b)
        a = jax.nn.sigmoid(a0_ref[...] + jnp.dot(lo.astype(BF16), aup_ref[...], preferred_element_type=F32))
        k2 = k * (1.0 + (a - 1.0) * ka_ref[...])
        bonus_o[b] = _seg_sum(r * k2 * rk_ref[...], g_ref[...]) * v
        k_b.append(_heads_major(k2, n_heads, n_head))
        r_b.append(_heads_major(r, n_heads, n_head))
        v_b.append(_heads_major(v, n_heads, n_head))
    k_o[...] = _chain_keys(k_b, dup)
    r_o[...] = _chain_keys(r_b, dup)
    v_o[...] = _chain_rows(v_b, dup)


def rwkv_prep(proj, proj_lo, offs, bsz, s, mu, w0, w_up, a0, a_up, k_k, k_a, r_k, n_heads, n_head):
    off_r, off_k, off_v, off_lo, lo_w = offs
    r_w, w = w_up.shape
    r_a = a_up.shape[0]
    tt = _tile(s, LANES)
    nt = s // tt
    assert off_r % w == 0 and off_k % w == 0 and off_v % w == 0 and off_lo % lo_w == 0
    assert LANES % n_head == 0 and w % LANES == 0 and LANES % (bsz * n_heads) == 0
    dup = LANES // (bsz * n_heads)
    assert n_head % dup == 0
    vec = lambda x: x.reshape(1, -1)
    mu_r, mu_k, mu_v = vec(mu[:w]), vec(mu[w:2 * w]), vec(mu[2 * w:3 * w])
    mu_lo = vec(jnp.pad(mu[3 * w:], (0, lo_w - r_w - r_a)))
    wup_pad = jnp.zeros((lo_w, w), F32).at[:r_w].set(w_up).astype(BF16)
    aup_pad = jnp.zeros((lo_w, w), F32).at[r_w:r_w + r_a].set(a_up).astype(BF16)
    gmat = jnp.asarray(np.kron(np.eye(LANES // n_head), np.ones((n_head, n_head))), F32)

    row_w = pl.BlockSpec((1, w), lambda t: (0, 0))
    row_lo = pl.BlockSpec((1, lo_w), lambda t: (0, 0))
    lora = pl.BlockSpec((lo_w, w), lambda t: (0, 0))
    gspec = pl.BlockSpec((LANES, LANES), lambda t: (0, 0))

    def blk(off, width):
        return [pl.BlockSpec((tt, width), functools.partial(lambda t, b: (b * nt + t, off // width), b=b))
                for b in range(bsz)]

    keys_spec = pl.BlockSpec((n_head, tt, LANES), lambda t: (0, t, 0))
    keys_shape = jax.ShapeDtypeStruct((n_head, s, LANES), F32)
    hist = lambda width: pltpu.VMEM((bsz, SUBLANES, width), F32)
    statics = dict(bsz=bsz, n_heads=n_heads, n_head=n_head, dup=dup)

    kk_e, w_e, kka_e = pl.pallas_call(
        functools.partial(_rwkv_keys_kernel, **statics),
        grid=(nt,),
        in_specs=blk(off_k, w) + blk(off_lo, lo_w) + [row_w, row_lo, row_w, lora, row_w, lora, row_w, gspec],
        out_specs=[keys_spec] * 3,
        out_shape=[keys_shape] * 3,
        scratch_shapes=[hist(w), hist(lo_w)],
        compiler_params=_cparams("arbitrary"),
        name="rwkv_prep_keys",
    )(*[proj] * bsz, *[proj_lo] * bsz, mu_k, mu_lo, vec(w0), wup_pad, vec(a0), aup_pad, vec(k_k), gmat)

    k_e, r_e, v_e, bonus = pl.pallas_call(
        functools.partial(_rwkv_vals_kernel, **statics),
        grid=(nt,),
        in_specs=(blk(off_r, w) + blk(off_k, w) + blk(off_v, w) + blk(off_lo, lo_w)
                  + [row_w, row_w, row_w, row_lo, row_w, lora, row_w, row_w, gspec]),
        out_specs=[keys_spec, keys_spec,
                   pl.BlockSpec((tt, n_head // dup, LANES), lambda t: (t, 0, 0)),
                   pl.BlockSpec((bsz, tt, w), lambda t: (0, t, 0))],
        out_shape=[keys_shape, keys_shape,
                   jax.ShapeDtypeStruct((s, n_head // dup, LANES), F32),
                   jax.ShapeDtypeStruct((bsz, s, w), F32)],
        scratch_shapes=[hist(w), hist(w), hist(w), hist(lo_w)],
        compiler_params=_cparams("arbitrary"),
        name="rwkv_prep_vals",
    )(*[proj] * (3 * bsz), *[proj_lo] * bsz, mu_r, mu_k, mu_v, mu_lo, vec(a0), aup_pad, vec(k_a), vec(r_k), gmat)
    return kk_e, w_e, kka_e, k_e, r_e, v_e, bonus


def _rwkv_scan_kernel(kk_ref, w_ref, kka_ref, k_ref, r_ref, v_ref, y_ref, s_sc):
    c = pl.program_id(0)
    steps = kk_ref.shape[1]
    n_k, n_rows, _ = s_sc.shape
    blocks = [pl.ds(b * SUBLANES, SUBLANES) for b in range(n_rows // SUBLANES)]

    @pl.when(c == 0)
    def _():
        s_sc[...] = jnp.zeros_like(s_sc)

    def tree_sum(parts):
        while len(parts) > 1:
            parts = [parts[i] + parts[i + 1] for i in range(0, len(parts), 2)]
        return parts[0]

    sa0 = []
    for blk in blocks:
        acc = [None, None]
        for j in range(n_k):
            term = s_sc[j, blk, :] * kk_ref[j, 0:1, :]
            acc[j % 2] = term if acc[j % 2] is None else acc[j % 2] + term
        sa0.append(acc[0] + acc[1])

    def step(t, sa, with_next):
        v = [v_ref[t, blk, :] for blk in blocks]
        yacc = [[None, None] for _ in blocks]
        zacc = [[None, None] for _ in blocks]
        row = pl.ds(t, 1)
        for j in range(n_k):
            w_j, kka_j, k_j, r_j = w_ref[j, row, :], kka_ref[j, row, :], k_ref[j, row, :], r_ref[j, row, :]
            kk_j = kk_ref[j, pl.ds(t + 1, 1), :] if with_next else None
            for b, blk in enumerate(blocks):
                st = s_sc[j, blk, :] * w_j - sa[b] * kka_j + v[b] * k_j
                s_sc[j, blk, :] = st
                ya = st * r_j
                yacc[b][j % 2] = ya if yacc[b][j % 2] is None else yacc[b][j % 2] + ya
                if with_next:
                    za = st * kk_j
                    zacc[b][j % 2] = za if zacc[b][j % 2] is None else zacc[b][j % 2] + za
        for b, blk in enumerate(blocks):
            y_ref[t, blk, :] = tree_sum(yacc[b])
        return tuple(tree_sum(z) for z in zacc) if with_next else None

    sa = lax.fori_loop(0, steps - 1, lambda t, sa: step(t, sa, True), tuple(sa0))
    step(steps - 1, sa, False)


def rwkv_scan(kk_e, w_e, kka_e, k_e, r_e, v_e):
    n_k, s, lanes = kk_e.shape
    n_rows = v_e.shape[1]
    assert n_rows % SUBLANES == 0
    tc = _tile(s, 64)
    vec_spec = pl.BlockSpec((n_k, tc, lanes), lambda c: (0, c, 0))
    row_spec = pl.BlockSpec((tc, n_rows, lanes), lambda c: (c, 0, 0))
    return pl.pallas_call(
        _rwkv_scan_kernel,
        grid=(s // tc,),
        in_specs=[vec_spec] * 5 + [row_spec],
        out_specs=row_spec,
        out_shape=jax.ShapeDtypeStruct((s, n_rows, lanes), F32),
        scratch_shapes=[pltpu.VMEM((n_k, n_rows, lanes), F32)],
        compiler_params=_cparams("arbitrary"),
        name="rwkv_scan",
    )(kk_e, w_e, kka_e, k_e, r_e, v_e)


def _rwkv_out_kernel(*refs, n_heads, n_head, dup):
    y_ref, bonus_ref = refs[:2]
    bsz = bonus_ref.shape[0]
    gate_refs = refs[2:2 + bsz]
    gw_ref, gb_ref, g_ref, o_ref = refs[2 + bsz:]
    t = y_ref.shape[0]
    e = jnp.swapaxes(jnp.transpose(y_ref[...], (1, 0, 2)), 1, 2)
    bh = bsz * n_heads
    g = g_ref[...] * (1.0 / n_head)
    for b in range(bsz):
        rows = jnp.concatenate(
            [e[:, d * bh + b * n_heads:d * bh + (b + 1) * n_heads, :] for d in range(dup)], axis=0)
        y = jnp.transpose(rows, (1, 0, 2)).reshape(n_heads * n_head, t).T
        yc = y - _seg_sum(y, g)
        var = _seg_sum(yc * yc, g)
        yn = yc * lax.rsqrt(var + RWKV_GN_EPS) * gw_ref[...] + gb_ref[...]
        o_ref[b] = ((yn + bonus_ref[b]) * _silu(gate_refs[b][...])).astype(o_ref.dtype)


def rwkv_out(y_e, bonus, proj, off_g, gn_w, gn_b, n_heads, n_head):
    bsz, s, w = bonus.shape
    n_rows = y_e.shape[1]
    dup = LANES // (bsz * n_heads)
    tt = _tile(s, LANES)
    nt = s // tt
    assert off_g % w == 0
    gmat = jnp.asarray(np.kron(np.eye(LANES // n_head), np.ones((n_head, n_head))), F32)
    row = pl.BlockSpec((1, w), lambda t: (0, 0))
    tok = pl.BlockSpec((bsz, tt, w), lambda t: (0, t, 0))
    gates = [pl.BlockSpec((tt, w), functools.partial(lambda t, b: (b * nt + t, off_g // w), b=b))
             for b in range(bsz)]
    return pl.pallas_call(
        functools.partial(_rwkv_out_kernel, n_heads=n_heads, n_head=n_head, dup=dup),
        grid=(nt,),
        in_specs=[pl.BlockSpec((tt, n_rows, LANES), lambda t: (t, 0, 0)), tok] + gates
                 + [row, row, pl.BlockSpec((LANES, LANES), lambda t: (0, 0))],
        out_specs=tok,
        out_shape=jax.ShapeDtypeStruct((bsz, s, w), BF16),
        compiler_params=_cparams("parallel"),
        name="rwkv_out",
    )(y_e, bonus, *[proj] * bsz, gn_w.reshape(1, w), gn_b.reshape(1, w), gmat).reshape(bsz * s, w)


def _mlstm_kernel(qk_ref, v_ref, o_ref, g_ref, if_ref, cw_ref, cb_ref, bias_ref, gw_ref, tril_ref,
                  out_ref, hist_sc, c_sc, m_sc, *, n_heads, dqk, dv):
    t = pl.program_id(1)

    @pl.when(t == 0)
    def _():
        hist_sc[...] = jnp.zeros_like(hist_sc)
        c_sc[...] = jnp.zeros_like(c_sc)
        m_sc[...] = jnp.full_like(m_sc, -jnp.inf)

    qk_in = qk_ref[...]
    rows = qk_in.shape[0]
    qk = _silu(_causal_conv(qk_in, hist_sc[...], cw_ref[...], cb_ref[...]))
    hist_sc[...] = qk_in[rows - SUBLANES:]

    pre = if_ref[...] + bias_ref[...]
    li = pre
    lf = jax.nn.log_sigmoid(pltpu.roll(pre, LANES - n_heads, axis=1))
    tril = tril_ref[...]
    bcum = jnp.dot(tril, lf, precision=HIGHEST, preferred_element_type=F32)
    g_tot = bcum[rows - 1:rows]
    m_prev = m_sc[...]
    le = g_tot - bcum + li
    m_new = jnp.maximum(g_tot + m_prev, jnp.max(le, axis=0, keepdims=True))
    keep = jnp.exp(g_tot + m_prev - m_new)
    wk = jnp.exp(le - m_new)
    m_sc[...] = m_new
    bcum_t = bcum.T
    li_t = li.T
    causal = tril > 0.5
    ones_blk = (lax.broadcasted_iota(jnp.int32, (rows, LANES), 1) == 0).astype(F32)

    for h in range(n_heads):
        q = qk[:, h * dqk:(h + 1) * dqk]
        k = qk[:, (n_heads + h) * dqk:(n_heads + h + 1) * dqk] * (dqk ** -0.5)
        v = v_ref[:, h * dv:(h + 1) * dv]
        v_aug = jnp.concatenate([v, ones_blk], axis=1).astype(BF16)
        b_col = bcum[:, h:h + 1]
        dmat = b_col - bcum_t[h:h + 1, :] + li_t[h:h + 1, :]
        dmat = jnp.where(causal, dmat, -jnp.inf)
        inter = b_col + m_prev[:, h:h + 1]
        m_t = jnp.maximum(inter, jnp.max(dmat, axis=-1, keepdims=True))
        w_intra = jnp.exp(dmat - m_t)
        w_inter = jnp.exp(inter - m_t)
        qb = q.astype(BF16)
        sc = lax.dot_general(qb, k.astype(BF16), (((1,), (1,)), ((), ())),
                             preferred_element_type=F32) * w_intra
        c_aug = c_sc[h]
        tot = (w_inter * jnp.dot(qb, c_aug.astype(BF16), preferred_element_type=F32)
               + jnp.dot(sc.astype(BF16), v_aug, preferred_element_type=F32))
        num = tot[:, :dv]
        den = tot[:, dv:dv + 1]
        hh = num / jnp.maximum(jnp.abs(den), jnp.exp(-m_t))
        kw = (k * wk[:, h:h + 1]).astype(BF16)
        c_sc[h] = keep[:, h:h + 1] * c_aug + lax.dot_general(
            kw, v_aug, (((0,), (0,)), ((), ())), preferred_element_type=F32)

        ho = hh * jax.nn.sigmoid(o_ref[:, h * dv:(h + 1) * dv])
        mu = jnp.mean(ho, axis=-1, keepdims=True)
        hc = ho - mu
        var = jnp.mean(hc * hc, axis=-1, keepdims=True)
        hn = hc * lax.rsqrt(var + MLSTM_GN_EPS) * gw_ref[:, h * dv:(h + 1) * dv]
        out_ref[:, h * dv:(h + 1) * dv] = (hn * _silu(g_ref[:, h * dv:(h + 1) * dv])).astype(out_ref.dtype)


def mlstm_branch(proj, proj_if, offs, bsz, s, conv_w, conv_b, b_i, b_f, gn_w, w_c):
    off_qk, off_v, off_o, off_g, off_if, if_w = offs
    n_heads = b_i.shape[0]
    qk2 = conv_w.shape[1]
    dqk = qk2 // 2 // n_heads
    dv = w_c // n_heads
    ll = _tile(s, 256)
    nt = s // ll
    assert off_qk % qk2 == 0 and off_v % w_c == 0 and off_o % w_c == 0 and off_g % w_c == 0
    assert off_if % if_w == 0 and if_w == LANES and 2 * n_heads <= LANES
    bias = jnp.pad(jnp.concatenate([b_i, b_f]), (0, LANES - 2 * n_heads)).reshape(1, LANES)
    tril = jnp.asarray(np.tril(np.ones((ll, ll))), F32)
    blk = lambda off, width: pl.BlockSpec((ll, width), lambda b, t: (b * nt + t, off // width))
    full = lambda shape: pl.BlockSpec(shape, lambda b, t: (0,) * len(shape))
    return pl.pallas_call(
        functools.partial(_mlstm_kernel, n_heads=n_heads, dqk=dqk, dv=dv),
        grid=(bsz, nt),
        in_specs=[blk(off_qk, qk2), blk(off_v, w_c), blk(off_o, w_c), blk(off_g, w_c), blk(off_if, if_w),
                  full(conv_w.shape), full((1, qk2)), full((1, LANES)), full((1, w_c)), full((ll, ll))],
        out_specs=pl.BlockSpec((ll, w_c), lambda b, t: (b * nt + t, 0)),
        out_shape=jax.ShapeDtypeStruct((bsz * s, w_c), BF16),
        scratch_shapes=[pltpu.VMEM((SUBLANES, qk2), F32),
                        pltpu.VMEM((n_heads, dqk, dv + LANES), F32),
                        pltpu.VMEM((1, LANES), F32)],
        compiler_params=_cparams("parallel", "arbitrary"),
        name="mlstm",
    )(proj, proj, proj, proj, proj_if, conv_w, conv_b.reshape(1, qk2), bias, gn_w.reshape(1, w_c), tril)


def _xattn_kernel(q_ref, gate_ref, kv_ref, o_ref, *, n_heads):
    w_x = q_ref.shape[1]
    for h in range(n_heads):
        sl = slice(h * DH_X, (h + 1) * DH_X)
        q = q_ref[:, sl].astype(BF16)
        k = kv_ref[:, sl].astype(BF16)
        v = kv_ref[:, w_x + h * DH_X:w_x + (h + 1) * DH_X].astype(BF16)
        logits = lax.dot_general(q, k, (((1,), (1,)), ((), ())), preferred_element_type=F32) * (DH_X ** -0.5)
        e = jnp.exp(logits - jnp.max(logits, axis=-1, keepdims=True))
        probs = e / jnp.sum(e, axis=-1, keepdims=True)
        o = jnp.dot(probs.astype(BF16), v, preferred_element_type=F32)
        o_ref[:, sl] = (o * _silu(gate_ref[:, sl])).astype(o_ref.dtype)


def xattn_branch(proj, off_q, off_g, bsz, s, kv, w_x):
    m_len = kv.shape[0] // bsz
    tt = _tile(s, 512)
    nt = s // tt
    assert off_q % w_x == 0 and off_g % w_x == 0 and w_x % DH_X == 0
    return pl.pallas_call(
        functools.partial(_xattn_kernel, n_heads=w_x // DH_X),
        grid=(bsz, nt),
        in_specs=[pl.BlockSpec((tt, w_x), lambda b, t: (b * nt + t, off_q // w_x)),
                  pl.BlockSpec((tt, w_x), lambda b, t: (b * nt + t, off_g // w_x)),
                  pl.BlockSpec((m_len, 2 * w_x), lambda b, t: (b, 0))],
        out_specs=pl.BlockSpec((tt, w_x), lambda b, t: (b * nt + t, 0)),
        out_shape=jax.ShapeDtypeStruct((bsz * s, w_x), BF16),
        compiler_params=_cparams("parallel", "parallel"),
        name="mem_xattn",
    )(proj, proj, kv)


def _round_up(n, m):
    return (n + m - 1) // m * m


def _proj_layout(sizes, w_b, tn, tn_small):
    src = np.concatenate([[0], np.cumsum(sizes)])
    seg = {name: (int(src[i]), int(sizes[i])) for i, name in enumerate(
        ("a_x", "a_g", "b_s", "b_g", "c_qk", "c_v", "c_o", "c_g", "c_if", "x_q", "x_g", "gates"))}
    b0 = seg["b_s"][0]
    seg["b_r"], seg["b_k"], seg["b_v"] = (b0, w_b), (b0 + w_b, w_b), (b0 + 2 * w_b, w_b)
    seg["b_lo"] = (b0 + 3 * w_b, seg["b_s"][1] - 3 * w_b)
    assert seg["x_g"][0] == seg["x_q"][0] + seg["x_q"][1]
    seg["x_qg"] = (seg["x_q"][0], seg["x_q"][1] + seg["x_g"][1])
    offs, main_src, small_src = {}, [], []
    for name in ("a_x", "a_g", "b_g", "c_qk", "c_v", "c_o", "c_g", "b_r", "b_k", "b_v", "gates", "x_qg"):
        start, width = seg[name]
        assert start % SUBLANES == 0 and width % tn == 0, (name, start, width)
        offs[name] = len(main_src) * tn
        main_src.extend(start + b * tn for b in range(width // tn))
    offs["x_q"], offs["x_g"] = offs["x_qg"], offs["x_qg"] + seg["x_q"][1]
    for name in ("b_lo", "c_if"):
        start, width = seg[name]
        assert start % SUBLANES == 0 and width <= tn_small and start + tn_small <= src[-1]
        offs[name] = len(small_src) * tn_small
        small_src.append(start)
    return offs, main_src, small_src


def _in_proj_kernel(src_ref, h_ref, w_ref, o_ref):
    del src_ref
    o_ref[...] = lax.dot_general(h_ref[...], w_ref[...].astype(BF16), (((1,), (1,)), ((), ())),
                                 preferred_element_type=F32)


def in_proj(h, w_in_stacked, layer, block_src, tm, tn, name):
    m, k = h.shape
    n_layers, _, c_in = w_in_stacked.shape
    assert c_in % SUBLANES == 0
    w_t = jnp.swapaxes(w_in_stacked, 1, 2).reshape(n_layers * c_in, k)
    rows8 = jnp.asarray([(layer * c_in + c) // SUBLANES for c in block_src], jnp.int32)
    tm = _tile(m, tm)
    n_blocks = len(block_src)
    grid_spec = pltpu.PrefetchScalarGridSpec(
        num_scalar_prefetch=1,
        grid=(n_blocks, m // tm),
        in_specs=[pl.BlockSpec((tm, k), lambda j, i, src: (i, 0)),
                  pl.BlockSpec((pl.Element(tn), pl.Element(k)), lambda j, i, src: (src[j] * SUBLANES, 0))],
        out_specs=pl.BlockSpec((tm, tn), lambda j, i, src: (i, j)))
    return pl.pallas_call(
        _in_proj_kernel,
        grid_spec=grid_spec,
        out_shape=jax.ShapeDtypeStruct((m, n_blocks * tn), F32),
        compiler_params=_cparams("parallel", "parallel"),
        name=name,
    )(rows8, h, w_t)


def _layer(x2d, memn_src, bsz, s, p, stacked, layer):
    d = x2d.shape[1]
    w_a = p["lru_lambda"].shape[0]
    n_heads_b, n_head_b = p["rwkv_r_k"].shape
    w_b = n_heads_b * n_head_b
    w_c = stacked["w_branch_c"].shape[1]
    w_x = stacked["w_branch_x"].shape[1]
    qk2 = p["mlstm_conv_w"].shape[1]
    n_heads_c = p["mlstm_b_i"].shape[0]
    b_shift_w = p["rwkv_mu"].shape[0]
    sizes = (w_a, w_a, b_shift_w, w_b, qk2, w_c, w_c, w_c, 2 * n_heads_c, w_x, w_x, 4 * d)
    lo_w = _round_up(b_shift_w - 3 * w_b, 256)
    tn = min(1024, w_a, w_b, w_c, 2 * w_x, qk2)
    offs, main_src, small_src = _proj_layout(sizes, w_b, tn, lo_w)

    h = rmsnorm(x2d, p["norm_g"], BF16)
    proj = in_proj(h, stacked["w_in"], layer, main_src, 512, tn, "in_proj")
    proj_small = in_proj(h, stacked["w_in"], layer, small_src, 1024, lo_w, "in_proj_small")

    y_a = lru_branch(proj, offs["a_x"], offs["a_g"], bsz, s, p["lru_conv_w"], p["lru_conv_b"],
                     p["lru_wa"], p["lru_ba"], p["lru_wx"], p["lru_bx"], p["lru_lambda"])

    kk_e, w_e, kka_e, k_e, r_e, v_e, bonus = rwkv_prep(
        proj, proj_small, (offs["b_r"], offs["b_k"], offs["b_v"], offs["b_lo"], lo_w), bsz, s, p["rwkv_mu"],
        p["rwkv_w0"], p["rwkv_w_up"], p["rwkv_a0"], p["rwkv_a_up"], p["rwkv_k_k"], p["rwkv_k_a"],
        p["rwkv_r_k"], n_heads_b, n_head_b)
    y_e = rwkv_scan(kk_e, w_e, kka_e, k_e, r_e, v_e)
    y_b = rwkv_out(y_e, bonus, proj, offs["b_g"], p["rwkv_gn_w"], p["rwkv_gn_b"], n_heads_b, n_head_b)

    y_c = mlstm_branch(proj, proj_small, (offs["c_qk"], offs["c_v"], offs["c_o"], offs["c_g"], offs["c_if"], LANES),
                       bsz, s, p["mlstm_conv_w"], p["mlstm_conv_b"], p["mlstm_b_i"], p["mlstm_b_f"],
                       p["mlstm_gn_w"], w_c)

    memn = rmsnorm(memn_src, p["mem_norm_g"], BF16)
    kv = matmul(memn, stacked["xattn_w_kv"], layer, F32, 512, 512, name="mem_kv")
    y_x = xattn_branch(proj, offs["x_q"], offs["x_g"], bsz, s, kv, w_x)

    merged = merge((y_a, y_b, y_c, y_x),
                   tuple(stacked[n].astype(BF16) for n in ("w_branch_a", "w_branch_b", "w_branch_c", "w_branch_x")),
                   layer, proj, offs["gates"], d)
    return matmul(merged, stacked["w_out"], layer, F32, 512, 1024, residual=x2d, name="out_proj")


_LAYER_PARAMS = (
    "norm_g", "mem_norm_g", "w_in", "lru_conv_w", "lru_conv_b", "lru_wa", "lru_ba", "lru_wx", "lru_bx",
    "lru_lambda", "rwkv_mu", "rwkv_w0", "rwkv_w_up", "rwkv_a0", "rwkv_a_up", "rwkv_k_k", "rwkv_k_a",
    "rwkv_r_k", "rwkv_gn_w", "rwkv_gn_b", "mlstm_conv_w", "mlstm_conv_b", "mlstm_b_i", "mlstm_b_f",
    "mlstm_gn_w", "xattn_w_kv", "w_branch_a", "w_branch_b", "w_branch_c", "w_branch_x", "w_out")


_STACKED_WEIGHTS = ("w_in", "xattn_w_kv", "w_branch_a", "w_branch_b", "w_branch_c", "w_branch_x", "w_out")


def kernel(x, mem, norm_g, mem_norm_g, w_in, lru_conv_w, lru_conv_b, lru_wa, lru_ba, lru_wx, lru_bx, lru_lambda, rwkv_mu, rwkv_w0, rwkv_w_up, rwkv_a0, rwkv_a_up, rwkv_k_k, rwkv_k_a, rwkv_r_k, rwkv_gn_w, rwkv_gn_b, mlstm_conv_w, mlstm_conv_b, mlstm_b_i, mlstm_b_f, mlstm_gn_w, xattn_w_kv, w_branch_a, w_branch_b, w_branch_c, w_branch_x, w_out, final_norm_g):
    stacked = dict(zip(_LAYER_PARAMS, (
        norm_g, mem_norm_g, w_in, lru_conv_w, lru_conv_b, lru_wa, lru_ba, lru_wx, lru_bx, lru_lambda,
        rwkv_mu, rwkv_w0, rwkv_w_up, rwkv_a0, rwkv_a_up, rwkv_k_k, rwkv_k_a, rwkv_r_k, rwkv_gn_w,
        rwkv_gn_b, mlstm_conv_w, mlstm_conv_b, mlstm_b_i, mlstm_b_f, mlstm_gn_w, xattn_w_kv,
        w_branch_a, w_branch_b, w_branch_c, w_branch_x, w_out)))
    bsz, s, d = x.shape
    x2d = x.reshape(bsz * s, d)
    mem2d = mem.reshape(-1, d)
    for layer in range(norm_g.shape[0]):
        params = {name: val[layer] for name, val in stacked.items() if name not in _STACKED_WEIGHTS}
        x2d = _layer(x2d, mem2d, bsz, s, params, stacked, layer)
    return rmsnorm(x2d, final_norm_g, x.dtype).reshape(bsz, s, d)
```
